```python
import math
import jax, jax.numpy as jnp
from jax import lax
import numpy as np

D_MODEL = 1024
BATCH = 32
SEQ = 256
DEPTH = 2
DEC_BATCH = 4
DEC_SEQ = 2048
PAST_LEN = 512

GRID_W = 64
HEAD_DIM = 64
N_Q_HEADS = 8
N_KV_HEADS = 2
Q_PER_KV = N_Q_HEADS // N_KV_HEADS
D_ATTN = N_Q_HEADS * HEAD_DIM
D_KV = N_KV_HEADS * HEAD_DIM
D_SSM = D_MODEL - D_ATTN
SSM_GROUP = 16
N_SSM_GROUPS = D_SSM // SSM_GROUP
SSM_STATE = 64
D_IN = D_ATTN + 2 * D_KV + D_SSM
D_MIX = D_ATTN + D_SSM
D_FF = ((8 * D_MODEL // 3 + 255) // 256) * 256
AXIS_DIM = HEAD_DIM // 2
ROPE_THETA = 10000.0
Q_BLOCK = 128
N_MOD = 6
EPS = 1e-6

kernel_name = "hymba_s5_gqa_prefix_dit_step"


def rmsnorm(x, g):
    xf = x.astype(jnp.float32)
    y = xf * lax.rsqrt(jnp.mean(xf * xf, axis=-1, keepdims=True) + EPS)
    return (y * g.astype(jnp.float32)).astype(x.dtype)


def axial_rope(n_tokens):
    rows = n_tokens // GRID_W
    row = jnp.repeat(jnp.arange(rows, dtype=jnp.float32), GRID_W)
    col = jnp.tile(jnp.arange(GRID_W, dtype=jnp.float32), rows)
    inv_freq = ROPE_THETA ** (-jnp.arange(0, AXIS_DIM, 2, dtype=jnp.float32) / AXIS_DIM)
    ang = jnp.concatenate([row[:, None] * inv_freq, col[:, None] * inv_freq], axis=-1)
    return jnp.cos(ang), jnp.sin(ang)


def apply_rope(x, cos, sin):
    b, l, h, d = x.shape
    xp = x.reshape(b, l, h, d // 2, 2)
    x0, x1 = xp[..., 0], xp[..., 1]
    cs = cos[None, :, None, :].astype(x.dtype)
    sn = sin[None, :, None, :].astype(x.dtype)
    out = jnp.stack([x0 * cs - x1 * sn, x0 * sn + x1 * cs], axis=-1)
    return out.reshape(b, l, h, d)


def block_attention(q, k, v):
    b, l = q.shape[0], q.shape[1]
    nb = l // Q_BLOCK
    qb = q.reshape(b, nb, Q_BLOCK, N_KV_HEADS, Q_PER_KV, HEAD_DIM).transpose(1, 0, 2, 3, 4, 5)
    scale = HEAD_DIM ** -0.5

    def one_block(q_blk):
        s = jnp.einsum("bqkgd,bskd->bkgqs", q_blk, k, preferred_element_type=jnp.float32) * scale
        p = jax.nn.softmax(s, axis=-1).astype(v.dtype)
        return jnp.einsum("bkgqs,bskd->bqkgd", p, v)

    out = lax.map(one_block, qb)
    return out.transpose(1, 0, 2, 3, 4, 5).reshape(b, l, D_ATTN)


def ssm_discretise(a_re, a_im, log_dt, b_re, b_im):
    dt = jnp.exp(log_dt.astype(jnp.float32))[:, None]
    ar = a_re.astype(jnp.float32)
    ai = a_im.astype(jnp.float32)
    mag = jnp.exp(ar * dt)
    lr = mag * jnp.cos(ai * dt)
    li = mag * jnp.sin(ai * dt)
    den = ar * ar + ai * ai
    zr = ((lr - 1.0) * ar + li * ai) / den
    zi = (li * ar - (lr - 1.0) * ai) / den
    br = b_re.astype(jnp.float32)
    bi = b_im.astype(jnp.float32)
    bbar_re = zr[..., None] * br - zi[..., None] * bi
    bbar_im = zr[..., None] * bi + zi[..., None] * br
    return lr, li, bbar_re, bbar_im


def _combine(e1, e2):
    a1r, a1i, b1r, b1i = e1
    a2r, a2i, b2r, b2i = e2
    return (a2r * a1r - a2i * a1i, a2r * a1i + a2i * a1r,
            a2r * b1r - a2i * b1i + b2r, a2r * b1i + a2i * b1r + b2i)


def ssm_direction(u, a_re, a_im, log_dt, b_re, b_im, c_re, c_im, h0, reverse):
    lr, li, bbr, bbi = ssm_discretise(a_re, a_im, log_dt, b_re, b_im)
    xr = jnp.einsum("blgh,gph->blgp", u, bbr)
    xi = jnp.einsum("blgh,gph->blgp", u, bbi)
    if h0 is not None:
        first = -1 if reverse else 0
        h0r, h0i = h0
        xr = xr.at[:, first].add(lr * h0r - li * h0i)
        xi = xi.at[:, first].add(lr * h0i + li * h0r)
    ar = jnp.broadcast_to(lr, xr.shape)
    ai = jnp.broadcast_to(li, xr.shape)
    _, _, hr, hi = lax.associative_scan(_combine, (ar, ai, xr, xi), reverse=reverse, axis=1)
    y = (jnp.einsum("blgp,ghp->blgh", hr, c_re.astype(jnp.float32))
         - jnp.einsum("blgp,ghp->blgh", hi, c_im.astype(jnp.float32)))
    return y, hr, hi


def ssm_mixer(u, p, h0, return_state):
    b, l, _ = u.shape
    uf = u.astype(jnp.float32).reshape(b, l, N_SSM_GROUPS, SSM_GROUP)
    y_sum = None
    finals = []
    for d, reverse in ((0, False), (1, True)):
        init = None if h0 is None else (h0[:, d, 0].astype(jnp.float32), h0[:, d, 1].astype(jnp.float32))
        y, hr, hi = ssm_direction(uf, p["a_re"][d], p["a_im"][d], p["log_dt"][d], p["b_re"][d],
                                  p["b_im"][d], p["c_re"][d], p["c_im"][d], init, reverse)
        y_sum = y if y_sum is None else y_sum + y
        if return_state:
            last = 0 if reverse else -1
            finals.append(jnp.stack([hr[:, last], hi[:, last]], axis=1))
    y = y_sum.reshape(b, l, D_SSM) + p["d_skip"].astype(jnp.float32) * uf.reshape(b, l, D_SSM)
    g = jax.nn.gelu(y, approximate=False).astype(u.dtype)
    out = g * jax.nn.sigmoid(g @ p["w_glu"] + p["b_glu"])
    if return_state:
        return out, jnp.stack(finals, axis=1)
    return out


def adaln(cond_act, w_mod, b_mod):
    m = (cond_act @ w_mod + b_mod).reshape(cond_act.shape[0], 1, N_MOD, D_MODEL)
    return tuple(m[:, :, i] for i in range(N_MOD))


def trunk_layer(x, mods, p, rope=None, ctx_k=None, ctx_v=None, h0=None):
    shift1, scale1, gate1, shift2, scale2, gate2 = mods
    b, l, _ = x.shape
    h = rmsnorm(x, p["norm1"]) * (1.0 + scale1) + shift1
    proj = h @ p["w_in"]
    q, k, v, u = jnp.split(proj, [D_ATTN, D_ATTN + D_KV, D_ATTN + 2 * D_KV], axis=-1)
    q = rmsnorm(q.reshape(b, l, N_Q_HEADS, HEAD_DIM), p["q_norm"])
    k = rmsnorm(k.reshape(b, l, N_KV_HEADS, HEAD_DIM), p["k_norm"])
    v = v.reshape(b, l, N_KV_HEADS, HEAD_DIM)
    is_context = rope is None
    if is_context:
        attn = block_attention(q, k, v)
        ssm_out, state = ssm_mixer(u, p, None, True)
    else:
        cos, sin = rope
        q = apply_rope(q, cos, sin)
        k_lat = apply_rope(k, cos, sin)
        k_all = jnp.concatenate([ctx_k.astype(k.dtype), k_lat], axis=1)
        v_all = jnp.concatenate([ctx_v.astype(v.dtype), v], axis=1)
        attn = block_attention(q, k_all, v_all)
        ssm_out = ssm_mixer(u, p, h0, False)
    mixed = jnp.concatenate([attn, ssm_out], axis=-1) @ p["w_out"]
    x = x + gate1 * mixed
    h2 = rmsnorm(x, p["norm2"]) * (1.0 + scale2) + shift2
    gate, up = jnp.split(h2 @ p["w_ffn_in"], [D_FF], axis=-1)
    x = x + gate2 * ((jax.nn.silu(gate) * up) @ p["w_ffn_out"])
    if is_context:
        return x, k, v, state
    return x


def setup_inputs(seed: int = 0) -> dict:
    key = jax.random.key(seed)
    ks = iter(jax.random.split(key, 40))
    f32 = jnp.float32
    G, P, H = N_SSM_GROUPS, SSM_STATE, SSM_GROUP

    def nrm(shape, scale):
        return jax.random.normal(next(ks), shape, f32) * scale

    n_idx = jnp.arange(P, dtype=f32)
    return {
        "x_prompt": nrm((BATCH, SEQ, D_MODEL), 1.0),
        "x_sample": nrm((DEC_BATCH, DEC_SEQ, D_MODEL), 1.0),
        "cache_k": nrm((DEC_BATCH, DEPTH, PAST_LEN, N_KV_HEADS, HEAD_DIM), 1.0),
        "cache_v": nrm((DEC_BATCH, DEPTH, PAST_LEN, N_KV_HEADS, HEAD_DIM), 1.0),
        "state_ssm": nrm((DEC_BATCH, DEPTH, 2, 2, G, P), 0.5),
        "c": nrm((DEC_BATCH, D_MODEL), 1.0),
        "c_ctx": nrm((D_MODEL,), 1.0),
        "w_mod": nrm((DEPTH, D_MODEL, N_MOD * D_MODEL), 0.5 * D_MODEL ** -0.5),
        "b_mod": nrm((DEPTH, N_MOD * D_MODEL), 0.01),
        "norm1": 1.0 + nrm((DEPTH, D_MODEL), 0.02),
        "norm2": 1.0 + nrm((DEPTH, D_MODEL), 0.02),
        "w_in": nrm((DEPTH, D_MODEL, D_IN), D_MODEL ** -0.5),
        "q_norm": 1.0 + nrm((DEPTH, HEAD_DIM), 0.02),
        "k_norm": 1.0 + nrm((DEPTH, HEAD_DIM), 0.02),
        "ssm_a_re": -0.5 + nrm((DEPTH, 2, G, P), 0.01),
        "ssm_a_im": math.pi * n_idx + nrm((DEPTH, 2, G, P), 0.01),
        "ssm_log_dt": jax.random.uniform(next(ks), (DEPTH, 2, G), f32, math.log(1e-3), math.log(1e-1)),
        "ssm_b_re": nrm((DEPTH, 2, G, P, H), H ** -0.5),
        "ssm_b_im": nrm((DEPTH, 2, G, P, H), H ** -0.5),
        "ssm_c_re": nrm((DEPTH, 2, G, H, P), 0.5 * P ** -0.5),
        "ssm_c_im": nrm((DEPTH, 2, G, H, P), 0.5 * P ** -0.5),
        "ssm_d": nrm((DEPTH, D_SSM), 0.5),
        "w_glu": nrm((DEPTH, D_SSM, D_SSM), D_SSM ** -0.5),
        "b_glu": nrm((DEPTH, D_SSM), 0.01),
        "w_out": nrm((DEPTH, D_MIX, D_MODEL), D_MIX ** -0.5),
        "w_ffn_in": nrm((DEPTH, D_MODEL, 2 * D_FF), D_MODEL ** -0.5),
        "w_ffn_out": nrm((DEPTH, D_FF, D_MODEL), D_FF ** -0.5),
        "final_norm": 1.0 + nrm((D_MODEL,), 0.02),
    }


def reference(x_prompt, x_sample, cache_k, cache_v, state_ssm, c, c_ctx, w_mod, b_mod, norm1, norm2,
              w_in, q_norm, k_norm, ssm_a_re, ssm_a_im, ssm_log_dt, ssm_b_re, ssm_b_im, ssm_c_re,
              ssm_c_im, ssm_d, w_glu, b_glu, w_out, w_ffn_in, w_ffn_out, final_norm):
    rope = axial_rope(x_sample.shape[1])
    cond_ctx = jax.nn.silu(c_ctx)[None, :]
    cond_lat = jax.nn.silu(c)
    xp, xs = x_prompt, x_sample
    new_k, new_v, new_s = [], [], []
    for l in range(DEPTH):
        p = {
            "norm1": norm1[l], "norm2": norm2[l], "w_in": w_in[l],
            "q_norm": q_norm[l], "k_norm": k_norm[l],
            "a_re": ssm_a_re[l], "a_im": ssm_a_im[l], "log_dt": ssm_log_dt[l],
            "b_re": ssm_b_re[l], "b_im": ssm_b_im[l], "c_re": ssm_c_re[l], "c_im": ssm_c_im[l],
            "d_skip": ssm_d[l], "w_glu": w_glu[l], "b_glu": b_glu[l], "w_out": w_out[l],
            "w_ffn_in": w_ffn_in[l], "w_ffn_out": w_ffn_out[l],
        }
        xp, k_ctx, v_ctx, s_ctx = trunk_layer(xp, adaln(cond_ctx, w_mod[l], b_mod[l]), p)
        new_k.append(k_ctx)
        new_v.append(v_ctx)
        new_s.append(s_ctx)
        xs = trunk_layer(xs, adaln(cond_lat, w_mod[l], b_mod[l]), p, rope=rope,
                         ctx_k=cache_k[:, l], ctx_v=cache_v[:, l], h0=state_ssm[:, l])
    y_prompt = rmsnorm(xp, final_norm)
    y_sample = rmsnorm(xs, final_norm)
    new_cache_k = jnp.stack(new_k, axis=1)
    new_cache_v = jnp.stack(new_v, axis=1)
    new_state_ssm = jnp.stack(new_s, axis=1)
    return (y_prompt, y_sample, new_cache_k, new_cache_v, new_state_ssm)
```

```python
import functools
import math

import jax
import jax.numpy as jnp
from jax import lax
from jax.experimental import pallas as pl
from jax.experimental.pallas import tpu as pltpu

F32 = jnp.float32
BF16 = jnp.bfloat16

HEAD_DIM = 64
N_Q_HEADS = 8
N_KV_HEADS = 2
D_ATTN = N_Q_HEADS * HEAD_DIM
D_KV = N_KV_HEADS * HEAD_DIM
SSM_GROUP = 16
SSM_STATE = 64
GRID_W = 64
ROPE_THETA = 10000.0
N_MOD = 6
EPS = 1e-6

LANES = 128
SUBLANES = 8
SLAB_GROUPS = LANES // SSM_GROUP
SLAB_STATES = SLAB_GROUPS * SSM_STATE
VMEM_LIMIT = 56 * 1024 * 1024


def _const_spec(shape):
    nd = len(shape)
    return pl.BlockSpec(shape, lambda *_: (0,) * nd, pipeline_mode=pl.Buffered(1))


def _rms(x, gain):
    ms = jnp.mean(x * x, axis=-1, keepdims=True)
    return x * lax.rsqrt(ms + EPS) * gain


def _mods_kernel(cond_ref, w_ref, b_ref, o_ref):
    c = cond_ref[...]
    act = c * jax.nn.sigmoid(c)
    o_ref[0] = jnp.dot(act.astype(BF16), w_ref[0].astype(BF16), preferred_element_type=F32) + b_ref[0]


def _mods_call(cond, w_mod, b_mod):
    depth, d_model, n_out = w_mod.shape
    rows = cond.shape[0]
    nt = 4
    tn = n_out // nt
    return pl.pallas_call(
        _mods_kernel,
        grid=(depth, nt),
        in_specs=[
            pl.BlockSpec((rows, d_model), lambda l, j: (0, 0)),
            pl.BlockSpec((1, d_model, tn), lambda l, j: (l, 0, j)),
            pl.BlockSpec((1, 1, tn), lambda l, j: (l, 0, j)),
        ],
        out_specs=pl.BlockSpec((1, rows, tn), lambda l, j: (l, 0, j)),
        out_shape=jax.ShapeDtypeStruct((depth, rows, n_out), F32),
        compiler_params=pltpu.CompilerParams(
            dimension_semantics=("arbitrary", "arbitrary"), vmem_limit_bytes=VMEM_LIMIT),
        name="adaln_mods",
    )(cond, w_mod, b_mod.reshape(depth, 1, n_out))


def _disc_kernel(are_ref, aim_ref, ldt_ref, bre_ref, bim_ref, lr_ref, li_ref, bbr_ref, bbi_ref):
    dt = jnp.exp(ldt_ref[...])
    ar = are_ref[...]
    ai = aim_ref[...]
    mag = jnp.exp(ar * dt)
    lr = mag * jnp.cos(ai * dt)
    li = mag * jnp.sin(ai * dt)
    den = ar * ar + ai * ai
    zr = ((lr - 1.0) * ar + li * ai) / den
    zi = (li * ar - (lr - 1.0) * ai) / den
    br = bre_ref[...]
    bi = bim_ref[...]
    lr_ref[...] = lr
    li_ref[...] = li
    bbr_ref[...] = zr * br - zi * bi
    bbi_ref[...] = zr * bi + zi * br


def _disc_call(a_re, a_im, log_dt, b_re, b_im):
    depth, ndir, g, p, h = b_re.shape
    rows, cols = depth * ndir * g, p * h
    expand = lambda a: jnp.broadcast_to(a[..., None], (depth, ndir, g, p, h)).reshape(rows, cols)
    ldt = jnp.broadcast_to(log_dt[..., None, None], (depth, ndir, g, p, h)).reshape(rows, cols)
    out = jax.ShapeDtypeStruct((rows, cols), F32)
    lr, li, bbr, bbi = pl.pallas_call(
        _disc_kernel,
        out_shape=(out, out, out, out),
        compiler_params=pltpu.CompilerParams(vmem_limit_bytes=VMEM_LIMIT),
        name="ssm_discretise",
    )(expand(a_re), expand(a_im), ldt, b_re.reshape(rows, cols), b_im.reshape(rows, cols))
    shape5 = (depth, ndir, g, p, h)
    return (lr.reshape(shape5)[..., 0], li.reshape(shape5)[..., 0],
            bbr.reshape(shape5), bbi.reshape(shape5))


def _group_sumsq(z, ones_ref, width):
    z2 = z * z
    hi = z2.astype(BF16)
    lo = (z2 - hi.astype(F32)).astype(BF16)
    ones = ones_ref[0:width, 0:width]
    return (jnp.dot(hi, ones, preferred_element_type=F32)
            + jnp.dot(lo, ones, preferred_element_type=F32))


def _head_rms(z, gain, ones_ref):
    ss = _group_sumsq(z, ones_ref, z.shape[-1])
    return z * lax.rsqrt(ss * (1.0 / HEAD_DIM) + EPS) * gain


def _rope(z, cos, sin_signed):
    width = z.shape[-1]
    lane = lax.broadcasted_iota(jnp.int32, z.shape, 1)
    nxt = pltpu.roll(z, width - 1, axis=1)
    prv = pltpu.roll(z, 1, axis=1)
    partner = jnp.where((lane & 1) == 0, nxt, prv)
    return z * cos + partner * sin_signed


def _inproj_kernel(*refs, rope):
    if rope:
        (x_ref, mod_ref, g1_ref, w_ref, qg_ref, kg_ref, ones_ref, cos_ref, sin_ref,
         q_ref, k_ref, v_ref, u_ref) = refs
    else:
        (x_ref, mod_ref, g1_ref, w_ref, qg_ref, kg_ref, ones_ref,
         q_ref, k_ref, v_ref, u_ref) = refs
    mod = mod_ref[0]
    h = _rms(x_ref[...], g1_ref[...]) * (1.0 + mod[1:2]) + mod[0:1]
    proj = jnp.dot(h.astype(BF16), w_ref[...], preferred_element_type=F32)
    q = _head_rms(proj[:, :D_ATTN], qg_ref[...], ones_ref)
    k = _head_rms(proj[:, D_ATTN:D_ATTN + D_KV], kg_ref[...], ones_ref)
    if rope:
        q = _rope(q, cos_ref[...], sin_ref[...])
        k = _rope(k, cos_ref[:, 0:D_KV], sin_ref[:, 0:D_KV])
    q_ref[...] = (q * (HEAD_DIM ** -0.5)).astype(BF16)
    k_ref[...] = k
    v_ref[...] = proj[:, D_ATTN + D_KV:D_ATTN + 2 * D_KV]
    u_ref[...] = proj[:, D_ATTN + 2 * D_KV:]


def _inproj_call(x2d, mods, g1, w_in_b, qg, kg, ones_bd, rope_tabs, tokens_per_batch, tile):
    n_tok, d_model = x2d.shape
    d_in = w_in_b.shape[1]
    d_ssm = d_in - D_ATTN - 2 * D_KV
    tiles_per_batch = tokens_per_batch // tile
    if mods.shape[0] > 1:
        mod_map = lambda i: (i // tiles_per_batch, 0, 0)
    else:
        mod_map = lambda i: (0, 0, 0)
    in_specs = [
        pl.BlockSpec((tile, d_model), lambda i: (i, 0)),
        pl.BlockSpec((1, N_MOD, d_model), mod_map),
        _const_spec((1, d_model)),
        _const_spec((d_model, d_in)),
        _const_spec((1, D_ATTN)),
        _const_spec((1, D_KV)),
        _const_spec((D_ATTN, D_ATTN)),
    ]
    args = [x2d, mods, g1, w_in_b, qg, kg, ones_bd]
    if rope_tabs is not None:
        in_specs += [pl.BlockSpec((tile, D_ATTN), lambda i: (i % tiles_per_batch, 0))] * 2
        args += list(rope_tabs)
    row = lambda width: pl.BlockSpec((tile, width), lambda i: (i, 0))
    return pl.pallas_call(
        functools.partial(_inproj_kernel, rope=rope_tabs is not None),
        grid=(n_tok // tile,),
        in_specs=in_specs,
        out_specs=(row(D_ATTN), row(D_KV), row(D_KV), row(d_ssm)),
        out_shape=(jax.ShapeDtypeStruct((n_tok, D_ATTN), BF16),
                   jax.ShapeDtypeStruct((n_tok, D_KV), F32),
                   jax.ShapeDtypeStruct((n_tok, D_KV), F32),
                   jax.ShapeDtypeStruct((n_tok, d_ssm), F32)),
        compiler_params=pltpu.CompilerParams(
            dimension_semantics=("arbitrary",), vmem_limit_bytes=VMEM_LIMIT),
        name="inproj_rope" if rope_tabs is not None else "inproj",
    )(*args)


def _attn_kernel(*refs, has_cache):
    if has_cache:
        q_ref, kn_ref, vn_ref, ck_ref, cv_ref, o_ref, kvar, vvar = refs
    else:
        q_ref, kn_ref, vn_ref, o_ref, kvar, vvar = refs

    @pl.when(pl.program_id(1) == 0)
    def _():
        for new_ref, cache_ref, dst in ((kn_ref, ck_ref if has_cache else None, kvar),
                                        (vn_ref, cv_ref if has_cache else None, vvar)):
            src = new_ref[0]
            if has_cache:
                src = jnp.concatenate([cache_ref[0], src], axis=0)
            low = lax.broadcasted_iota(jnp.int32, src.shape, 1) < HEAD_DIM
            head0 = jnp.where(low, src, 0.0)
            head1 = jnp.where(low, 0.0, src)
            dst[0] = head0.astype(BF16)
            dst[1] = pltpu.roll(head0, HEAD_DIM, axis=1).astype(BF16)
            dst[2] = pltpu.roll(head1, HEAD_DIM, axis=1).astype(BF16)
            dst[3] = head1.astype(BF16)

    slabs = D_ATTN // LANES
    for slab in range(slabs):
        qs = q_ref[0, :, slab * LANES:(slab + 1) * LANES]
        kv = slab // (slabs // N_KV_HEADS)
        acc = None
        for half in range(2):
            idx = kv * 2 + half
            s = lax.dot_general(qs, kvar[idx], (((1,), (1,)), ((), ())), preferred_element_type=F32)
            m = jnp.max(s, axis=-1, keepdims=True)
            p = jnp.exp(s - m)
            denom = jnp.sum(p, axis=-1, keepdims=True)
            o = jnp.dot(p.astype(BF16), vvar[idx], preferred_element_type=F32) / denom
            acc = o if acc is None else acc + o
        o_ref[0, :, slab * LANES:(slab + 1) * LANES] = acc.astype(BF16)


def _attn_call(q3, k3, v3, ck3, cv3, tq):
    b, l, _ = q3.shape
    has_cache = ck3 is not None
    s_len = l + (ck3.shape[1] if has_cache else 0)
    in_specs = [
        pl.BlockSpec((1, tq, D_ATTN), lambda bi, qi: (bi, qi, 0)),
        pl.BlockSpec((1, l, D_KV), lambda bi, qi: (bi, 0, 0)),
        pl.BlockSpec((1, l, D_KV), lambda bi, qi: (bi, 0, 0)),
    ]
    args = [q3, k3, v3]
    if has_cache:
        in_specs += [pl.BlockSpec((1, ck3.shape[1], D_KV), lambda bi, qi: (bi, 0, 0))] * 2
        args += [ck3, cv3]
    return pl.pallas_call(
        functools.partial(_attn_kernel, has_cache=has_cache),
        grid=(b, l // tq),
        in_specs=in_specs,
        out_specs=pl.BlockSpec((1, tq, D_ATTN), lambda bi, qi: (bi, qi, 0)),
        out_shape=jax.ShapeDtypeStruct((b, l, D_ATTN), BF16),
        scratch_shapes=[pltpu.VMEM((4, s_len, D_KV), BF16), pltpu.VMEM((4, s_len, D_KV), BF16)],
        compiler_params=pltpu.CompilerParams(
            dimension_semantics=("arbitrary", "arbitrary"), vmem_limit_bytes=VMEM_LIMIT),
        name="attn_cache" if has_cache else "attn",
    )(*args)


def _ssm_kernel(uf_ref, ub_ref, h0_ref, lam_ref, wx_ref, wyr_ref, wyi_ref,
                yf_ref, yb_ref, ht_ref, xs, upad, ypad, carry, *, nb, nq, t_len, pitch):
    c = pl.program_id(1)
    n_chunks = pl.num_programs(1)
    half = SLAB_STATES // LANES
    rows_q = nb * pitch

    @pl.when(c == 0)
    def _():
        for d in range(2):
            for j in range(2 * half):
                carry[d, j] = h0_ref[d, j // half, :, (j % half) * LANES:(j % half + 1) * LANES]

    for d, u_ref in ((0, uf_ref), (1, ub_ref)):
        for q in range(nq):
            for b in range(nb):
                upad[q, pl.ds(b * pitch, t_len), :] = u_ref[b, :, q * LANES:(q + 1) * LANES]
                upad[q, pl.ds(b * pitch + t_len, pitch - t_len), :] = jnp.zeros(
                    (pitch - t_len, LANES), F32)
        for q in range(nq):
            lhs = upad[q].astype(BF16)
            for jp in range(half):
                xv = jnp.dot(lhs, wx_ref[d, q, :, jp * 2 * LANES:(jp + 1) * 2 * LANES],
                             preferred_element_type=F32)
                xs[d, 2 * jp, pl.ds(q * rows_q, rows_q), :] = xv[:, :LANES]
                xs[d, 2 * jp + 1, pl.ds(q * rows_q, rows_q), :] = xv[:, LANES:]

    lam = [[(lam_ref[d, 0, :, j * LANES:(j + 1) * LANES], lam_ref[d, 1, :, j * LANES:(j + 1) * LANES])
            for j in range(half)] for d in range(2)]

    def step(t, hs):
        out = []
        for d in range(2):
            row = t if d == 0 else t_len - 1 - t
            idx = pl.ds(row, SUBLANES, stride=pitch)
            for j in range(half):
                lr, li = lam[d][j]
                hr, hi = hs[d * 2 * half + j], hs[d * 2 * half + half + j]
                nr = lr * hr - li * hi + xs[d, j, idx, :]
                ni = lr * hi + li * hr + xs[d, half + j, idx, :]
                xs[d, j, idx, :] = nr
                xs[d, half + j, idx, :] = ni
                out.append((d * 2 * half + j, nr))
                out.append((d * 2 * half + half + j, ni))
        return tuple(v for _, v in sorted(out, key=lambda kv: kv[0]))

    init = tuple(carry[d, j] for d in range(2) for j in range(2 * half))
    final = lax.fori_loop(0, t_len, step, init, unroll=2)
    for d in range(2):
        for j in range(2 * half):
            carry[d, j] = final[d * 2 * half + j]

    @pl.when(c == n_chunks - 1)
    def _():
        for d in range(2):
            for j in range(2 * half):
                ht_ref[d, j // half, :, (j % half) * LANES:(j % half + 1) * LANES] = (
                    final[d * 2 * half + j])

    for d, y_ref in ((0, yf_ref), (1, yb_ref)):
        for q in range(nq):
            rows = pl.ds(q * rows_q, rows_q)
            h_re = jnp.concatenate([xs[d, j, rows, :] for j in range(half)], axis=1).astype(BF16)
            h_im = jnp.concatenate([xs[d, half + j, rows, :] for j in range(half)], axis=1).astype(BF16)
            ypad[...] = (jnp.dot(h_re, wyr_ref[d, q], preferred_element_type=F32)
                         - jnp.dot(h_im, wyi_ref[d, q], preferred_element_type=F32))
            for b in range(nb):
                y_ref[b, :, q * LANES:(q + 1) * LANES] = ypad[pl.ds(b * pitch, t_len), :]


def _ssm_call(u3, h0, lam, wx, wyr, wyi, nb, t_len):
    b, l, chans = u3.shape
    nq = SUBLANES // nb
    n_bg = b // nb
    n_sg = chans // (nq * LANES)
    n_rows = n_bg * n_sg
    n_chunks = l // t_len
    pitch = t_len + 4
    u_map_f = lambda r, c: (r // n_sg, c, r % n_sg)
    u_map_b = lambda r, c: (r // n_sg, n_chunks - 1 - c, r % n_sg)
    u_block = (nb, t_len, nq * LANES)
    st_block = (2, 2, SUBLANES, SLAB_STATES)
    tab = lambda shape: pl.BlockSpec((None,) + shape, lambda r, c: (r % n_sg,) + (0,) * len(shape))
    kern = functools.partial(_ssm_kernel, nb=nb, nq=nq, t_len=t_len, pitch=pitch)
    return pl.pallas_call(
        kern,
        grid=(n_rows, n_chunks),
        in_specs=[
            pl.BlockSpec(u_block, u_map_f),
            pl.BlockSpec(u_block, u_map_b),
            pl.BlockSpec(st_block, lambda r, c: (0, 0, r, 0)),
            tab((2, 2, SUBLANES, SLAB_STATES)),
            tab((2, nq, LANES, 2 * SLAB_STATES)),
            tab((2, nq, SLAB_STATES, LANES)),
            tab((2, nq, SLAB_STATES, LANES)),
        ],
        out_specs=(
            pl.BlockSpec(u_block, u_map_f),
            pl.BlockSpec(u_block, u_map_b),
            pl.BlockSpec(st_block, lambda r, c: (0, 0, r, 0)),
        ),
        out_shape=(jax.ShapeDtypeStruct(u3.shape, F32),
                   jax.ShapeDtypeStruct(u3.shape, F32),
                   jax.ShapeDtypeStruct((2, 2, n_rows * SUBLANES, SLAB_STATES), F32)),
        scratch_shapes=[
            pltpu.VMEM((2, 2 * SLAB_STATES // LANES, SUBLANES * pitch, LANES), F32),
            pltpu.VMEM((nq, nb * pitch, LANES), F32),
            pltpu.VMEM((nb * pitch, LANES), F32),
            pltpu.VMEM((2, 2 * SLAB_STATES // LANES, SUBLANES, LANES), F32),
        ],
        compiler_params=pltpu.CompilerParams(
            dimension_semantics=("arbitrary", "arbitrary"), vmem_limit_bytes=VMEM_LIMIT),
        name=f"ssm_scan_nb{nb}",
    )(u3, u3, h0, lam, wx, wyr, wyi)


def _ssm_tables(lr, li, bbr, bbi, c_re, c_im, nb):
    ndir, g, p, h = bbr.shape
    n_slab = g // SLAB_GROUPS
    nq = SUBLANES // nb
    n_sg = n_slab // nq
    eye = jnp.eye(SLAB_GROUPS, dtype=F32)

    def lam_rows(a):
        a = a.reshape(ndir, n_sg, nq, 1, SLAB_STATES)
        a = jnp.broadcast_to(a, (ndir, n_sg, nq, nb, SLAB_STATES))
        return a.reshape(ndir, n_sg, SUBLANES, SLAB_STATES).transpose(1, 0, 2, 3)

    lam = jnp.stack([lam_rows(lr), lam_rows(li)], axis=2)

    def in_blockdiag(bb):
        bb = bb.reshape(ndir, n_slab, SLAB_GROUPS, p, h)
        m = jnp.einsum("dsgph,gk->dsghkp", bb, eye)
        return m.reshape(ndir, n_slab, SLAB_GROUPS * h, SLAB_GROUPS * p)

    wx = jnp.concatenate([in_blockdiag(bbr), in_blockdiag(bbi)], axis=-1)
    wx = wx.reshape(ndir, n_sg, nq, LANES, 2 * SLAB_STATES).transpose(1, 0, 2, 3, 4).astype(BF16)

    def out_blockdiag(cc):
        cc = cc.astype(F32).reshape(ndir, n_slab, SLAB_GROUPS, h, p)
        m = jnp.einsum("dsghp,gk->dsgpkh", cc, eye)
        m = m.reshape(ndir, n_sg, nq, SLAB_GROUPS * p, SLAB_GROUPS * h)
        return m.transpose(1, 0, 2, 3, 4).astype(BF16)

    return lam, wx, out_blockdiag(c_re), out_blockdiag(c_im)


def _mix_ffn_kernel(attn_ref, yf_ref, yb_ref, u_ref, x_ref, mod_ref, dskip_ref, wglu_ref, bglu_ref,
                    wout_ref, g2_ref, wffi_ref, wffo_ref, fn_ref, o_ref, *, final, ff_chunks):
    d_ff = wffo_ref.shape[0]
    y = (yf_ref[...] + yb_ref[...]) + dskip_ref[...] * u_ref[...]
    g = 0.5 * y * (1.0 + lax.erf(y * (2.0 ** -0.5)))
    z = jnp.dot(g.astype(BF16), wglu_ref[...], preferred_element_type=F32) + bglu_ref[...]
    ssm_out = g * jax.nn.sigmoid(z)
    mixed = (jnp.dot(attn_ref[...], wout_ref[0:D_ATTN, :], preferred_element_type=F32)
             + jnp.dot(ssm_out.astype(BF16), wout_ref[D_ATTN:, :], preferred_element_type=F32))
    mod = mod_ref[0]
    x1 = x_ref[...] + mod[2:3] * mixed
    h2 = (_rms(x1, g2_ref[...]) * (1.0 + mod[4:5]) + mod[3:4]).astype(BF16)
    fc = d_ff // ff_chunks
    acc = None
    for ci in range(ff_chunks):
        gate = jnp.dot(h2, wffi_ref[:, ci * fc:(ci + 1) * fc], preferred_element_type=F32)
        up = jnp.dot(h2, wffi_ref[:, d_ff + ci * fc:d_ff + (ci + 1) * fc], preferred_element_type=F32)
        act = (gate * jax.nn.sigmoid(gate) * up).astype(BF16)
        part = jnp.dot(act, wffo_ref[ci * fc:(ci + 1) * fc, :], preferred_element_type=F32)
        acc = part if acc is None else acc + part
    x2 = x1 + mod[5:6] * acc
    if final:
        x2 = _rms(x2, fn_ref[...])
    o_ref[...] = x2


def _mix_ffn_call(attn, yf, yb, u, x2d, mods, dskip, wglu_b, bglu, wout_b, g2, wffi_b, wffo_b, fnorm,
                  tokens_per_batch, tile, final):
    n_tok, d_model = x2d.shape
    d_ssm = u.shape[1]
    d_ff = wffo_b.shape[0]
    tiles_per_batch = tokens_per_batch // tile
    if mods.shape[0] > 1:
        mod_map = lambda i: (i // tiles_per_batch, 0, 0)
    else:
        mod_map = lambda i: (0, 0, 0)
    row = lambda width: pl.BlockSpec((tile, width), lambda i: (i, 0))
    return pl.pallas_call(
        functools.partial(_mix_ffn_kernel, final=final, ff_chunks=2),
        grid=(n_tok // tile,),
        in_specs=[
            row(D_ATTN), row(d_ssm), row(d_ssm), row(d_ssm), row(d_model),
            pl.BlockSpec((1, N_MOD, d_model), mod_map),
            _const_spec((1, d_ssm)),
            _const_spec((d_ssm, d_ssm)),
            _const_spec((1, d_ssm)),
            _const_spec((D_ATTN + d_ssm, d_model)),
            _const_spec((1, d_model)),
            _const_spec((d_model, 2 * d_ff)),
            _const_spec((d_ff, d_model)),
            _const_spec((1, d_model)),
        ],
        out_specs=row(d_model),
        out_shape=jax.ShapeDtypeStruct((n_tok, d_model), F32),
        compiler_params=pltpu.CompilerParams(
            dimension_semantics=("arbitrary",), vmem_limit_bytes=VMEM_LIMIT),
        name="mix_ffn_final" if final else "mix_ffn",
    )(attn, yf, yb, u, x2d, mods, dskip, wglu_b, bglu, wout_b, g2, wffi_b, wffo_b, fnorm)


def _rope_tables(n_tokens):
    axis_dim = HEAD_DIM // 2
    rows = n_tokens // GRID_W
    row = jnp.repeat(jnp.arange(rows, dtype=F32), GRID_W)
    col = jnp.tile(jnp.arange(GRID_W, dtype=F32), rows)
    inv_freq = ROPE_THETA ** (-jnp.arange(0, axis_dim, 2, dtype=F32) / axis_dim)
    ang = jnp.concatenate([row[:, None] * inv_freq, col[:, None] * inv_freq], axis=-1)
    cos = jnp.repeat(jnp.cos(ang), 2, axis=-1)
    sin = jnp.repeat(jnp.sin(ang), 2, axis=-1) * jnp.tile(jnp.array([-1.0, 1.0], F32), HEAD_DIM // 2)
    return jnp.tile(cos, (1, N_Q_HEADS)), jnp.tile(sin, (1, N_Q_HEADS))


def _states_to_rows(st, nb):
    b = st.shape[0]
    nq = SUBLANES // nb
    n_slab = st.shape[3] // SLAB_GROUPS
    s = st.reshape(b // nb, nb, 2, 2, n_slab // nq, nq, SLAB_STATES)
    return s.transpose(2, 3, 0, 4, 5, 1, 6).reshape(2, 2, -1, SLAB_STATES)


def _rows_to_states(rows, batch, nb, n_groups):
    nq = SUBLANES // nb
    n_slab = n_groups // SLAB_GROUPS
    s = rows.reshape(2, 2, batch // nb, n_slab // nq, nq, nb, SLAB_STATES)
    return s.transpose(2, 5, 0, 1, 3, 4, 6).reshape(batch, 2, 2, n_groups, SSM_STATE)


def kernel(x_prompt, x_sample, cache_k, cache_v, state_ssm, c, c_ctx, w_mod, b_mod, norm1, norm2, w_in, q_norm, k_norm, ssm_a_re, ssm_a_im, ssm_log_dt, ssm_b_re, ssm_b_im, ssm_c_re, ssm_c_im, ssm_d, w_glu, b_glu, w_out, w_ffn_in, w_ffn_out, final_norm):
    batch, seq, d_model = x_prompt.shape
    dec_batch, dec_seq, _ = x_sample.shape
    depth = w_in.shape[0]
    past = cache_k.shape[2]
    n_groups = ssm_a_re.shape[2]
    d_ssm = n_groups * SSM_GROUP

    cond = jnp.zeros((SUBLANES, d_model), F32).at[0].set(c_ctx).at[1:1 + dec_batch].set(c)
    mods = _mods_call(cond, w_mod, b_mod).reshape(depth, SUBLANES, N_MOD, d_model)
    lr, li, bbr, bbi = _disc_call(ssm_a_re, ssm_a_im, ssm_log_dt, ssm_b_re, ssm_b_im)
    rope_tabs = _rope_tables(dec_seq)
    head_ids = jnp.arange(D_ATTN) // HEAD_DIM
    ones_bd = (head_ids[:, None] == head_ids[None, :]).astype(BF16)

    ctx_nb, lat_nb = SUBLANES, dec_batch
    xp = x_prompt.reshape(batch * seq, d_model)
    xs = x_sample.reshape(dec_batch * dec_seq, d_model)
    zero_state = jnp.zeros((2, 2, batch * n_groups // SLAB_GROUPS, SLAB_STATES), F32)
    new_k, new_v, new_s = [], [], []
    for l in range(depth):
        w_in_b = w_in[l].astype(BF16)
        wglu_b = w_glu[l].astype(BF16)
        wout_b = w_out[l].astype(BF16)
        wffi_b = w_ffn_in[l].astype(BF16)
        wffo_b = w_ffn_out[l].astype(BF16)
        g1 = norm1[l].reshape(1, d_model)
        g2 = norm2[l].reshape(1, d_model)
        qg = jnp.tile(q_norm[l], N_Q_HEADS).reshape(1, D_ATTN)
        kg = jnp.tile(k_norm[l], N_KV_HEADS).reshape(1, D_KV)
        dskip = ssm_d[l].reshape(1, d_ssm)
        bglu = b_glu[l].reshape(1, d_ssm)
        fnorm = final_norm.reshape(1, d_model)
        final = l == depth - 1
        for is_ctx in (True, False):
            if is_ctx:
                x2d, n_b, n_l, nb, tile, tq, t_len = xp, batch, seq, ctx_nb, 512, seq, seq
                layer_mods, tabs, ck, cv, h0 = mods[l, 0:1], None, None, None, zero_state
            else:
                x2d, n_b, n_l, nb, tile, tq, t_len = xs, dec_batch, dec_seq, lat_nb, 512, 512, 256
                layer_mods, tabs = mods[l, 1:1 + dec_batch], rope_tabs
                ck = cache_k[:, l].reshape(dec_batch, past, D_KV)
                cv = cache_v[:, l].reshape(dec_batch, past, D_KV)
                h0 = _states_to_rows(state_ssm[:, l], nb)
            tokens_per_batch = n_l if not is_ctx else batch * seq
            q, k, v, u = _inproj_call(x2d, layer_mods, g1, w_in_b, qg, kg, ones_bd, tabs,
                                      tokens_per_batch, tile)
            attn = _attn_call(q.reshape(n_b, n_l, D_ATTN), k.reshape(n_b, n_l, D_KV),
                              v.reshape(n_b, n_l, D_KV), ck, cv, tq)
            lam, wx, wyr, wyi = _ssm_tables(lr[l], li[l], bbr[l], bbi[l], ssm_c_re[l], ssm_c_im[l], nb)
            yf, yb, ht = _ssm_call(u.reshape(n_b, n_l, d_ssm), h0, lam, wx, wyr, wyi, nb, t_len)
            x_new = _mix_ffn_call(attn.reshape(-1, D_ATTN), yf.reshape(-1, d_ssm), yb.reshape(-1, d_ssm),
                                  u, x2d, layer_mods, dskip, wglu_b, bglu, wout_b, g2, wffi_b, wffo_b,
                                  fnorm, tokens_per_batch, 256, final)
            if is_ctx:
                xp = x_new
                new_k.append(k.reshape(batch, seq, N_KV_HEADS, HEAD_DIM))
                new_v.append(v.reshape(batch, seq, N_KV_HEADS, HEAD_DIM))
                new_s.append(_rows_to_states(ht, batch, nb, n_groups))
            else:
                xs = x_new
    return (xp.reshape(batch, seq, d_model), xs.reshape(dec_batch, dec_seq, d_model),
            jnp.stack(new_k, axis=1), jnp.stack(new_v, axis=1), jnp.stack(new_s, axis=1))
```

```python
import functools
import math

import jax
import jax.numpy as jnp
from jax import lax
from jax.experimental import pallas as pl
from jax.experimental.pallas import tpu as pltpu

F32 = jnp.float32
BF16 = jnp.bfloat16

HEAD_DIM = 64
N_Q_HEADS = 8
N_KV_HEADS = 2
D_ATTN = N_Q_HEADS * HEAD_DIM
D_KV = N_KV_HEADS * HEAD_DIM
SSM_GROUP = 16
SSM_STATE = 64
GRID_W = 64
ROPE_THETA = 10000.0
N_MOD = 6
EPS = 1e-6

LANES = 128
SUBLANES = 8
SLAB_GROUPS = LANES // SSM_GROUP
SLAB_STATES = SLAB_GROUPS * SSM_STATE
VMEM_LIMIT = 56 * 1024 * 1024
TOKEN_TILE = 512


def _const_spec(shape):
    nd = len(shape)
    return pl.BlockSpec(shape, lambda *_: (0,) * nd, pipeline_mode=pl.Buffered(1))


def _layer_spec(shape, layer):
    nd = len(shape)
    return pl.BlockSpec((None,) + shape, lambda *_: (layer,) + (0,) * nd,
                        pipeline_mode=pl.Buffered(1))


def _mod_spec(d_model, layer, first_row, tiles_per_batch):
    if tiles_per_batch is None:
        return pl.BlockSpec((None, 1, N_MOD, d_model), lambda i: (layer, first_row, 0, 0))
    return pl.BlockSpec((None, 1, N_MOD, d_model),
                        lambda i: (layer, first_row + i // tiles_per_batch, 0, 0))


def _rms(x, gain):
    ms = jnp.mean(x * x, axis=-1, keepdims=True)
    return x * lax.rsqrt(ms + EPS) * gain


def _mods_kernel(cond_ref, w_ref, b_ref, o_ref):
    c = cond_ref[...]
    act = c * jax.nn.sigmoid(c)
    o_ref[0] = jnp.dot(act.astype(BF16), w_ref[0].astype(BF16), preferred_element_type=F32) + b_ref[0]


def _mods_call(cond, w_mod, b_mod):
    depth, d_model, n_out = w_mod.shape
    rows = cond.shape[0]
    nt = 4
    tn = n_out // nt
    return pl.pallas_call(
        _mods_kernel,
        grid=(depth, nt),
        in_specs=[
            pl.BlockSpec((rows, d_model), lambda l, j: (0, 0)),
            pl.BlockSpec((1, d_model, tn), lambda l, j: (l, 0, j)),
            pl.BlockSpec((1, 1, tn), lambda l, j: (l, 0, j)),
        ],
        out_specs=pl.BlockSpec((1, rows, tn), lambda l, j: (l, 0, j)),
        out_shape=jax.ShapeDtypeStruct((depth, rows, n_out), F32),
        compiler_params=pltpu.CompilerParams(
            dimension_semantics=("arbitrary", "arbitrary"), vmem_limit_bytes=VMEM_LIMIT),
        name="adaln_mods",
    )(cond, w_mod, b_mod.reshape(depth, 1, n_out))


def _disc_kernel(are_ref, aim_ref, ldt_ref, bre_ref, bim_ref, lr_ref, li_ref, bbr_ref, bbi_ref):
    dt = jnp.exp(ldt_ref[...])
    ar = are_ref[...]
    ai = aim_ref[...]
    mag = jnp.exp(ar * dt)
    lr = mag * jnp.cos(ai * dt)
    li = mag * jnp.sin(ai * dt)
    den = ar * ar + ai * ai
    zr = ((lr - 1.0) * ar + li * ai) / den
    zi = (li * ar - (lr - 1.0) * ai) / den
    br = bre_ref[...]
    bi = bim_ref[...]
    lr_ref[...] = lr
    li_ref[...] = li
    bbr_ref[...] = zr * br - zi * bi
    bbi_ref[...] = zr * bi + zi * br


def _disc_call(a_re, a_im, log_dt, b_re, b_im):
    depth, ndir, g, p, h = b_re.shape
    rows, cols = depth * ndir * g, p * h
    expand = lambda a: jnp.broadcast_to(a[..., None], (depth, ndir, g, p, h)).reshape(rows, cols)
    ldt = jnp.broadcast_to(log_dt[..., None, None], (depth, ndir, g, p, h)).reshape(rows, cols)
    out = jax.ShapeDtypeStruct((rows, cols), F32)
    lr, li, bbr, bbi = pl.pallas_call(
        _disc_kernel,
        out_shape=(out, out, out, out),
        compiler_params=pltpu.CompilerParams(vmem_limit_bytes=VMEM_LIMIT),
        name="ssm_discretise",
    )(expand(a_re), expand(a_im), ldt, b_re.reshape(rows, cols), b_im.reshape(rows, cols))
    shape5 = (depth, ndir, g, p, h)
    return (lr.reshape(shape5)[..., 0], li.reshape(shape5)[..., 0],
            bbr.reshape(shape5), bbi.reshape(shape5))


def _group_sumsq(z, ones_ref, width):
    z2 = z * z
    hi = z2.astype(BF16)
    lo = (z2 - hi.astype(F32)).astype(BF16)
    ones = ones_ref[0:width, 0:width]
    return (jnp.dot(hi, ones, preferred_element_type=F32)
            + jnp.dot(lo, ones, preferred_element_type=F32))


def _head_rms(z, gain, ones_ref):
    ss = _group_sumsq(z, ones_ref, z.shape[-1])
    return z * lax.rsqrt(ss * (1.0 / HEAD_DIM) + EPS) * gain


def _rope(z, cos, sin_signed):
    width = z.shape[-1]
    lane = lax.broadcasted_iota(jnp.int32, z.shape, 1)
    nxt = pltpu.roll(z, width - 1, axis=1)
    prv = pltpu.roll(z, 1, axis=1)
    partner = jnp.where((lane & 1) == 0, nxt, prv)
    return z * cos + partner * sin_signed


def _inproj_kernel(*refs, rope):
    if rope:
        (x_ref, mod_ref, g1_ref, w_ref, qg_ref, kg_ref, ones_ref, cos_ref, sin_ref,
         q_ref, k_ref, v_ref, u_ref) = refs
    else:
        (x_ref, mod_ref, g1_ref, w_ref, qg_ref, kg_ref, ones_ref,
         q_ref, k_ref, v_ref, u_ref) = refs
    mod = mod_ref[0]
    h = _rms(x_ref[...], g1_ref[...]) * (1.0 + mod[1:2]) + mod[0:1]
    proj = jnp.dot(h.astype(BF16), w_ref[...], preferred_element_type=F32)
    q = _head_rms(proj[:, :D_ATTN], qg_ref[...], ones_ref)
    k = _head_rms(proj[:, D_ATTN:D_ATTN + D_KV], kg_ref[...], ones_ref)
    if rope:
        q = _rope(q, cos_ref[...], sin_ref[...])
        k = _rope(k, cos_ref[:, 0:D_KV], sin_ref[:, 0:D_KV])
    q_ref[...] = (q * (HEAD_DIM ** -0.5)).astype(BF16)
    k_ref[...] = k
    v_ref[...] = proj[:, D_ATTN + D_KV:D_ATTN + 2 * D_KV]
    u_ref[...] = proj[:, D_ATTN + 2 * D_KV:]


def _inproj_call(x2d, mods, g1, w_in_b, qg, kg, ones_bd, rope_tabs, layer, mod_row, tokens_per_batch,
                 tile):
    n_tok, d_model = x2d.shape
    d_in = w_in_b.shape[-1]
    d_ssm = d_in - D_ATTN - 2 * D_KV
    tiles_per_batch = None if tokens_per_batch is None else tokens_per_batch // tile
    in_specs = [
        pl.BlockSpec((tile, d_model), lambda i: (i, 0)),
        _mod_spec(d_model, layer, mod_row, tiles_per_batch),
        _layer_spec((1, d_model), layer),
        _layer_spec((d_model, d_in), layer),
        _layer_spec((1, D_ATTN), layer),
        _layer_spec((1, D_KV), layer),
        _const_spec((D_ATTN, D_ATTN)),
    ]
    args = [x2d, mods, g1, w_in_b, qg, kg, ones_bd]
    if rope_tabs is not None:
        in_specs += [pl.BlockSpec((tile, D_ATTN), lambda i: (i % tiles_per_batch, 0))] * 2
        args += list(rope_tabs)
    row = lambda width: pl.BlockSpec((tile, width), lambda i: (i, 0))
    return pl.pallas_call(
        functools.partial(_inproj_kernel, rope=rope_tabs is not None),
        grid=(n_tok // tile,),
        in_specs=in_specs,
        out_specs=(row(D_ATTN), row(D_KV), row(D_KV), row(d_ssm)),
        out_shape=(jax.ShapeDtypeStruct((n_tok, D_ATTN), BF16),
                   jax.ShapeDtypeStruct((n_tok, D_KV), F32),
                   jax.ShapeDtypeStruct((n_tok, D_KV), F32),
                   jax.ShapeDtypeStruct((n_tok, d_ssm), F32)),
        compiler_params=pltpu.CompilerParams(
            dimension_semantics=("arbitrary",), vmem_limit_bytes=VMEM_LIMIT),
        name="inproj_rope" if rope_tabs is not None else "inproj",
    )(*args)


def _attn_kernel(*refs, has_cache):
    if has_cache:
        q_ref, kn_ref, vn_ref, ck_ref, cv_ref, o_ref, kvar, vvar = refs
    else:
        q_ref, kn_ref, vn_ref, o_ref, kvar, vvar = refs

    @pl.when(pl.program_id(1) == 0)
    def _():
        for new_ref, cache_ref, dst in ((kn_ref, ck_ref if has_cache else None, kvar),
                                        (vn_ref, cv_ref if has_cache else None, vvar)):
            src = new_ref[0]
            if has_cache:
                src = jnp.concatenate([cache_ref[0], src], axis=0)
            low = lax.broadcasted_iota(jnp.int32, src.shape, 1) < HEAD_DIM
            head0 = jnp.where(low, src, 0.0)
            head1 = jnp.where(low, 0.0, src)
            dst[0] = head0.astype(BF16)
            dst[1] = pltpu.roll(head0, HEAD_DIM, axis=1).astype(BF16)
            dst[2] = pltpu.roll(head1, HEAD_DIM, axis=1).astype(BF16)
            dst[3] = head1.astype(BF16)

    slabs = D_ATTN // LANES
    for slab in range(slabs):
        qs = q_ref[0, :, slab * LANES:(slab + 1) * LANES]
        kv = slab // (slabs // N_KV_HEADS)
        acc = None
        for half in range(2):
            idx = kv * 2 + half
            s = lax.dot_general(qs, kvar[idx], (((1,), (1,)), ((), ())), preferred_element_type=F32)
            m = jnp.max(s, axis=-1, keepdims=True)
            p = jnp.exp(s - m)
            denom = jnp.sum(p, axis=-1, keepdims=True)
            o = jnp.dot(p.astype(BF16), vvar[idx], preferred_element_type=F32) / denom
            acc = o if acc is None else acc + o
        o_ref[0, :, slab * LANES:(slab + 1) * LANES] = acc.astype(BF16)


def _attn_call(q3, k3, v3, ck3, cv3, tq):
    b, l, _ = q3.shape
    has_cache = ck3 is not None
    s_len = l + (ck3.shape[1] if has_cache else 0)
    in_specs = [
        pl.BlockSpec((1, tq, D_ATTN), lambda bi, qi: (bi, qi, 0)),
        pl.BlockSpec((1, l, D_KV), lambda bi, qi: (bi, 0, 0)),
        pl.BlockSpec((1, l, D_KV), lambda bi, qi: (bi, 0, 0)),
    ]
    args = [q3, k3, v3]
    if has_cache:
        in_specs += [pl.BlockSpec((1, ck3.shape[1], D_KV), lambda bi, qi: (bi, 0, 0))] * 2
        args += [ck3, cv3]
    return pl.pallas_call(
        functools.partial(_attn_kernel, has_cache=has_cache),
        grid=(b, l // tq),
        in_specs=in_specs,
        out_specs=pl.BlockSpec((1, tq, D_ATTN), lambda bi, qi: (bi, qi, 0)),
        out_shape=jax.ShapeDtypeStruct((b, l, D_ATTN), BF16),
        scratch_shapes=[pltpu.VMEM((4, s_len, D_KV), BF16), pltpu.VMEM((4, s_len, D_KV), BF16)],
        compiler_params=pltpu.CompilerParams(
            dimension_semantics=("arbitrary", "arbitrary"), vmem_limit_bytes=VMEM_LIMIT),
        name="attn_cache" if has_cache else "attn",
    )(*args)


def _ssm_kernel(uf_ref, ub_ref, h0_ref, lam_ref, wx_ref, wyr_ref, wyi_ref,
                yf_ref, yb_ref, ht_ref, xs, upad, ypad, carry, *, nb, nq, t_len, pitch):
    c = pl.program_id(1)
    n_chunks = pl.num_programs(1)
    half = SLAB_STATES // LANES
    rows_q = nb * pitch

    @pl.when(c == 0)
    def _():
        for d in range(2):
            for j in range(2 * half):
                carry[d, j] = h0_ref[d, j // half, :, (j % half) * LANES:(j % half + 1) * LANES]

    for d, u_ref in ((0, uf_ref), (1, ub_ref)):
        for q in range(nq):
            for b in range(nb):
                upad[q, pl.ds(b * pitch, t_len), :] = u_ref[b, :, q * LANES:(q + 1) * LANES]
                upad[q, pl.ds(b * pitch + t_len, pitch - t_len), :] = jnp.zeros(
                    (pitch - t_len, LANES), F32)
        for q in range(nq):
            lhs = upad[q].astype(BF16)
            for jp in range(half):
                xv = jnp.dot(lhs, wx_ref[d, q, :, jp * 2 * LANES:(jp + 1) * 2 * LANES],
                             preferred_element_type=F32)
                xs[d, 2 * jp, pl.ds(q * rows_q, rows_q), :] = xv[:, :LANES]
                xs[d, 2 * jp + 1, pl.ds(q * rows_q, rows_q), :] = xv[:, LANES:]

    lam = [[(lam_ref[d, 0, :, j * LANES:(j + 1) * LANES], lam_ref[d, 1, :, j * LANES:(j + 1) * LANES])
            for j in range(half)] for d in range(2)]

    def step(t, hs):
        out = []
        for d in range(2):
            row = t if d == 0 else t_len - 1 - t
            idx = pl.ds(row, SUBLANES, stride=pitch)
            for j in range(half):
                lr, li = lam[d][j]
                hr, hi = hs[d * 2 * half + j], hs[d * 2 * half + half + j]
                nr = lr * hr - li * hi + xs[d, j, idx, :]
                ni = lr * hi + li * hr + xs[d, half + j, idx, :]
                xs[d, j, idx, :] = nr
                xs[d, half + j, idx, :] = ni
                out.append((d * 2 * half + j, nr))
                out.append((d * 2 * half + half + j, ni))
        return tuple(v for _, v in sorted(out, key=lambda kv: kv[0]))

    init = tuple(carry[d, j] for d in range(2) for j in range(2 * half))
    final = lax.fori_loop(0, t_len, step, init, unroll=2)
    for d in range(2):
        for j in range(2 * half):
            carry[d, j] = final[d * 2 * half + j]

    @pl.when(c == n_chunks - 1)
    def _():
        for d in range(2):
            for j in range(2 * half):
                ht_ref[d, j // half, :, (j % half) * LANES:(j % half + 1) * LANES] = (
                    final[d * 2 * half + j])

    for d, y_ref in ((0, yf_ref), (1, yb_ref)):
        for q in range(nq):
            rows = pl.ds(q * rows_q, rows_q)
            h_re = jnp.concatenate([xs[d, j, rows, :] for j in range(half)], axis=1).astype(BF16)
            h_im = jnp.concatenate([xs[d, half + j, rows, :] for j in range(half)], axis=1).astype(BF16)
            ypad[...] = (jnp.dot(h_re, wyr_ref[d, q], preferred_element_type=F32)
                         - jnp.dot(h_im, wyi_ref[d, q], preferred_element_type=F32))
            for b in range(nb):
                y_ref[b, :, q * LANES:(q + 1) * LANES] = ypad[pl.ds(b * pitch, t_len), :]


def _ssm_call(u3, h0, lam, wx, wyr, wyi, nb, t_len):
    b, l, chans = u3.shape
    nq = SUBLANES // nb
    n_bg = b // nb
    n_sg = chans // (nq * LANES)
    n_rows = n_bg * n_sg
    n_chunks = l // t_len
    pitch = t_len + 4
    u_map_f = lambda r, c: (r // n_sg, c, r % n_sg)
    u_map_b = lambda r, c: (r // n_sg, n_chunks - 1 - c, r % n_sg)
    u_block = (nb, t_len, nq * LANES)
    st_block = (2, 2, SUBLANES, SLAB_STATES)
    tab = lambda shape: pl.BlockSpec((None,) + shape, lambda r, c: (r % n_sg,) + (0,) * len(shape))
    kern = functools.partial(_ssm_kernel, nb=nb, nq=nq, t_len=t_len, pitch=pitch)
    return pl.pallas_call(
        kern,
        grid=(n_rows, n_chunks),
        in_specs=[
            pl.BlockSpec(u_block, u_map_f),
            pl.BlockSpec(u_block, u_map_b),
            pl.BlockSpec(st_block, lambda r, c: (0, 0, r, 0)),
            tab((2, 2, SUBLANES, SLAB_STATES)),
            tab((2, nq, LANES, 2 * SLAB_STATES)),
            tab((2, nq, SLAB_STATES, LANES)),
            tab((2, nq, SLAB_STATES, LANES)),
        ],
        out_specs=(
            pl.BlockSpec(u_block, u_map_f),
            pl.BlockSpec(u_block, u_map_b),
            pl.BlockSpec(st_block, lambda r, c: (0, 0, r, 0)),
        ),
        out_shape=(jax.ShapeDtypeStruct(u3.shape, F32),
                   jax.ShapeDtypeStruct(u3.shape, F32),
                   jax.ShapeDtypeStruct((2, 2, n_rows * SUBLANES, SLAB_STATES), F32)),
        scratch_shapes=[
            pltpu.VMEM((2, 2 * SLAB_STATES // LANES, SUBLANES * pitch, LANES), F32),
            pltpu.VMEM((nq, nb * pitch, LANES), F32),
            pltpu.VMEM((nb * pitch, LANES), F32),
            pltpu.VMEM((2, 2 * SLAB_STATES // LANES, SUBLANES, LANES), F32),
        ],
        compiler_params=pltpu.CompilerParams(
            dimension_semantics=("arbitrary", "arbitrary"), vmem_limit_bytes=VMEM_LIMIT),
        name=f"ssm_scan_nb{nb}",
    )(u3, u3, h0, lam, wx, wyr, wyi)


def _ssm_tables(lr, li, bbr, bbi, c_re, c_im, nb):
    ndir, g, p, h = bbr.shape
    n_slab = g // SLAB_GROUPS
    nq = SUBLANES // nb
    n_sg = n_slab // nq
    eye = jnp.eye(SLAB_GROUPS, dtype=F32)

    def lam_rows(a):
        a = a.reshape(ndir, n_sg, nq, 1, SLAB_STATES)
        a = jnp.broadcast_to(a, (ndir, n_sg, nq, nb, SLAB_STATES))
        return a.reshape(ndir, n_sg, SUBLANES, SLAB_STATES).transpose(1, 0, 2, 3)

    lam = jnp.stack([lam_rows(lr), lam_rows(li)], axis=2)

    def in_blockdiag(bb):
        bb = bb.reshape(ndir, n_slab, SLAB_GROUPS, p, h)
        m = jnp.einsum("dsgph,gk->dsghkp", bb, eye)
        return m.reshape(ndir, n_slab, SLAB_GROUPS * h, SLAB_GROUPS * p)

    wx = jnp.concatenate([in_blockdiag(bbr), in_blockdiag(bbi)], axis=-1)
    wx = wx.reshape(ndir, n_sg, nq, LANES, 2 * SLAB_STATES).transpose(1, 0, 2, 3, 4).astype(BF16)

    def out_blockdiag(cc):
        cc = cc.astype(F32).reshape(ndir, n_slab, SLAB_GROUPS, h, p)
        m = jnp.einsum("dsghp,gk->dsgpkh", cc, eye)
        m = m.reshape(ndir, n_sg, nq, SLAB_GROUPS * p, SLAB_GROUPS * h)
        return m.transpose(1, 0, 2, 3, 4).astype(BF16)

    return lam, wx, out_blockdiag(c_re), out_blockdiag(c_im)


def _mix_ffn_kernel(attn_ref, yf_ref, yb_ref, u_ref, x_ref, mod_ref, dskip_ref, wglu_ref, bglu_ref,
                    wout_ref, g2_ref, wffi_ref, wffo_ref, fn_ref, o_ref, *, final, ff_chunks):
    d_ff = wffo_ref.shape[0]
    y = (yf_ref[...] + yb_ref[...]) + dskip_ref[...] * u_ref[...]
    g = 0.5 * y * (1.0 + lax.erf(y * (2.0 ** -0.5)))
    z = jnp.dot(g.astype(BF16), wglu_ref[...], preferred_element_type=F32) + bglu_ref[...]
    ssm_out = g * jax.nn.sigmoid(z)
    mixed = (jnp.dot(attn_ref[...], wout_ref[0:D_ATTN, :], preferred_element_type=F32)
             + jnp.dot(ssm_out.astype(BF16), wout_ref[D_ATTN:, :], preferred_element_type=F32))
    mod = mod_ref[0]
    x1 = x_ref[...] + mod[2:3] * mixed
    h2 = (_rms(x1, g2_ref[...]) * (1.0 + mod[4:5]) + mod[3:4]).astype(BF16)
    fc = d_ff // ff_chunks
    acc = None
    for ci in range(ff_chunks):
        gate = jnp.dot(h2, wffi_ref[:, ci * fc:(ci + 1) * fc], preferred_element_type=F32)
        up = jnp.dot(h2, wffi_ref[:, d_ff + ci * fc:d_ff + (ci + 1) * fc], preferred_element_type=F32)
        act = (gate * jax.nn.sigmoid(gate) * up).astype(BF16)
        part = jnp.dot(act, wffo_ref[ci * fc:(ci + 1) * fc, :], preferred_element_type=F32)
        acc = part if acc is None else acc + part
    x2 = x1 + mod[5:6] * acc
    if final:
        x2 = _rms(x2, fn_ref[...])
    o_ref[...] = x2


def _mix_ffn_call(attn, yf, yb, u, x2d, mods, dskip, wglu_b, bglu, wout_b, g2, wffi_b, wffo_b, fnorm,
                  layer, mod_row, tokens_per_batch, tile, final):
    n_tok, d_model = x2d.shape
    d_ssm = u.shape[1]
    d_ff = wffo_b.shape[-2]
    tiles_per_batch = None if tokens_per_batch is None else tokens_per_batch // tile
    row = lambda width: pl.BlockSpec((tile, width), lambda i: (i, 0))
    return pl.pallas_call(
        functools.partial(_mix_ffn_kernel, final=final, ff_chunks=2),
        grid=(n_tok // tile,),
        in_specs=[
            row(D_ATTN), row(d_ssm), row(d_ssm), row(d_ssm), row(d_model),
            _mod_spec(d_model, layer, mod_row, tiles_per_batch),
            _layer_spec((1, d_ssm), layer),
            _layer_spec((d_ssm, d_ssm), layer),
            _layer_spec((1, d_ssm), layer),
            _layer_spec((D_ATTN + d_ssm, d_model), layer),
            _layer_spec((1, d_model), layer),
            _layer_spec((d_model, 2 * d_ff), layer),
            _layer_spec((d_ff, d_model), layer),
            _const_spec((1, d_model)),
        ],
        out_specs=row(d_model),
        out_shape=jax.ShapeDtypeStruct((n_tok, d_model), F32),
        compiler_params=pltpu.CompilerParams(
            dimension_semantics=("arbitrary",), vmem_limit_bytes=VMEM_LIMIT),
        name="mix_ffn_final" if final else "mix_ffn",
    )(attn, yf, yb, u, x2d, mods, dskip, wglu_b, bglu, wout_b, g2, wffi_b, wffo_b, fnorm)


def _rope_tables(n_tokens):
    axis_dim = HEAD_DIM // 2
    rows = n_tokens // GRID_W
    row = jnp.repeat(jnp.arange(rows, dtype=F32), GRID_W)
    col = jnp.tile(jnp.arange(GRID_W, dtype=F32), rows)
    inv_freq = ROPE_THETA ** (-jnp.arange(0, axis_dim, 2, dtype=F32) / axis_dim)
    ang = jnp.concatenate([row[:, None] * inv_freq, col[:, None] * inv_freq], axis=-1)
    cos = jnp.repeat(jnp.cos(ang), 2, axis=-1)
    sin = jnp.repeat(jnp.sin(ang), 2, axis=-1) * jnp.tile(jnp.array([-1.0, 1.0], F32), HEAD_DIM // 2)
    return jnp.tile(cos, (1, N_Q_HEADS)), jnp.tile(sin, (1, N_Q_HEADS))


def _states_to_rows(st, nb):
    b = st.shape[0]
    nq = SUBLANES // nb
    n_slab = st.shape[3] // SLAB_GROUPS
    s = st.reshape(b // nb, nb, 2, 2, n_slab // nq, nq, SLAB_STATES)
    return s.transpose(2, 3, 0, 4, 5, 1, 6).reshape(2, 2, -1, SLAB_STATES)


def _rows_to_states(rows, batch, nb, n_groups):
    nq = SUBLANES // nb
    n_slab = n_groups // SLAB_GROUPS
    s = rows.reshape(2, 2, batch // nb, n_slab // nq, nq, nb, SLAB_STATES)
    return s.transpose(2, 5, 0, 1, 3, 4, 6).reshape(batch, 2, 2, n_groups, SSM_STATE)


def kernel(x_prompt, x_sample, cache_k, cache_v, state_ssm, c, c_ctx, w_mod, b_mod, norm1, norm2, w_in, q_norm, k_norm, ssm_a_re, ssm_a_im, ssm_log_dt, ssm_b_re, ssm_b_im, ssm_c_re, ssm_c_im, ssm_d, w_glu, b_glu, w_out, w_ffn_in, w_ffn_out, final_norm):
    batch, seq, d_model = x_prompt.shape
    dec_batch, dec_seq, _ = x_sample.shape
    depth = w_in.shape[0]
    past = cache_k.shape[2]
    n_groups = ssm_a_re.shape[2]
    d_ssm = n_groups * SSM_GROUP

    cond = jnp.zeros((SUBLANES, d_model), F32).at[0].set(c_ctx).at[1:1 + dec_batch].set(c)
    mods = _mods_call(cond, w_mod, b_mod).reshape(depth, SUBLANES, N_MOD, d_model)
    lr, li, bbr, bbi = _disc_call(ssm_a_re, ssm_a_im, ssm_log_dt, ssm_b_re, ssm_b_im)
    rope_tabs = _rope_tables(dec_seq)
    head_ids = jnp.arange(D_ATTN) // HEAD_DIM
    ones_bd = (head_ids[:, None] == head_ids[None, :]).astype(BF16)

    ctx_nb, lat_nb = SUBLANES, dec_batch
    xp = x_prompt.reshape(batch * seq, d_model)
    xs = x_sample.reshape(dec_batch * dec_seq, d_model)
    zero_state = jnp.zeros((2, 2, batch * n_groups // SLAB_GROUPS, SLAB_STATES), F32)
    w_in_b = w_in.astype(BF16)
    wglu_b = w_glu.astype(BF16)
    wout_b = w_out.astype(BF16)
    wffi_b = w_ffn_in.astype(BF16)
    wffo_b = w_ffn_out.astype(BF16)
    g1 = norm1.reshape(depth, 1, d_model)
    g2 = norm2.reshape(depth, 1, d_model)
    qg = jnp.tile(q_norm, (1, N_Q_HEADS)).reshape(depth, 1, D_ATTN)
    kg = jnp.tile(k_norm, (1, N_KV_HEADS)).reshape(depth, 1, D_KV)
    dskip = ssm_d.reshape(depth, 1, d_ssm)
    bglu = b_glu.reshape(depth, 1, d_ssm)
    fnorm = final_norm.reshape(1, d_model)
    new_k, new_v, new_s = [], [], []
    for l in range(depth):
        final = l == depth - 1
        for is_ctx in (True, False):
            if is_ctx:
                x2d, n_b, n_l, nb, tq, t_len = xp, batch, seq, ctx_nb, seq, seq
                mod_row, tokens_per_batch, tabs, ck, cv, h0 = 0, None, None, None, None, zero_state
            else:
                x2d, n_b, n_l, nb, tq, t_len = xs, dec_batch, dec_seq, lat_nb, 512, 256
                mod_row, tokens_per_batch, tabs = 1, dec_seq, rope_tabs
                ck = cache_k[:, l].reshape(dec_batch, past, D_KV)
                cv = cache_v[:, l].reshape(dec_batch, past, D_KV)
                h0 = _states_to_rows(state_ssm[:, l], nb)
            q, k, v, u = _inproj_call(x2d, mods, g1, w_in_b, qg, kg, ones_bd, tabs, l, mod_row,
                                      tokens_per_batch, TOKEN_TILE)
            attn = _attn_call(q.reshape(n_b, n_l, D_ATTN), k.reshape(n_b, n_l, D_KV),
                              v.reshape(n_b, n_l, D_KV), ck, cv, tq)
            lam, wx, wyr, wyi = _ssm_tables(lr[l], li[l], bbr[l], bbi[l], ssm_c_re[l], ssm_c_im[l], nb)
            yf, yb, ht = _ssm_call(u.reshape(n_b, n_l, d_ssm), h0, lam, wx, wyr, wyi, nb, t_len)
            x_new = _mix_ffn_call(attn.reshape(-1, D_ATTN), yf.reshape(-1, d_ssm), yb.reshape(-1, d_ssm),
                                  u, x2d, mods, dskip, wglu_b, bglu, wout_b, g2, wffi_b, wffo_b,
                                  fnorm, l, mod_row, tokens_per_batch, TOKEN_TILE, final)
            if is_ctx:
                xp = x_new
                new_k.append(k.reshape(batch, seq, N_KV_HEADS, HEAD_DIM))
                new_v.append(v.reshape(batch, seq, N_KV_HEADS, HEAD_DIM))
                new_s.append(_rows_to_states(ht, batch, nb, n_groups))
            else:
                xs = x_new
    return (xp.reshape(batch, seq, d_model), xs.reshape(dec_batch, dec_seq, d_model),
            jnp.stack(new_k, axis=1), jnp.stack(new_v, axis=1), jnp.stack(new_s, axis=1))
```

```python
import functools

import jax
import jax.numpy as jnp
from jax import lax
from jax.experimental import pallas as pl
from jax.experimental.pallas import tpu as pltpu

F32 = jnp.float32
BF16 = jnp.bfloat16

HEAD_DIM = 64
N_Q_HEADS = 8
N_KV_HEADS = 2
D_ATTN = N_Q_HEADS * HEAD_DIM
D_KV = N_KV_HEADS * HEAD_DIM
SSM_GROUP = 16
SSM_STATE = 64
GRID_W = 64
ROPE_THETA = 10000.0
N_MOD = 6
EPS = 1e-6

LANES = 128
SUBLANES = 8
MXU_DIM = 256
SSM_BLOCK = MXU_DIM // SSM_GROUP
SLAB_GROUPS = LANES // SSM_GROUP
VMEM_LIMIT = 56 * 1024 * 1024
TOKEN_TILE = 512

_NT = (((1,), (1,)), ((), ()))


def _const_spec(shape):
    nd = len(shape)
    return pl.BlockSpec(shape, lambda *_: (0,) * nd, pipeline_mode=pl.Buffered(1))


def _layer_spec(shape, layer):
    nd = len(shape)
    return pl.BlockSpec((None,) + shape, lambda *_: (layer,) + (0,) * nd,
                        pipeline_mode=pl.Buffered(1))


def _mod_spec(d_model, layer, first_row, tiles_per_batch):
    if tiles_per_batch is None:
        return pl.BlockSpec((None, 1, N_MOD, d_model), lambda i: (layer, first_row, 0, 0))
    return pl.BlockSpec((None, 1, N_MOD, d_model),
                        lambda i: (layer, first_row + i // tiles_per_batch, 0, 0))


def _rms(x, gain):
    ms = jnp.mean(x * x, axis=-1, keepdims=True)
    return x * lax.rsqrt(ms + EPS) * gain


def _mods_kernel(cond_ref, w_ref, b_ref, o_ref):
    c = cond_ref[...]
    act = c * jax.nn.sigmoid(c)
    o_ref[0] = jnp.dot(act.astype(BF16), w_ref[0].astype(BF16), preferred_element_type=F32) + b_ref[0]


def _mods_call(cond, w_mod, b_mod):
    depth, d_model, n_out = w_mod.shape
    rows = cond.shape[0]
    nt = 4
    tn = n_out // nt
    return pl.pallas_call(
        _mods_kernel,
        grid=(depth, nt),
        in_specs=[
            pl.BlockSpec((rows, d_model), lambda l, j: (0, 0)),
            pl.BlockSpec((1, d_model, tn), lambda l, j: (l, 0, j)),
            pl.BlockSpec((1, 1, tn), lambda l, j: (l, 0, j)),
        ],
        out_specs=pl.BlockSpec((1, rows, tn), lambda l, j: (l, 0, j)),
        out_shape=jax.ShapeDtypeStruct((depth, rows, n_out), F32),
        compiler_params=pltpu.CompilerParams(
            dimension_semantics=("arbitrary", "arbitrary"), vmem_limit_bytes=VMEM_LIMIT),
        name="adaln_mods",
    )(cond, w_mod, b_mod.reshape(depth, 1, n_out))


def _group_sumsq(z, ones_ref, width):
    z2 = z * z
    hi = z2.astype(BF16)
    lo = (z2 - hi.astype(F32)).astype(BF16)
    ones = ones_ref[0:width, 0:width]
    return (jnp.dot(hi, ones, preferred_element_type=F32)
            + jnp.dot(lo, ones, preferred_element_type=F32))


def _head_rms(z, gain, ones_ref):
    ss = _group_sumsq(z, ones_ref, z.shape[-1])
    return z * lax.rsqrt(ss * (1.0 / HEAD_DIM) + EPS) * gain


def _rope(z, cos, sin_signed):
    width = z.shape[-1]
    lane = lax.broadcasted_iota(jnp.int32, z.shape, 1)
    nxt = pltpu.roll(z, width - 1, axis=1)
    prv = pltpu.roll(z, 1, axis=1)
    partner = jnp.where((lane & 1) == 0, nxt, prv)
    return z * cos + partner * sin_signed


def _inproj_kernel(*refs, rope):
    if rope:
        (x_ref, mod_ref, g1_ref, w_ref, qg_ref, kg_ref, ones_ref, cos_ref, sin_ref,
         q_ref, k_ref, v_ref, u_ref) = refs
    else:
        (x_ref, mod_ref, g1_ref, w_ref, qg_ref, kg_ref, ones_ref,
         q_ref, k_ref, v_ref, u_ref) = refs
    mod = mod_ref[0]
    h = _rms(x_ref[...], g1_ref[...]) * (1.0 + mod[1:2]) + mod[0:1]
    proj = jnp.dot(h.astype(BF16), w_ref[...], preferred_element_type=F32)
    q = _head_rms(proj[:, :D_ATTN], qg_ref[...], ones_ref)
    k = _head_rms(proj[:, D_ATTN:D_ATTN + D_KV], kg_ref[...], ones_ref)
    if rope:
        q = _rope(q, cos_ref[...], sin_ref[...])
        k = _rope(k, cos_ref[:, 0:D_KV], sin_ref[:, 0:D_KV])
    q_ref[...] = (q * (HEAD_DIM ** -0.5)).astype(BF16)
    k_ref[...] = k
    v_ref[...] = proj[:, D_ATTN + D_KV:D_ATTN + 2 * D_KV]
    u_ref[...] = proj[:, D_ATTN + 2 * D_KV:]


def _inproj_call(x2d, mods, g1, w_in_b, qg, kg, ones_bd, rope_tabs, layer, mod_row, tokens_per_batch,
                 tile):
    n_tok, d_model = x2d.shape
    d_in = w_in_b.shape[-1]
    d_ssm = d_in - D_ATTN - 2 * D_KV
    tiles_per_batch = None if tokens_per_batch is None else tokens_per_batch // tile
    in_specs = [
        pl.BlockSpec((tile, d_model), lambda i: (i, 0)),
        _mod_spec(d_model, layer, mod_row, tiles_per_batch),
        _layer_spec((1, d_model), layer),
        _layer_spec((d_model, d_in), layer),
        _layer_spec((1, D_ATTN), layer),
        _layer_spec((1, D_KV), layer),
        _const_spec((D_ATTN, D_ATTN)),
    ]
    args = [x2d, mods, g1, w_in_b, qg, kg, ones_bd]
    if rope_tabs is not None:
        in_specs += [pl.BlockSpec((tile, D_ATTN), lambda i: (i % tiles_per_batch, 0))] * 2
        args += list(rope_tabs)
    row = lambda width: pl.BlockSpec((tile, width), lambda i: (i, 0))
    return pl.pallas_call(
        functools.partial(_inproj_kernel, rope=rope_tabs is not None),
        grid=(n_tok // tile,),
        in_specs=in_specs,
        out_specs=(row(D_ATTN), row(D_KV), row(D_KV), row(d_ssm)),
        out_shape=(jax.ShapeDtypeStruct((n_tok, D_ATTN), BF16),
                   jax.ShapeDtypeStruct((n_tok, D_KV), F32),
                   jax.ShapeDtypeStruct((n_tok, D_KV), F32),
                   jax.ShapeDtypeStruct((n_tok, d_ssm), F32)),
        compiler_params=pltpu.CompilerParams(
            dimension_semantics=("arbitrary",), vmem_limit_bytes=VMEM_LIMIT),
        name="inproj_rope" if rope_tabs is not None else "inproj",
    )(*args)


def _attn_kernel(*refs, has_cache):
    if has_cache:
        q_ref, kn_ref, vn_ref, ck_ref, cv_ref, o_ref, kvar, vvar = refs
    else:
        q_ref, kn_ref, vn_ref, o_ref, kvar, vvar = refs

    @pl.when(pl.program_id(1) == 0)
    def _():
        for new_ref, cache_ref, dst in ((kn_ref, ck_ref if has_cache else None, kvar),
                                        (vn_ref, cv_ref if has_cache else None, vvar)):
            src = new_ref[0]
            if has_cache:
                src = jnp.concatenate([cache_ref[0], src], axis=0)
            low = lax.broadcasted_iota(jnp.int32, src.shape, 1) < HEAD_DIM
            head0 = jnp.where(low, src, 0.0)
            head1 = jnp.where(low, 0.0, src)
            dst[0] = head0.astype(BF16)
            dst[1] = pltpu.roll(head0, HEAD_DIM, axis=1).astype(BF16)
            dst[2] = pltpu.roll(head1, HEAD_DIM, axis=1).astype(BF16)
            dst[3] = head1.astype(BF16)

    slabs = D_ATTN // LANES
    for slab in range(slabs):
        qs = q_ref[0, :, slab * LANES:(slab + 1) * LANES]
        kv = slab // (slabs // N_KV_HEADS)
        acc = None
        for half in range(2):
            idx = kv * 2 + half
            s = lax.dot_general(qs, kvar[idx], _NT, preferred_element_type=F32)
            m = jnp.max(s, axis=-1, keepdims=True)
            p = jnp.exp(s - m)
            denom = jnp.sum(p, axis=-1, keepdims=True)
            o = jnp.dot(p.astype(BF16), vvar[idx], preferred_element_type=F32) / denom
            acc = o if acc is None else acc + o
        o_ref[0, :, slab * LANES:(slab + 1) * LANES] = acc.astype(BF16)


def _attn_call(q3, k3, v3, ck3, cv3, tq):
    b, l, _ = q3.shape
    has_cache = ck3 is not None
    s_len = l + (ck3.shape[1] if has_cache else 0)
    in_specs = [
        pl.BlockSpec((1, tq, D_ATTN), lambda bi, qi: (bi, qi, 0)),
        pl.BlockSpec((1, l, D_KV), lambda bi, qi: (bi, 0, 0)),
        pl.BlockSpec((1, l, D_KV), lambda bi, qi: (bi, 0, 0)),
    ]
    args = [q3, k3, v3]
    if has_cache:
        in_specs += [pl.BlockSpec((1, ck3.shape[1], D_KV), lambda bi, qi: (bi, 0, 0))] * 2
        args += [ck3, cv3]
    return pl.pallas_call(
        functools.partial(_attn_kernel, has_cache=has_cache),
        grid=(b, l // tq),
        in_specs=in_specs,
        out_specs=pl.BlockSpec((1, tq, D_ATTN), lambda bi, qi: (bi, qi, 0)),
        out_shape=jax.ShapeDtypeStruct((b, l, D_ATTN), BF16),
        scratch_shapes=[pltpu.VMEM((4, s_len, D_KV), BF16), pltpu.VMEM((4, s_len, D_KV), BF16)],
        compiler_params=pltpu.CompilerParams(
            dimension_semantics=("arbitrary", "arbitrary"), vmem_limit_bytes=VMEM_LIMIT),
        name="attn_cache" if has_cache else "attn",
    )(*args)


def _ssm_prep_kernel(are_ref, aim_ref, ldt_ref, bre_ref, bim_ref, cre_ref, cim_ref,
                     tab_ref, gx_ref, call_ref, kk_ref):
    is_fwd = (pl.program_id(0) & 1) == 0
    n_groups = gx_ref.shape[0]
    shape = are_ref.shape
    low = lax.broadcasted_iota(jnp.int32, shape, 1) < SSM_STATE
    dt = jnp.exp(ldt_ref[...])
    ar = are_ref[...]
    ai = aim_ref[...]
    mag = jnp.exp(ar * dt)
    lr = mag * jnp.cos(ai * dt)
    li = mag * jnp.sin(ai * dt)
    den = ar * ar + ai * ai
    zr = ((lr - 1.0) * ar + li * ai) / den
    zi = (li * ar - (lr - 1.0) * ai) / den
    br = bre_ref[...]
    bi = bim_ref[...]
    bbr = zr * br - zi * bi
    bbi = zr * bi + zi * br
    cr = cre_ref[...]
    ci = cim_ref[...]
    split = lambda a: a.reshape(n_groups, SSM_GROUP, LANES)

    pr = jnp.ones(shape, F32)
    pi = jnp.zeros(shape, F32)
    for j in range(SSM_BLOCK + 1):
        call_ref[:, j] = split(jnp.where(low, cr * pr - ci * pi, -(cr * pi + ci * pr)))
        if j < SSM_BLOCK:
            val = split(jnp.where(low, pr * bbr - pi * bbi, pr * bbi + pi * bbr))

            @pl.when(is_fwd)
            def _(val=val, j=j):
                gx_ref[:, SSM_BLOCK - 1 - j] = val

            @pl.when(jnp.logical_not(is_fwd))
            def _(val=val, j=j):
                gx_ref[:, j] = val
        else:
            tab_ref[0] = pr
            tab_ref[1] = jnp.where(low, -pi, pi)
        pr, pi = pr * lr - pi * li, pr * li + pi * lr

    bb = jnp.where(low, bbr, bbi)
    for g in range(n_groups):
        lhs = call_ref[g, 0:SSM_BLOCK].reshape(SSM_BLOCK * SSM_GROUP, LANES)
        rhs = bb[g * SSM_GROUP:(g + 1) * SSM_GROUP]
        kk_ref[g] = lax.dot_general(lhs, rhs, _NT, precision=lax.Precision.HIGHEST,
                                    preferred_element_type=F32)


def _ssm_prep_call(a_re, a_im, log_dt, b_re, b_im, c_re, c_im):
    depth, ndir, g, p, h = b_re.shape
    n_ld = depth * ndir
    rows = g * h

    def rows_lanes(a):
        a = a.astype(F32).reshape(n_ld, rows, p)
        return jnp.concatenate([a, a], axis=-1)

    per_state = lambda a: rows_lanes(jnp.broadcast_to(a[:, :, :, None, :], (depth, ndir, g, h, p)))
    ldt = rows_lanes(jnp.broadcast_to(log_dt[:, :, :, None, None], (depth, ndir, g, h, p)))
    to_hp = lambda a: rows_lanes(jnp.swapaxes(a, -1, -2))
    in_spec = pl.BlockSpec((None, rows, LANES), lambda i: (i, 0, 0))
    blk = lambda *shape: pl.BlockSpec((None,) + shape, lambda i: (i,) + (0,) * len(shape))
    return pl.pallas_call(
        _ssm_prep_kernel,
        grid=(n_ld,),
        in_specs=[in_spec] * 7,
        out_specs=(blk(2, rows, LANES), blk(g, SSM_BLOCK, h, LANES), blk(g, SSM_BLOCK + 1, h, LANES),
                   blk(g, SSM_BLOCK * h, h)),
        out_shape=(jax.ShapeDtypeStruct((n_ld, 2, rows, LANES), F32),
                   jax.ShapeDtypeStruct((n_ld, g, SSM_BLOCK, h, LANES), F32),
                   jax.ShapeDtypeStruct((n_ld, g, SSM_BLOCK + 1, h, LANES), F32),
                   jax.ShapeDtypeStruct((n_ld, g, SSM_BLOCK * h, h), F32)),
        compiler_params=pltpu.CompilerParams(
            dimension_semantics=("arbitrary",), vmem_limit_bytes=VMEM_LIMIT),
        name="ssm_prepare",
    )(per_state(a_re), per_state(a_im), ldt, to_hp(b_re), to_hp(b_im), rows_lanes(c_re), rows_lanes(c_im))


def _ssm_operators(tab, gx, call, kk, depth):
    n_ld, g, r, h, _ = gx.shape
    width = r * h
    gx = gx.reshape(depth, 2, g, width, LANES)
    call = call.reshape(depth, 2, g, r + 1, h, LANES)
    kk = kk.reshape(depth, 2, g, r, h, h)
    wx = jnp.concatenate([gx[:, 0], gx[:, 1]], axis=-1).astype(BF16)
    gx_sw = jnp.roll(gx, SSM_STATE, axis=-1)
    wxs = jnp.concatenate([gx_sw[:, 0], gx_sw[:, 1]], axis=-1).astype(BF16)
    vt = jnp.concatenate([call[:, 0, :, 1:].reshape(depth, g, width, LANES),
                          jnp.flip(call[:, 1, :, 1:], axis=2).reshape(depth, g, width, LANES)],
                         axis=-1).astype(BF16)
    kf, kb = kk[:, 0], kk[:, 1]
    seq = jnp.concatenate([jnp.flip(kb[:, :, 1:], axis=2), kf[:, :, :1] + kb[:, :, :1], kf[:, :, 1:],
                           jnp.zeros_like(kf[:, :, :1])], axis=2)
    seq = jnp.moveaxis(seq, 2, 3)
    skew = jnp.tile(seq, (1, 1, 1, r, 1))[:, :, :, :r * (2 * r - 1)]
    skew = skew.reshape(depth, g, h, r, 2 * r - 1, h)[:, :, :, :, r - 1:]
    tt = skew.transpose(0, 1, 4, 2, 3, 5).reshape(depth, g, width, width).astype(BF16)
    tab = tab.reshape(depth, 2, 2, g, h, LANES)[:, :, :, :, 0]
    tab = jnp.broadcast_to(tab.transpose(0, 3, 1, 2, 4)[:, :, :, :, None, :],
                           (depth, g, 2, 2, SUBLANES, LANES))
    return tab, wx, wxs, tt, vt


def _transpose_pieces(vs, piece):
    vs = list(vs)
    n = len(vs)
    s = n // 2
    while s:
        keep_low = (piece & s) == 0
        for i in range(n):
            if i & s:
                continue
            a, b = vs[i], vs[i + s]
            vs[i] = jnp.where(keep_low, a, pltpu.roll(b, s * SSM_GROUP, axis=1))
            vs[i + s] = jnp.where(keep_low, pltpu.roll(a, LANES - s * SSM_GROUP, axis=1), b)
        s //= 2
    return vs


def _ssm_kernel(u_ref, h0_ref, tab_ref, wx_ref, wxs_ref, tt_ref, vt_ref, y_ref, ht_ref, ug, xb, xsb, yg,
                *, batch, seq, pitch):
    m_blk = seq // SSM_BLOCK
    chunk = 2 * SUBLANES
    chunks = m_blk // chunk
    piece = lax.broadcasted_iota(jnp.int32, (chunk, LANES), 1) // SSM_GROUP

    @pl.when(pl.program_id(0) == 0)
    def _():
        ug[...] = jnp.zeros(ug.shape, F32)

    def gather(i, carry):
        b, ch = i // chunks, i % chunks
        tok = b * seq + ch * (chunk * SSM_BLOCK)
        rows = pl.ds(pl.multiple_of(b * pitch + ch * chunk, SUBLANES), chunk)
        for half in range(2):
            us = [u_ref[pl.ds(tok + half * SLAB_GROUPS + rr, chunk, stride=SSM_BLOCK), :]
                  for rr in range(SLAB_GROUPS)]
            for gl, v in enumerate(_transpose_pieces(us, piece)):
                ug[gl, half, rows, :] = v
        return carry

    lax.fori_loop(0, batch * chunks, gather, 0, unroll=4)

    for gl in range(SLAB_GROUPS):
        lhs = jnp.concatenate([ug[gl, 0], ug[gl, 1]], axis=1).astype(BF16)
        x = jnp.dot(lhs, wx_ref[gl], preferred_element_type=F32)
        xb[gl, 0] = x[:, :LANES]
        xb[gl, 1] = x[:, LANES:]
        x = jnp.dot(lhs, wxs_ref[gl], preferred_element_type=F32)
        xsb[gl, 0] = x[:, :LANES]
        xsb[gl, 1] = x[:, LANES:]
        y_in = lax.dot_general(lhs, tt_ref[gl], _NT, preferred_element_type=F32)
        yg[gl, 0] = y_in[:, :LANES]
        yg[gl, 1] = y_in[:, LANES:]

    n_bt = -(-batch // SUBLANES)
    rows_per = min(batch, SUBLANES)
    per_pass = max(1, 4 // n_bt)
    for g0 in range(0, SLAB_GROUPS, per_pass):
        keys = [(gl, d, bt) for gl in range(g0, g0 + per_pass) for d in range(2) for bt in range(n_bt)]
        lane0 = lambda gl, d: (gl * 2 + d) * LANES
        init = []
        for gl, d, bt in keys:
            h = h0_ref[bt * SUBLANES:bt * SUBLANES + rows_per, lane0(gl, d):lane0(gl, d) + LANES]
            init += [h, pltpu.roll(h, SSM_STATE, axis=1)]
        mult = {(gl, d): (tab_ref[gl, d, 0, 0:rows_per, :], tab_ref[gl, d, 1, 0:rows_per, :])
                for gl, d, _ in keys}

        def step(m, hs, keys=keys, mult=mult):
            out = []
            for k, (gl, d, bt) in enumerate(keys):
                h, h_sw = hs[2 * k], hs[2 * k + 1]
                row = m if d == 0 else m_blk - 1 - m
                idx = pl.ds(bt * SUBLANES * pitch + row, rows_per, stride=pitch)
                x = xb[gl, d, idx, :]
                x_sw = xsb[gl, d, idx, :]
                ug[gl, d, idx, :] = h
                a, b = mult[(gl, d)]
                out += [a * h + b * h_sw + x, a * h_sw - b * h + x_sw]
            return tuple(out)

        final = lax.fori_loop(0, m_blk, step, tuple(init), unroll=2)
        for k, (gl, d, bt) in enumerate(keys):
            ht_ref[bt * SUBLANES:bt * SUBLANES + rows_per, lane0(gl, d):lane0(gl, d) + LANES] = final[2 * k]

    for gl in range(SLAB_GROUPS):
        states = jnp.concatenate([ug[gl, 0], ug[gl, 1]], axis=1).astype(BF16)
        y_st = lax.dot_general(states, vt_ref[gl], _NT, preferred_element_type=F32)
        yg[gl, 0] = yg[gl, 0] + y_st[:, :LANES]
        yg[gl, 1] = yg[gl, 1] + y_st[:, LANES:]

    def scatter(i, carry):
        b, ch = i // chunks, i % chunks
        tok = b * seq + ch * (chunk * SSM_BLOCK)
        rows = pl.ds(pl.multiple_of(b * pitch + ch * chunk, SUBLANES), chunk)
        for half in range(2):
            ys = [yg[gl, half, rows, :] for gl in range(SLAB_GROUPS)]
            for rr, v in enumerate(_transpose_pieces(ys, piece)):
                y_ref[pl.ds(tok + half * SLAB_GROUPS + rr, chunk, stride=SSM_BLOCK), :] = v
        return carry

    lax.fori_loop(0, batch * chunks, scatter, 0, unroll=4)


def _ssm_call(u2d, h0, tab, wx, wxs, tt, vt, layer, batch, seq):
    n_tok, chans = u2d.shape
    n_slab = chans // LANES
    m_blk = seq // SSM_BLOCK
    pitch = m_blk + SUBLANES
    rows_p = batch * pitch
    st_lanes = SLAB_GROUPS * 2 * LANES
    op_spec = pl.BlockSpec((None, SLAB_GROUPS, MXU_DIM, MXU_DIM), lambda s: (layer, s, 0, 0))
    scratch = pltpu.VMEM((SLAB_GROUPS, 2, rows_p, LANES), F32)
    return pl.pallas_call(
        functools.partial(_ssm_kernel, batch=batch, seq=seq, pitch=pitch),
        grid=(n_slab,),
        in_specs=[
            pl.BlockSpec((n_tok, LANES), lambda s: (0, s)),
            pl.BlockSpec((batch, st_lanes), lambda s: (0, s)),
            pl.BlockSpec((None, SLAB_GROUPS, 2, 2, SUBLANES, LANES), lambda s: (layer, s, 0, 0, 0, 0)),
            op_spec, op_spec, op_spec, op_spec,
        ],
        out_specs=(pl.BlockSpec((n_tok, LANES), lambda s: (0, s)),
                   pl.BlockSpec((batch, st_lanes), lambda s: (0, s))),
        out_shape=(jax.ShapeDtypeStruct((n_tok, chans), F32),
                   jax.ShapeDtypeStruct((batch, n_slab * st_lanes), F32)),
        scratch_shapes=[scratch, scratch, scratch, scratch],
        compiler_params=pltpu.CompilerParams(
            dimension_semantics=("arbitrary",), vmem_limit_bytes=VMEM_LIMIT),
        name=f"ssm_scan_b{batch}",
    )(u2d, h0, tab, wx, wxs, tt, vt)


def _mix_ffn_kernel(attn_ref, y_ref, u_ref, x_ref, mod_ref, dskip_ref, wglu_ref, bglu_ref,
                    wout_ref, g2_ref, wffi_ref, wffo_ref, fn_ref, o_ref, *, final, ff_chunks):
    d_ff = wffo_ref.shape[0]
    y = y_ref[...] + dskip_ref[...] * u_ref[...]
    g = 0.5 * y * (1.0 + lax.erf(y * (2.0 ** -0.5)))
    z = jnp.dot(g.astype(BF16), wglu_ref[...], preferred_element_type=F32) + bglu_ref[...]
    ssm_out = g * jax.nn.sigmoid(z)
    mixed = (jnp.dot(attn_ref[...], wout_ref[0:D_ATTN, :], preferred_element_type=F32)
             + jnp.dot(ssm_out.astype(BF16), wout_ref[D_ATTN:, :], preferred_element_type=F32))
    mod = mod_ref[0]
    x1 = x_ref[...] + mod[2:3] * mixed
    h2 = (_rms(x1, g2_ref[...]) * (1.0 + mod[4:5]) + mod[3:4]).astype(BF16)
    fc = d_ff // ff_chunks
    acc = None
    for ci in range(ff_chunks):
        gate = jnp.dot(h2, wffi_ref[:, ci * fc:(ci + 1) * fc], preferred_element_type=F32)
        up = jnp.dot(h2, wffi_ref[:, d_ff + ci * fc:d_ff + (ci + 1) * fc], preferred_element_type=F32)
        act = (gate * jax.nn.sigmoid(gate) * up).astype(BF16)
        part = jnp.dot(act, wffo_ref[ci * fc:(ci + 1) * fc, :], preferred_element_type=F32)
        acc = part if acc is None else acc + part
    x2 = x1 + mod[5:6] * acc
    if final:
        x2 = _rms(x2, fn_ref[...])
    o_ref[...] = x2


def _mix_ffn_call(attn, y, u, x2d, mods, dskip, wglu_b, bglu, wout_b, g2, wffi_b, wffo_b, fnorm,
                  layer, mod_row, tokens_per_batch, tile, final):
    n_tok, d_model = x2d.shape
    d_ssm = u.shape[1]
    d_ff = wffo_b.shape[-2]
    tiles_per_batch = None if tokens_per_batch is None else tokens_per_batch // tile
    row = lambda width: pl.BlockSpec((tile, width), lambda i: (i, 0))
    return pl.pallas_call(
        functools.partial(_mix_ffn_kernel, final=final, ff_chunks=2),
        grid=(n_tok // tile,),
        in_specs=[
            row(D_ATTN), row(d_ssm), row(d_ssm), row(d_model),
            _mod_spec(d_model, layer, mod_row, tiles_per_batch),
            _layer_spec((1, d_ssm), layer),
            _layer_spec((d_ssm, d_ssm), layer),
            _layer_spec((1, d_ssm), layer),
            _layer_spec((D_ATTN + d_ssm, d_model), layer),
            _layer_spec((1, d_model), layer),
            _layer_spec((d_model, 2 * d_ff), layer),
            _layer_spec((d_ff, d_model), layer),
            _const_spec((1, d_model)),
        ],
        out_specs=row(d_model),
        out_shape=jax.ShapeDtypeStruct((n_tok, d_model), F32),
        compiler_params=pltpu.CompilerParams(
            dimension_semantics=("arbitrary",), vmem_limit_bytes=VMEM_LIMIT),
        name="mix_ffn_final" if final else "mix_ffn",
    )(attn, y, u, x2d, mods, dskip, wglu_b, bglu, wout_b, g2, wffi_b, wffo_b, fnorm)


def _rope_tables(n_tokens):
    axis_dim = HEAD_DIM // 2
    rows = n_tokens // GRID_W
    row = jnp.repeat(jnp.arange(rows, dtype=F32), GRID_W)
    col = jnp.tile(jnp.arange(GRID_W, dtype=F32), rows)
    inv_freq = ROPE_THETA ** (-jnp.arange(0, axis_dim, 2, dtype=F32) / axis_dim)
    ang = jnp.concatenate([row[:, None] * inv_freq, col[:, None] * inv_freq], axis=-1)
    cos = jnp.repeat(jnp.cos(ang), 2, axis=-1)
    sin = jnp.repeat(jnp.sin(ang), 2, axis=-1) * jnp.tile(jnp.array([-1.0, 1.0], F32), HEAD_DIM // 2)
    return jnp.tile(cos, (1, N_Q_HEADS)), jnp.tile(sin, (1, N_Q_HEADS))


def _states_to_lanes(st):
    return st.transpose(0, 3, 1, 2, 4).reshape(st.shape[0], -1)


def _lanes_to_states(rows, n_groups):
    return rows.reshape(rows.shape[0], n_groups, 2, 2, SSM_STATE).transpose(0, 2, 3, 1, 4)


def kernel(x_prompt, x_sample, cache_k, cache_v, state_ssm, c, c_ctx, w_mod, b_mod, norm1, norm2, w_in, q_norm, k_norm, ssm_a_re, ssm_a_im, ssm_log_dt, ssm_b_re, ssm_b_im, ssm_c_re, ssm_c_im, ssm_d, w_glu, b_glu, w_out, w_ffn_in, w_ffn_out, final_norm):
    batch, seq, d_model = x_prompt.shape
    dec_batch, dec_seq, _ = x_sample.shape
    depth = w_in.shape[0]
    past = cache_k.shape[2]
    n_groups = ssm_a_re.shape[2]
    d_ssm = n_groups * SSM_GROUP

    cond = jnp.zeros((SUBLANES, d_model), F32).at[0].set(c_ctx).at[1:1 + dec_batch].set(c)
    mods = _mods_call(cond, w_mod, b_mod).reshape(depth, SUBLANES, N_MOD, d_model)
    tab, wx, wxs, tt, vt = _ssm_operators(
        *_ssm_prep_call(ssm_a_re, ssm_a_im, ssm_log_dt, ssm_b_re, ssm_b_im, ssm_c_re, ssm_c_im), depth)
    rope_tabs = _rope_tables(dec_seq)
    head_ids = jnp.arange(D_ATTN) // HEAD_DIM
    ones_bd = (head_ids[:, None] == head_ids[None, :]).astype(BF16)

    xp = x_prompt.reshape(batch * seq, d_model)
    xs = x_sample.reshape(dec_batch * dec_seq, d_model)
    zero_state = jnp.zeros((batch, n_groups * 2 * LANES), F32)
    w_in_b = w_in.astype(BF16)
    wglu_b = w_glu.astype(BF16)
    wout_b = w_out.astype(BF16)
    wffi_b = w_ffn_in.astype(BF16)
    wffo_b = w_ffn_out.astype(BF16)
    g1 = norm1.reshape(depth, 1, d_model)
    g2 = norm2.reshape(depth, 1, d_model)
    qg = jnp.tile(q_norm, (1, N_Q_HEADS)).reshape(depth, 1, D_ATTN)
    kg = jnp.tile(k_norm, (1, N_KV_HEADS)).reshape(depth, 1, D_KV)
    dskip = ssm_d.reshape(depth, 1, d_ssm)
    bglu = b_glu.reshape(depth, 1, d_ssm)
    fnorm = final_norm.reshape(1, d_model)
    new_k, new_v, new_s = [], [], []
    for l in range(depth):
        final = l == depth - 1
        for is_ctx in (True, False):
            if is_ctx:
                x2d, n_b, n_l, tq = xp, batch, seq, seq
                mod_row, tokens_per_batch, tabs, ck, cv, h0 = 0, None, None, None, None, zero_state
            else:
                x2d, n_b, n_l, tq = xs, dec_batch, dec_seq, 512
                mod_row, tokens_per_batch, tabs = 1, dec_seq, rope_tabs
                ck = cache_k[:, l].reshape(dec_batch, past, D_KV)
                cv = cache_v[:, l].reshape(dec_batch, past, D_KV)
                h0 = _states_to_lanes(state_ssm[:, l])
            q, k, v, u = _inproj_call(x2d, mods, g1, w_in_b, qg, kg, ones_bd, tabs, l, mod_row,
                                      tokens_per_batch, TOKEN_TILE)
            attn = _attn_call(q.reshape(n_b, n_l, D_ATTN), k.reshape(n_b, n_l, D_KV),
                              v.reshape(n_b, n_l, D_KV), ck, cv, tq)
            y, ht = _ssm_call(u, h0, tab, wx, wxs, tt, vt, l, n_b, n_l)
            x_new = _mix_ffn_call(attn.reshape(-1, D_ATTN), y, u, x2d, mods, dskip, wglu_b, bglu, wout_b,
                                  g2, wffi_b, wffo_b, fnorm, l, mod_row, tokens_per_batch, TOKEN_TILE,
                                  final)
            if is_ctx:
                xp = x_new
                new_k.append(k.reshape(batch, seq, N_KV_HEADS, HEAD_DIM))
                new_v.append(v.reshape(batch, seq, N_KV_HEADS, HEAD_DIM))
                new_s.append(_lanes_to_states(ht, n_groups))
            else:
                xs = x_new
    return (xp.reshape(batch, seq, d_model), xs.reshape(dec_batch, dec_seq, d_model),
            jnp.stack(new_k, axis=1), jnp.stack(new_v, axis=1), jnp.stack(new_s, axis=1))
```

```python
import functools

import jax
import jax.numpy as jnp
from jax import lax
from jax.experimental import pallas as pl
from jax.experimental.pallas import tpu as pltpu

F32 = jnp.float32
BF16 = jnp.bfloat16

HEAD_DIM = 64
N_Q_HEADS = 8
N_KV_HEADS = 2
D_ATTN = N_Q_HEADS * HEAD_DIM
D_KV = N_KV_HEADS * HEAD_DIM
SSM_GROUP = 16
SSM_STATE = 64
GRID_W = 64
ROPE_THETA = 10000.0
N_MOD = 6
EPS = 1e-6

LANES = 128
SUBLANES = 8
MXU_DIM = 256
SSM_BLOCK = MXU_DIM // SSM_GROUP
SLAB_GROUPS = LANES // SSM_GROUP
VMEM_LIMIT = 56 * 1024 * 1024
TOKEN_TILE = 512

_NT = (((1,), (1,)), ((), ()))


def _const_spec(shape):
    nd = len(shape)
    return pl.BlockSpec(shape, lambda *_: (0,) * nd, pipeline_mode=pl.Buffered(1))


def _layer_spec(shape, layer):
    nd = len(shape)
    return pl.BlockSpec((None,) + shape, lambda *_: (layer,) + (0,) * nd,
                        pipeline_mode=pl.Buffered(1))


def _mod_spec(d_model, layer, first_row, tiles_per_batch):
    if tiles_per_batch is None:
        return pl.BlockSpec((None, 1, N_MOD, d_model), lambda i: (layer, first_row, 0, 0))
    return pl.BlockSpec((None, 1, N_MOD, d_model),
                        lambda i: (layer, first_row + i // tiles_per_batch, 0, 0))


def _rms(x, gain):
    ms = jnp.mean(x * x, axis=-1, keepdims=True)
    return x * lax.rsqrt(ms + EPS) * gain


def _mods_kernel(cond_ref, w_ref, b_ref, o_ref):
    c = cond_ref[...]
    act = c * jax.nn.sigmoid(c)
    o_ref[0] = jnp.dot(act.astype(BF16), w_ref[0].astype(BF16), preferred_element_type=F32) + b_ref[0]


def _mods_call(cond, w_mod, b_mod):
    depth, d_model, n_out = w_mod.shape
    rows = cond.shape[0]
    nt = 4
    tn = n_out // nt
    return pl.pallas_call(
        _mods_kernel,
        grid=(depth, nt),
        in_specs=[
            pl.BlockSpec((rows, d_model), lambda l, j: (0, 0)),
            pl.BlockSpec((1, d_model, tn), lambda l, j: (l, 0, j)),
            pl.BlockSpec((1, 1, tn), lambda l, j: (l, 0, j)),
        ],
        out_specs=pl.BlockSpec((1, rows, tn), lambda l, j: (l, 0, j)),
        out_shape=jax.ShapeDtypeStruct((depth, rows, n_out), F32),
        compiler_params=pltpu.CompilerParams(
            dimension_semantics=("arbitrary", "arbitrary"), vmem_limit_bytes=VMEM_LIMIT),
        name="adaln_mods",
    )(cond, w_mod, b_mod.reshape(depth, 1, n_out))


def _group_sumsq(z, ones_ref, width):
    z2 = z * z
    hi = z2.astype(BF16)
    lo = (z2 - hi.astype(F32)).astype(BF16)
    ones = ones_ref[0:width, 0:width]
    return (jnp.dot(hi, ones, preferred_element_type=F32)
            + jnp.dot(lo, ones, preferred_element_type=F32))


def _head_rms(z, gain, ones_ref):
    ss = _group_sumsq(z, ones_ref, z.shape[-1])
    return z * lax.rsqrt(ss * (1.0 / HEAD_DIM) + EPS) * gain


def _rope(z, cos, sin_signed):
    width = z.shape[-1]
    lane = lax.broadcasted_iota(jnp.int32, z.shape, 1)
    nxt = pltpu.roll(z, width - 1, axis=1)
    prv = pltpu.roll(z, 1, axis=1)
    partner = jnp.where((lane & 1) == 0, nxt, prv)
    return z * cos + partner * sin_signed


def _inproj_kernel(*refs, rope):
    if rope:
        (x_ref, mod_ref, g1_ref, w_ref, qg_ref, kg_ref, ones_ref, cos_ref, sin_ref,
         q_ref, k_ref, v_ref, u_ref) = refs
    else:
        (x_ref, mod_ref, g1_ref, w_ref, qg_ref, kg_ref, ones_ref,
         q_ref, k_ref, v_ref, u_ref) = refs
    mod = mod_ref[0]
    h = _rms(x_ref[...], g1_ref[...]) * (1.0 + mod[1:2]) + mod[0:1]
    proj = jnp.dot(h.astype(BF16), w_ref[...], preferred_element_type=F32)
    q = _head_rms(proj[:, :D_ATTN], qg_ref[...], ones_ref)
    k = _head_rms(proj[:, D_ATTN:D_ATTN + D_KV], kg_ref[...], ones_ref)
    if rope:
        q = _rope(q, cos_ref[...], sin_ref[...])
        k = _rope(k, cos_ref[:, 0:D_KV], sin_ref[:, 0:D_KV])
    q_ref[...] = (q * (HEAD_DIM ** -0.5)).astype(BF16)
    k_ref[...] = k
    v_ref[...] = proj[:, D_ATTN + D_KV:D_ATTN + 2 * D_KV]
    u_ref[...] = proj[:, D_ATTN + 2 * D_KV:]


def _inproj_call(x2d, mods, g1, w_in_b, qg, kg, ones_bd, rope_tabs, layer, mod_row, tokens_per_batch,
                 tile):
    n_tok, d_model = x2d.shape
    d_in = w_in_b.shape[-1]
    d_ssm = d_in - D_ATTN - 2 * D_KV
    tiles_per_batch = None if tokens_per_batch is None else tokens_per_batch // tile
    in_specs = [
        pl.BlockSpec((tile, d_model), lambda i: (i, 0)),
        _mod_spec(d_model, layer, mod_row, tiles_per_batch),
        _layer_spec((1, d_model), layer),
        _layer_spec((d_model, d_in), layer),
        _layer_spec((1, D_ATTN), layer),
        _layer_spec((1, D_KV), layer),
        _const_spec((D_ATTN, D_ATTN)),
    ]
    args = [x2d, mods, g1, w_in_b, qg, kg, ones_bd]
    if rope_tabs is not None:
        in_specs += [pl.BlockSpec((tile, D_ATTN), lambda i: (i % tiles_per_batch, 0))] * 2
        args += list(rope_tabs)
    row = lambda width: pl.BlockSpec((tile, width), lambda i: (i, 0))
    return pl.pallas_call(
        functools.partial(_inproj_kernel, rope=rope_tabs is not None),
        grid=(n_tok // tile,),
        in_specs=in_specs,
        out_specs=(row(D_ATTN), row(D_KV), row(D_KV), row(d_ssm)),
        out_shape=(jax.ShapeDtypeStruct((n_tok, D_ATTN), BF16),
                   jax.ShapeDtypeStruct((n_tok, D_KV), F32),
                   jax.ShapeDtypeStruct((n_tok, D_KV), F32),
                   jax.ShapeDtypeStruct((n_tok, d_ssm), F32)),
        compiler_params=pltpu.CompilerParams(
            dimension_semantics=("arbitrary",), vmem_limit_bytes=VMEM_LIMIT),
        name="inproj_rope" if rope_tabs is not None else "inproj",
    )(*args)


def _attn_kernel(*refs, has_cache):
    if has_cache:
        q_ref, kn_ref, vn_ref, ck_ref, cv_ref, o_ref, kvar, vvar = refs
    else:
        q_ref, kn_ref, vn_ref, o_ref, kvar, vvar = refs

    @pl.when(pl.program_id(1) == 0)
    def _():
        for new_ref, cache_ref, dst in ((kn_ref, ck_ref if has_cache else None, kvar),
                                        (vn_ref, cv_ref if has_cache else None, vvar)):
            src = new_ref[0]
            if has_cache:
                src = jnp.concatenate([cache_ref[0], src], axis=0)
            low = lax.broadcasted_iota(jnp.int32, src.shape, 1) < HEAD_DIM
            head0 = jnp.where(low, src, 0.0)
            head1 = jnp.where(low, 0.0, src)
            dst[0] = head0.astype(BF16)
            dst[1] = pltpu.roll(head0, HEAD_DIM, axis=1).astype(BF16)
            dst[2] = pltpu.roll(head1, HEAD_DIM, axis=1).astype(BF16)
            dst[3] = head1.astype(BF16)

    slabs = D_ATTN // LANES
    for slab in range(slabs):
        qs = q_ref[0, :, slab * LANES:(slab + 1) * LANES]
        kv = slab // (slabs // N_KV_HEADS)
        acc = None
        for half in range(2):
            idx = kv * 2 + half
            s = lax.dot_general(qs, kvar[idx], _NT, preferred_element_type=F32)
            m = jnp.max(s, axis=-1, keepdims=True)
            p = jnp.exp(s - m)
            denom = jnp.sum(p, axis=-1, keepdims=True)
            o = jnp.dot(p.astype(BF16), vvar[idx], preferred_element_type=F32) / denom
            acc = o if acc is None else acc + o
        o_ref[0, :, slab * LANES:(slab + 1) * LANES] = acc.astype(BF16)


def _attn_call(q3, k3, v3, ck3, cv3, tq):
    b, l, _ = q3.shape
    has_cache = ck3 is not None
    s_len = l + (ck3.shape[1] if has_cache else 0)
    in_specs = [
        pl.BlockSpec((1, tq, D_ATTN), lambda bi, qi: (bi, qi, 0)),
        pl.BlockSpec((1, l, D_KV), lambda bi, qi: (bi, 0, 0)),
        pl.BlockSpec((1, l, D_KV), lambda bi, qi: (bi, 0, 0)),
    ]
    args = [q3, k3, v3]
    if has_cache:
        in_specs += [pl.BlockSpec((1, ck3.shape[1], D_KV), lambda bi, qi: (bi, 0, 0))] * 2
        args += [ck3, cv3]
    return pl.pallas_call(
        functools.partial(_attn_kernel, has_cache=has_cache),
        grid=(b, l // tq),
        in_specs=in_specs,
        out_specs=pl.BlockSpec((1, tq, D_ATTN), lambda bi, qi: (bi, qi, 0)),
        out_shape=jax.ShapeDtypeStruct((b, l, D_ATTN), BF16),
        scratch_shapes=[pltpu.VMEM((4, s_len, D_KV), BF16), pltpu.VMEM((4, s_len, D_KV), BF16)],
        compiler_params=pltpu.CompilerParams(
            dimension_semantics=("arbitrary", "arbitrary"), vmem_limit_bytes=VMEM_LIMIT),
        name="attn_cache" if has_cache else "attn",
    )(*args)


def _dot3(a, b):
    a_hi = a.astype(BF16)
    b_hi = b.astype(BF16)
    a_lo = (a - a_hi.astype(F32)).astype(BF16)
    b_lo = (b - b_hi.astype(F32)).astype(BF16)
    dot = lambda x, y: lax.dot_general(x, y, _NT, preferred_element_type=F32)
    return dot(a_hi, b_hi) + (dot(a_hi, b_lo) + dot(a_lo, b_hi))


def _ssm_prep_kernel(are_ref, aim_ref, ldt_ref, bre_ref, bim_ref, cre_ref, cim_ref,
                     tab_ref, wx_ref, wxs_ref, vt_ref, tt_ref, clm, lbm, acc):
    d = pl.program_id(2)
    n_groups = wx_ref.shape[0]
    width = SSM_BLOCK * SSM_GROUP
    shape = are_ref.shape
    low = lax.broadcasted_iota(jnp.int32, shape, 1) < SSM_STATE
    dt = jnp.exp(ldt_ref[...])
    ar = are_ref[...]
    ai = aim_ref[...]
    mag = jnp.exp(ar * dt)
    lr = mag * jnp.cos(ai * dt)
    li = mag * jnp.sin(ai * dt)
    den = ar * ar + ai * ai
    zr = ((lr - 1.0) * ar + li * ai) / den
    zi = (li * ar - (lr - 1.0) * ai) / den
    br = bre_ref[...]
    bi = bim_ref[...]
    bbr = zr * br - zi * bi
    bbi = zr * bi + zi * br
    cr = cre_ref[...]
    ci = cim_ref[...]
    split = lambda a: a.reshape(n_groups, SSM_GROUP, LANES)
    inv = 1.0 / (lr * lr + li * li)
    ir = lr * inv
    ii = -li * inv

    def cmul(xr, xi, yr, yi):
        return xr * yr - xi * yi, xr * yi + xi * yr

    pr, pi = jnp.ones(shape, F32), jnp.zeros(shape, F32)
    qr, qi = pr, pi
    lo, hi = slice(0, LANES), slice(LANES, 2 * LANES)
    for j in range(SSM_BLOCK + 1):
        c_re, c_im = cmul(cr, ci, pr, pi)
        c_pos = split(jnp.where(low, c_re, -c_im))
        if j >= 1:
            @pl.when(d == 0)
            def _(v=c_pos.astype(BF16), j=j):
                vt_ref[:, j - 1, :, lo] = v

            @pl.when(d == 1)
            def _(v=c_pos.astype(BF16), j=j):
                vt_ref[:, SSM_BLOCK - j, :, hi] = v
        if j < SSM_BLOCK:
            g_re, g_im = cmul(pr, pi, bbr, bbi)
            g_pos = split(jnp.where(low, g_re, g_im))
            g_pos_sw = split(jnp.where(low, g_im, g_re))
            n_re, n_im = cmul(qr, qi, bbr, bbi)
            g_neg = split(jnp.where(low, n_re, n_im))
            m_re, m_im = cmul(cr, ci, qr, qi)
            c_neg = split(jnp.where(low, m_re, -m_im))

            @pl.when(d == 0)
            def _(j=j, g_pos=g_pos, g_pos_sw=g_pos_sw, c_pos=c_pos, g_neg=g_neg):
                wx_ref[:, SSM_BLOCK - 1 - j, :, lo] = g_pos.astype(BF16)
                wxs_ref[:, SSM_BLOCK - 1 - j, :, lo] = g_pos_sw.astype(BF16)
                clm[:, j] = c_pos
                lbm[:, j] = g_neg

            @pl.when(d == 1)
            def _(j=j, g_pos=g_pos, g_pos_sw=g_pos_sw, c_neg=c_neg):
                wx_ref[:, j, :, hi] = g_pos.astype(BF16)
                wxs_ref[:, j, :, hi] = g_pos_sw.astype(BF16)
                clm[:, j] = c_neg
                lbm[:, j] = g_pos
        else:
            for dd in range(2):
                @pl.when(d == dd)
                def _(dd=dd, pr=pr, pi=pi):
                    tab_ref[:, dd, 0] = split(pr)[:, 0:SUBLANES]
                    tab_ref[:, dd, 1] = split(jnp.where(low, -pi, pi))[:, 0:SUBLANES]
        pr, pi = cmul(pr, pi, lr, li)
        qr, qi = cmul(qr, qi, ir, ii)

    row_blk = lax.broadcasted_iota(jnp.int32, (width, width), 0) // SSM_GROUP
    col_blk = lax.broadcasted_iota(jnp.int32, (width, width), 1) // SSM_GROUP
    for g in range(n_groups):
        t = _dot3(clm[g].reshape(width, LANES), lbm[g].reshape(width, LANES))

        @pl.when(d == 0)
        def _(g=g, t=t):
            acc[g] = jnp.where(col_blk <= row_blk, t, 0.0)

        @pl.when(d == 1)
        def _(g=g, t=t):
            tt_ref[g] = (acc[g] + jnp.where(col_blk >= row_blk, t, 0.0)).astype(BF16)


def _ssm_prep_call(a_re, a_im, log_dt, b_re, b_im, c_re, c_im):
    depth, ndir, g, p, h = b_re.shape
    halves = 2
    g_half = g // halves
    rows = g_half * h
    width = SSM_BLOCK * h

    def rows_lanes(a):
        a = a.astype(F32).reshape(depth, ndir, g * h, p)
        return jnp.concatenate([a, a], axis=-1)

    per_state = lambda a: rows_lanes(jnp.broadcast_to(a[:, :, :, None, :], (depth, ndir, g, h, p)))
    ldt = rows_lanes(jnp.broadcast_to(log_dt[:, :, :, None, None], (depth, ndir, g, h, p)))
    to_hp = lambda a: rows_lanes(jnp.swapaxes(a, -1, -2))
    in_spec = pl.BlockSpec((None, None, rows, LANES), lambda l, s, d: (l, d, s, 0))
    out_blk = lambda *shape: pl.BlockSpec((None, g_half) + shape,
                                          lambda l, s, d: (l, s) + (0,) * len(shape))
    op4 = jax.ShapeDtypeStruct((depth, g, SSM_BLOCK, h, 2 * LANES), BF16)
    tab, wx, wxs, vt, tt = pl.pallas_call(
        _ssm_prep_kernel,
        grid=(depth, halves, ndir),
        in_specs=[in_spec] * 7,
        out_specs=(out_blk(2, 2, SUBLANES, LANES), out_blk(SSM_BLOCK, h, 2 * LANES),
                   out_blk(SSM_BLOCK, h, 2 * LANES), out_blk(SSM_BLOCK, h, 2 * LANES),
                   out_blk(width, width)),
        out_shape=(jax.ShapeDtypeStruct((depth, g, 2, 2, SUBLANES, LANES), F32), op4, op4, op4,
                   jax.ShapeDtypeStruct((depth, g, width, width), BF16)),
        scratch_shapes=[pltpu.VMEM((g_half, SSM_BLOCK, h, LANES), F32),
                        pltpu.VMEM((g_half, SSM_BLOCK, h, LANES), F32),
                        pltpu.VMEM((g_half, width, width), F32)],
        compiler_params=pltpu.CompilerParams(
            dimension_semantics=("arbitrary", "arbitrary", "arbitrary"), vmem_limit_bytes=VMEM_LIMIT),
        name="ssm_prepare",
    )(per_state(a_re), per_state(a_im), ldt, to_hp(b_re), to_hp(b_im), rows_lanes(c_re), rows_lanes(c_im))
    merge = lambda a: a.reshape(depth, g, width, 2 * LANES)
    return tab, merge(wx), merge(wxs), tt, merge(vt)


def _transpose_pieces(vs, piece):
    vs = list(vs)
    n = len(vs)
    s = n // 2
    while s:
        keep_low = (piece & s) == 0
        for i in range(n):
            if i & s:
                continue
            a, b = vs[i], vs[i + s]
            vs[i] = jnp.where(keep_low, a, pltpu.roll(b, s * SSM_GROUP, axis=1))
            vs[i + s] = jnp.where(keep_low, pltpu.roll(a, LANES - s * SSM_GROUP, axis=1), b)
        s //= 2
    return vs


def _ssm_kernel(u_ref, h0_ref, tab_ref, wx_ref, wxs_ref, tt_ref, vt_ref, y_ref, ht_ref, ug, xb, xsb, yg,
                *, batch, seq, pitch):
    m_blk = seq // SSM_BLOCK
    chunk = 2 * SUBLANES
    chunks = m_blk // chunk
    piece = lax.broadcasted_iota(jnp.int32, (chunk, LANES), 1) // SSM_GROUP

    @pl.when(pl.program_id(0) == 0)
    def _():
        ug[...] = jnp.zeros(ug.shape, F32)

    def gather(i, carry):
        b, ch = i // chunks, i % chunks
        tok = b * seq + ch * (chunk * SSM_BLOCK)
        rows = pl.ds(pl.multiple_of(b * pitch + ch * chunk, SUBLANES), chunk)
        for half in range(2):
            us = [u_ref[pl.ds(tok + half * SLAB_GROUPS + rr, chunk, stride=SSM_BLOCK), :]
                  for rr in range(SLAB_GROUPS)]
            for gl, v in enumerate(_transpose_pieces(us, piece)):
                ug[gl, half, rows, :] = v
        return carry

    lax.fori_loop(0, batch * chunks, gather, 0, unroll=4)

    for gl in range(SLAB_GROUPS):
        lhs = jnp.concatenate([ug[gl, 0], ug[gl, 1]], axis=1).astype(BF16)
        x = jnp.dot(lhs, wx_ref[gl], preferred_element_type=F32)
        xb[gl, 0] = x[:, :LANES]
        xb[gl, 1] = x[:, LANES:]
        x = jnp.dot(lhs, wxs_ref[gl], preferred_element_type=F32)
        xsb[gl, 0] = x[:, :LANES]
        xsb[gl, 1] = x[:, LANES:]
        y_in = lax.dot_general(lhs, tt_ref[gl], _NT, preferred_element_type=F32)
        yg[gl, 0] = y_in[:, :LANES]
        yg[gl, 1] = y_in[:, LANES:]

    n_bt = -(-batch // SUBLANES)
    rows_per = min(batch, SUBLANES)
    per_pass = max(1, 4 // n_bt)
    for g0 in range(0, SLAB_GROUPS, per_pass):
        keys = [(gl, d, bt) for gl in range(g0, g0 + per_pass) for d in range(2) for bt in range(n_bt)]
        lane0 = lambda gl, d: (gl * 2 + d) * LANES
        init = []
        for gl, d, bt in keys:
            h = h0_ref[bt * SUBLANES:bt * SUBLANES + rows_per, lane0(gl, d):lane0(gl, d) + LANES]
            init += [h, pltpu.roll(h, SSM_STATE, axis=1)]
        mult = {(gl, d): (tab_ref[gl, d, 0, 0:rows_per, :], tab_ref[gl, d, 1, 0:rows_per, :])
                for gl, d, _ in keys}

        def step(m, hs, keys=keys, mult=mult):
            out = []
            for k, (gl, d, bt) in enumerate(keys):
                h, h_sw = hs[2 * k], hs[2 * k + 1]
                row = m if d == 0 else m_blk - 1 - m
                idx = pl.ds(bt * SUBLANES * pitch + row, rows_per, stride=pitch)
                x = xb[gl, d, idx, :]
                x_sw = xsb[gl, d, idx, :]
                ug[gl, d, idx, :] = h
                a, b = mult[(gl, d)]
                out += [a * h + b * h_sw + x, a * h_sw - b * h + x_sw]
            return tuple(out)

        final = lax.fori_loop(0, m_blk, step, tuple(init), unroll=2)
        for k, (gl, d, bt) in enumerate(keys):
            ht_ref[bt * SUBLANES:bt * SUBLANES + rows_per, lane0(gl, d):lane0(gl, d) + LANES] = final[2 * k]

    for gl in range(SLAB_GROUPS):
        states = jnp.concatenate([ug[gl, 0], ug[gl, 1]], axis=1).astype(BF16)
        y_st = lax.dot_general(states, vt_ref[gl], _NT, preferred_element_type=F32)
        yg[gl, 0] = yg[gl, 0] + y_st[:, :LANES]
        yg[gl, 1] = yg[gl, 1] + y_st[:, LANES:]

    def scatter(i, carry):
        b, ch = i // chunks, i % chunks
        tok = b * seq + ch * (chunk * SSM_BLOCK)
        rows = pl.ds(pl.multiple_of(b * pitch + ch * chunk, SUBLANES), chunk)
        for half in range(2):
            ys = [yg[gl, half, rows, :] for gl in range(SLAB_GROUPS)]
            for rr, v in enumerate(_transpose_pieces(ys, piece)):
                y_ref[pl.ds(tok + half * SLAB_GROUPS + rr, chunk, stride=SSM_BLOCK), :] = v
        return carry

    lax.fori_loop(0, batch * chunks, scatter, 0, unroll=4)


def _ssm_call(u2d, h0, tab, wx, wxs, tt, vt, layer, batch, seq):
    n_tok, chans = u2d.shape
    n_slab = chans // LANES
    m_blk = seq // SSM_BLOCK
    pitch = m_blk + SUBLANES
    rows_p = batch * pitch
    st_lanes = SLAB_GROUPS * 2 * LANES
    op_spec = pl.BlockSpec((None, SLAB_GROUPS, MXU_DIM, MXU_DIM), lambda s: (layer, s, 0, 0))
    scratch = pltpu.VMEM((SLAB_GROUPS, 2, rows_p, LANES), F32)
    return pl.pallas_call(
        functools.partial(_ssm_kernel, batch=batch, seq=seq, pitch=pitch),
        grid=(n_slab,),
        in_specs=[
            pl.BlockSpec((n_tok, LANES), lambda s: (0, s)),
            pl.BlockSpec((batch, st_lanes), lambda s: (0, s)),
            pl.BlockSpec((None, SLAB_GROUPS, 2, 2, SUBLANES, LANES), lambda s: (layer, s, 0, 0, 0, 0)),
            op_spec, op_spec, op_spec, op_spec,
        ],
        out_specs=(pl.BlockSpec((n_tok, LANES), lambda s: (0, s)),
                   pl.BlockSpec((batch, st_lanes), lambda s: (0, s))),
        out_shape=(jax.ShapeDtypeStruct((n_tok, chans), F32),
                   jax.ShapeDtypeStruct((batch, n_slab * st_lanes), F32)),
        scratch_shapes=[scratch, scratch, scratch, scratch],
        compiler_params=pltpu.CompilerParams(
            dimension_semantics=("arbitrary",), vmem_limit_bytes=VMEM_LIMIT),
        name=f"ssm_scan_b{batch}",
    )(u2d, h0, tab, wx, wxs, tt, vt)


def _mix_ffn_kernel(attn_ref, y_ref, u_ref, x_ref, mod_ref, dskip_ref, wglu_ref, bglu_ref,
                    wout_ref, g2_ref, wffi_ref, wffo_ref, fn_ref, o_ref, *, final, ff_chunks):
    d_ff = wffo_ref.shape[0]
    y = y_ref[...] + dskip_ref[...] * u_ref[...]
    g = 0.5 * y * (1.0 + lax.erf(y * (2.0 ** -0.5)))
    z = jnp.dot(g.astype(BF16), wglu_ref[...], preferred_element_type=F32) + bglu_ref[...]
    ssm_out = g * jax.nn.sigmoid(z)
    mixed = (jnp.dot(attn_ref[...], wout_ref[0:D_ATTN, :], preferred_element_type=F32)
             + jnp.dot(ssm_out.astype(BF16), wout_ref[D_ATTN:, :], preferred_element_type=F32))
    mod = mod_ref[0]
    x1 = x_ref[...] + mod[2:3] * mixed
    h2 = (_rms(x1, g2_ref[...]) * (1.0 + mod[4:5]) + mod[3:4]).astype(BF16)
    fc = d_ff // ff_chunks
    acc = None
    for ci in range(ff_chunks):
        gate = jnp.dot(h2, wffi_ref[:, ci * fc:(ci + 1) * fc], preferred_element_type=F32)
        up = jnp.dot(h2, wffi_ref[:, d_ff + ci * fc:d_ff + (ci + 1) * fc], preferred_element_type=F32)
        act = (gate * jax.nn.sigmoid(gate) * up).astype(BF16)
        part = jnp.dot(act, wffo_ref[ci * fc:(ci + 1) * fc, :], preferred_element_type=F32)
        acc = part if acc is None else acc + part
    x2 = x1 + mod[5:6] * acc
    if final:
        x2 = _rms(x2, fn_ref[...])
    o_ref[...] = x2


def _mix_ffn_call(attn, y, u, x2d, mods, dskip, wglu_b, bglu, wout_b, g2, wffi_b, wffo_b, fnorm,
                  layer, mod_row, tokens_per_batch, tile, final):
    n_tok, d_model = x2d.shape
    d_ssm = u.shape[1]
    d_ff = wffo_b.shape[-2]
    tiles_per_batch = None if tokens_per_batch is None else tokens_per_batch // tile
    row = lambda width: pl.BlockSpec((tile, width), lambda i: (i, 0))
    return pl.pallas_call(
        functools.partial(_mix_ffn_kernel, final=final, ff_chunks=2),
        grid=(n_tok // tile,),
        in_specs=[
            row(D_ATTN), row(d_ssm), row(d_ssm), row(d_model),
            _mod_spec(d_model, layer, mod_row, tiles_per_batch),
            _layer_spec((1, d_ssm), layer),
            _layer_spec((d_ssm, d_ssm), layer),
            _layer_spec((1, d_ssm), layer),
            _layer_spec((D_ATTN + d_ssm, d_model), layer),
            _layer_spec((1, d_model), layer),
            _layer_spec((d_model, 2 * d_ff), layer),
            _layer_spec((d_ff, d_model), layer),
            _const_spec((1, d_model)),
        ],
        out_specs=row(d_model),
        out_shape=jax.ShapeDtypeStruct((n_tok, d_model), F32),
        compiler_params=pltpu.CompilerParams(
            dimension_semantics=("arbitrary",), vmem_limit_bytes=VMEM_LIMIT),
        name="mix_ffn_final" if final else "mix_ffn",
    )(attn, y, u, x2d, mods, dskip, wglu_b, bglu, wout_b, g2, wffi_b, wffo_b, fnorm)


def _rope_tables(n_tokens):
    axis_dim = HEAD_DIM // 2
    rows = n_tokens // GRID_W
    row = jnp.repeat(jnp.arange(rows, dtype=F32), GRID_W)
    col = jnp.tile(jnp.arange(GRID_W, dtype=F32), rows)
    inv_freq = ROPE_THETA ** (-jnp.arange(0, axis_dim, 2, dtype=F32) / axis_dim)
    ang = jnp.concatenate([row[:, None] * inv_freq, col[:, None] * inv_freq], axis=-1)
    cos = jnp.repeat(jnp.cos(ang), 2, axis=-1)
    sin = jnp.repeat(jnp.sin(ang), 2, axis=-1) * jnp.tile(jnp.array([-1.0, 1.0], F32), HEAD_DIM // 2)
    return jnp.tile(cos, (1, N_Q_HEADS)), jnp.tile(sin, (1, N_Q_HEADS))


def _states_to_lanes(st):
    return st.transpose(0, 3, 1, 2, 4).reshape(st.shape[0], -1)


def _lanes_to_states(rows, n_groups):
    return rows.reshape(rows.shape[0], n_groups, 2, 2, SSM_STATE).transpose(0, 2, 3, 1, 4)


def kernel(x_prompt, x_sample, cache_k, cache_v, state_ssm, c, c_ctx, w_mod, b_mod, norm1, norm2, w_in, q_norm, k_norm, ssm_a_re, ssm_a_im, ssm_log_dt, ssm_b_re, ssm_b_im, ssm_c_re, ssm_c_im, ssm_d, w_glu, b_glu, w_out, w_ffn_in, w_ffn_out, final_norm):
    batch, seq, d_model = x_prompt.shape
    dec_batch, dec_seq, _ = x_sample.shape
    depth = w_in.shape[0]
    past = cache_k.shape[2]
    n_groups = ssm_a_re.shape[2]
    d_ssm = n_groups * SSM_GROUP

    cond = jnp.zeros((SUBLANES, d_model), F32).at[0].set(c_ctx).at[1:1 + dec_batch].set(c)
    mods = _mods_call(cond, w_mod, b_mod).reshape(depth, SUBLANES, N_MOD, d_model)
    tab, wx, wxs, tt, vt = _ssm_prep_call(ssm_a_re, ssm_a_im, ssm_log_dt, ssm_b_re, ssm_b_im,
                                          ssm_c_re, ssm_c_im)
    rope_tabs = _rope_tables(dec_seq)
    head_ids = jnp.arange(D_ATTN) // HEAD_DIM
    ones_bd = (head_ids[:, None] == head_ids[None, :]).astype(BF16)

    xp = x_prompt.reshape(batch * seq, d_model)
    xs = x_sample.reshape(dec_batch * dec_seq, d_model)
    zero_state = jnp.zeros((batch, n_groups * 2 * LANES), F32)
    w_in_b = w_in.astype(BF16)
    wglu_b = w_glu.astype(BF16)
    wout_b = w_out.astype(BF16)
    wffi_b = w_ffn_in.astype(BF16)
    wffo_b = w_ffn_out.astype(BF16)
    g1 = norm1.reshape(depth, 1, d_model)
    g2 = norm2.reshape(depth, 1, d_model)
    qg = jnp.tile(q_norm, (1, N_Q_HEADS)).reshape(depth, 1, D_ATTN)
    kg = jnp.tile(k_norm, (1, N_KV_HEADS)).reshape(depth, 1, D_KV)
    dskip = ssm_d.reshape(depth, 1, d_ssm)
    bglu = b_glu.reshape(depth, 1, d_ssm)
    fnorm = final_norm.reshape(1, d_model)
    new_k, new_v, new_s = [], [], []
    for l in range(depth):
        final = l == depth - 1
        for is_ctx in (True, False):
            if is_ctx:
                x2d, n_b, n_l, tq = xp, batch, seq, seq
                mod_row, tokens_per_batch, tabs, ck, cv, h0 = 0, None, None, None, None, zero_state
            else:
                x2d, n_b, n_l, tq = xs, dec_batch, dec_seq, 512
                mod_row, tokens_per_batch, tabs = 1, dec_seq, rope_tabs
                ck = cache_k[:, l].reshape(dec_batch, past, D_KV)
                cv = cache_v[:, l].reshape(dec_batch, past, D_KV)
                h0 = _states_to_lanes(state_ssm[:, l])
            q, k, v, u = _inproj_call(x2d, mods, g1, w_in_b, qg, kg, ones_bd, tabs, l, mod_row,
                                      tokens_per_batch, TOKEN_TILE)
            attn = _attn_call(q.reshape(n_b, n_l, D_ATTN), k.reshape(n_b, n_l, D_KV),
                              v.reshape(n_b, n_l, D_KV), ck, cv, tq)
            y, ht = _ssm_call(u, h0, tab, wx, wxs, tt, vt, l, n_b, n_l)
            x_new = _mix_ffn_call(attn.reshape(-1, D_ATTN), y, u, x2d, mods, dskip, wglu_b, bglu, wout_b,
                                  g2, wffi_b, wffo_b, fnorm, l, mod_row, tokens_per_batch, TOKEN_TILE,
                                  final)
            if is_ctx:
                xp = x_new
                new_k.append(k.reshape(batch, seq, N_KV_HEADS, HEAD_DIM))
                new_v.append(v.reshape(batch, seq, N_KV_HEADS, HEAD_DIM))
                new_s.append(_lanes_to_states(ht, n_groups))
            else:
                xs = x_new
    return (xp.reshape(batch, seq, d_model), xs.reshape(dec_batch, dec_seq, d_model),
            jnp.stack(new_k, axis=1), jnp.stack(new_v, axis=1), jnp.stack(new_s, axis=1))
```

```python
import functools

import jax
import jax.numpy as jnp
from jax import lax
from jax.experimental import pallas as pl
from jax.experimental.pallas import tpu as pltpu

F32 = jnp.float32
BF16 = jnp.bfloat16

HEAD_DIM = 64
N_Q_HEADS = 8
N_KV_HEADS = 2
D_ATTN = N_Q_HEADS * HEAD_DIM
D_KV = N_KV_HEADS * HEAD_DIM
SSM_GROUP = 16
SSM_STATE = 64
GRID_W = 64
ROPE_THETA = 10000.0
N_MOD = 6
EPS = 1e-6

LANES = 128
SUBLANES = 8
MXU_DIM = 256
SSM_BLOCK = MXU_DIM // SSM_GROUP
SLAB_GROUPS = LANES // SSM_GROUP
VMEM_LIMIT = 56 * 1024 * 1024
TOKEN_TILE = 512

_NT = (((1,), (1,)), ((), ()))


def _const_spec(shape):
    nd = len(shape)
    return pl.BlockSpec(shape, lambda *_: (0,) * nd, pipeline_mode=pl.Buffered(1))


def _layer_spec(shape, layer):
    nd = len(shape)
    return pl.BlockSpec((None,) + shape, lambda *_: (layer,) + (0,) * nd,
                        pipeline_mode=pl.Buffered(1))


def _mod_spec(d_model, layer, first_row, tiles_per_batch):
    if tiles_per_batch is None:
        return pl.BlockSpec((None, 1, N_MOD, d_model), lambda i: (layer, first_row, 0, 0))
    return pl.BlockSpec((None, 1, N_MOD, d_model),
                        lambda i: (layer, first_row + i // tiles_per_batch, 0, 0))


def _rms(x, gain):
    ms = jnp.mean(x * x, axis=-1, keepdims=True)
    return x * lax.rsqrt(ms + EPS) * gain


def _mods_kernel(cond_ref, w_ref, b_ref, o_ref):
    c = cond_ref[...]
    act = c * jax.nn.sigmoid(c)
    o_ref[0] = jnp.dot(act.astype(BF16), w_ref[0].astype(BF16), preferred_element_type=F32) + b_ref[0]


def _mods_call(cond, w_mod, b_mod):
    depth, d_model, n_out = w_mod.shape
    rows = cond.shape[0]
    nt = 4
    tn = n_out // nt
    return pl.pallas_call(
        _mods_kernel,
        grid=(depth, nt),
        in_specs=[
            pl.BlockSpec((rows, d_model), lambda l, j: (0, 0)),
            pl.BlockSpec((1, d_model, tn), lambda l, j: (l, 0, j)),
            pl.BlockSpec((1, 1, tn), lambda l, j: (l, 0, j)),
        ],
        out_specs=pl.BlockSpec((1, rows, tn), lambda l, j: (l, 0, j)),
        out_shape=jax.ShapeDtypeStruct((depth, rows, n_out), F32),
        compiler_params=pltpu.CompilerParams(
            dimension_semantics=("arbitrary", "arbitrary"), vmem_limit_bytes=VMEM_LIMIT),
        name="adaln_mods",
    )(cond, w_mod, b_mod.reshape(depth, 1, n_out))


def _group_sumsq(z, ones_ref, width):
    z2 = z * z
    hi = z2.astype(BF16)
    lo = (z2 - hi.astype(F32)).astype(BF16)
    ones = ones_ref[0:width, 0:width]
    return (jnp.dot(hi, ones, preferred_element_type=F32)
            + jnp.dot(lo, ones, preferred_element_type=F32))


def _head_rms(z, gain, ones_ref):
    ss = _group_sumsq(z, ones_ref, z.shape[-1])
    return z * lax.rsqrt(ss * (1.0 / HEAD_DIM) + EPS) * gain


def _rope(z, cos, sin_signed):
    width = z.shape[-1]
    lane = lax.broadcasted_iota(jnp.int32, z.shape, 1)
    nxt = pltpu.roll(z, width - 1, axis=1)
    prv = pltpu.roll(z, 1, axis=1)
    partner = jnp.where((lane & 1) == 0, nxt, prv)
    return z * cos + partner * sin_signed


def _inproj_kernel(*refs, rope):
    if rope:
        (x_ref, mod_ref, g1_ref, w_ref, qg_ref, kg_ref, ones_ref, cos_ref, sin_ref,
         q_ref, k_ref, v_ref, u_ref) = refs
    else:
        (x_ref, mod_ref, g1_ref, w_ref, qg_ref, kg_ref, ones_ref,
         q_ref, k_ref, v_ref, u_ref) = refs
    mod = mod_ref[0]
    h = _rms(x_ref[...], g1_ref[...]) * (1.0 + mod[1:2]) + mod[0:1]
    proj = jnp.dot(h.astype(BF16), w_ref[...], preferred_element_type=F32)
    q = _head_rms(proj[:, :D_ATTN], qg_ref[...], ones_ref)
    k = _head_rms(proj[:, D_ATTN:D_ATTN + D_KV], kg_ref[...], ones_ref)
    if rope:
        q = _rope(q, cos_ref[...], sin_ref[...])
        k = _rope(k, cos_ref[:, 0:D_KV], sin_ref[:, 0:D_KV])
    q_ref[...] = (q * (HEAD_DIM ** -0.5)).astype(BF16)
    k_ref[...] = k
    v_ref[...] = proj[:, D_ATTN + D_KV:D_ATTN + 2 * D_KV]
    u_ref[...] = proj[:, D_ATTN + 2 * D_KV:]


def _inproj_call(x2d, mods, g1, w_in_b, qg, kg, ones_bd, rope_tabs, layer, mod_row, tokens_per_batch,
                 tile):
    n_tok, d_model = x2d.shape
    d_in = w_in_b.shape[-1]
    d_ssm = d_in - D_ATTN - 2 * D_KV
    tiles_per_batch = None if tokens_per_batch is None else tokens_per_batch // tile
    in_specs = [
        pl.BlockSpec((tile, d_model), lambda i: (i, 0)),
        _mod_spec(d_model, layer, mod_row, tiles_per_batch),
        _layer_spec((1, d_model), layer),
        _layer_spec((d_model, d_in), layer),
        _layer_spec((1, D_ATTN), layer),
        _layer_spec((1, D_KV), layer),
        _const_spec((D_ATTN, D_ATTN)),
    ]
    args = [x2d, mods, g1, w_in_b, qg, kg, ones_bd]
    if rope_tabs is not None:
        in_specs += [pl.BlockSpec((tile, D_ATTN), lambda i: (i % tiles_per_batch, 0))] * 2
        args += list(rope_tabs)
    row = lambda width: pl.BlockSpec((tile, width), lambda i: (i, 0))
    return pl.pallas_call(
        functools.partial(_inproj_kernel, rope=rope_tabs is not None),
        grid=(n_tok // tile,),
        in_specs=in_specs,
        out_specs=(row(D_ATTN), row(D_KV), row(D_KV), row(d_ssm)),
        out_shape=(jax.ShapeDtypeStruct((n_tok, D_ATTN), BF16),
                   jax.ShapeDtypeStruct((n_tok, D_KV), F32),
                   jax.ShapeDtypeStruct((n_tok, D_KV), F32),
                   jax.ShapeDtypeStruct((n_tok, d_ssm), F32)),
        compiler_params=pltpu.CompilerParams(
            dimension_semantics=("arbitrary",), vmem_limit_bytes=VMEM_LIMIT),
        name="inproj_rope" if rope_tabs is not None else "inproj",
    )(*args)


def _attn_kernel(*refs, has_cache):
    if has_cache:
        q_ref, kn_ref, vn_ref, ck_ref, cv_ref, o_ref, kvar, vvar = refs
    else:
        q_ref, kn_ref, vn_ref, o_ref, kvar, vvar = refs

    @pl.when(pl.program_id(1) == 0)
    def _():
        for new_ref, cache_ref, dst in ((kn_ref, ck_ref if has_cache else None, kvar),
                                        (vn_ref, cv_ref if has_cache else None, vvar)):
            src = new_ref[0]
            if has_cache:
                src = jnp.concatenate([cache_ref[0], src], axis=0)
            low = lax.broadcasted_iota(jnp.int32, src.shape, 1) < HEAD_DIM
            head0 = jnp.where(low, src, 0.0)
            head1 = jnp.where(low, 0.0, src)
            dst[0] = head0.astype(BF16)
            dst[1] = pltpu.roll(head0, HEAD_DIM, axis=1).astype(BF16)
            dst[2] = pltpu.roll(head1, HEAD_DIM, axis=1).astype(BF16)
            dst[3] = head1.astype(BF16)

    slabs = D_ATTN // LANES
    for slab in range(slabs):
        qs = q_ref[0, :, slab * LANES:(slab + 1) * LANES]
        kv = slab // (slabs // N_KV_HEADS)
        acc = None
        for half in range(2):
            idx = kv * 2 + half
            s = lax.dot_general(qs, kvar[idx], _NT, preferred_element_type=F32)
            m = jnp.max(s, axis=-1, keepdims=True)
            p = jnp.exp(s - m)
            denom = jnp.sum(p, axis=-1, keepdims=True)
            o = jnp.dot(p.astype(BF16), vvar[idx], preferred_element_type=F32) / denom
            acc = o if acc is None else acc + o
        o_ref[0, :, slab * LANES:(slab + 1) * LANES] = acc.astype(BF16)


def _attn_call(q3, k3, v3, ck3, cv3, tq):
    b, l, _ = q3.shape
    has_cache = ck3 is not None
    s_len = l + (ck3.shape[1] if has_cache else 0)
    in_specs = [
        pl.BlockSpec((1, tq, D_ATTN), lambda bi, qi: (bi, qi, 0)),
        pl.BlockSpec((1, l, D_KV), lambda bi, qi: (bi, 0, 0)),
        pl.BlockSpec((1, l, D_KV), lambda bi, qi: (bi, 0, 0)),
    ]
    args = [q3, k3, v3]
    if has_cache:
        in_specs += [pl.BlockSpec((1, ck3.shape[1], D_KV), lambda bi, qi: (bi, 0, 0))] * 2
        args += [ck3, cv3]
    return pl.pallas_call(
        functools.partial(_attn_kernel, has_cache=has_cache),
        grid=(b, l // tq),
        in_specs=in_specs,
        out_specs=pl.BlockSpec((1, tq, D_ATTN), lambda bi, qi: (bi, qi, 0)),
        out_shape=jax.ShapeDtypeStruct((b, l, D_ATTN), BF16),
        scratch_shapes=[pltpu.VMEM((4, s_len, D_KV), BF16), pltpu.VMEM((4, s_len, D_KV), BF16)],
        compiler_params=pltpu.CompilerParams(
            dimension_semantics=("arbitrary", "arbitrary"), vmem_limit_bytes=VMEM_LIMIT),
        name="attn_cache" if has_cache else "attn",
    )(*args)


def _dot3(a, b):
    a_hi = a.astype(BF16)
    b_hi = b.astype(BF16)
    a_lo = (a - a_hi.astype(F32)).astype(BF16)
    b_lo = (b - b_hi.astype(F32)).astype(BF16)
    dot = lambda x, y: lax.dot_general(x, y, _NT, preferred_element_type=F32)
    return dot(a_hi, b_hi) + (dot(a_hi, b_lo) + dot(a_lo, b_hi))


def _ssm_prep_kernel(are_ref, aim_ref, ldt_ref, bre_ref, bim_ref, cre_ref, cim_ref,
                     tab_ref, wx_ref, wxs_ref, vt_ref, tt_ref, clm, lbm):
    n_groups = wx_ref.shape[0]
    width = SSM_BLOCK * SSM_GROUP
    shape = (SSM_GROUP, LANES)
    low = lax.broadcasted_iota(jnp.int32, shape, 1) < SSM_STATE
    row_blk = lax.broadcasted_iota(jnp.int32, (width, width), 0) // SSM_GROUP
    col_blk = lax.broadcasted_iota(jnp.int32, (width, width), 1) // SSM_GROUP

    def cmul(xr, xi, yr, yi):
        return xr * yr - xi * yi, xr * yi + xi * yr

    def group(g, carry):
        rows = pl.ds(pl.multiple_of(g * SSM_GROUP, SSM_GROUP), SSM_GROUP)
        tile = None
        for d in range(2):
            lanes = slice(d * LANES, (d + 1) * LANES)
            dt = jnp.exp(ldt_ref[d, rows, :])
            ar = are_ref[d, rows, :]
            ai = aim_ref[d, rows, :]
            mag = jnp.exp(ar * dt)
            lr = mag * jnp.cos(ai * dt)
            li = mag * jnp.sin(ai * dt)
            den = ar * ar + ai * ai
            zr = ((lr - 1.0) * ar + li * ai) / den
            zi = (li * ar - (lr - 1.0) * ai) / den
            bbr, bbi = cmul(zr, zi, bre_ref[d, rows, :], bim_ref[d, rows, :])
            cr = cre_ref[d, rows, :]
            ci = cim_ref[d, rows, :]
            inv = 1.0 / (lr * lr + li * li)
            ir = lr * inv
            ii = -li * inv
            pr, pi = jnp.ones(shape, F32), jnp.zeros(shape, F32)
            qr, qi = pr, pi
            for j in range(SSM_BLOCK + 1):
                c_re, c_im = cmul(cr, ci, pr, pi)
                c_pos = jnp.where(low, c_re, -c_im)
                if j >= 1:
                    vt_ref[g, (j - 1) if d == 0 else (SSM_BLOCK - j), :, lanes] = c_pos.astype(BF16)
                if j < SSM_BLOCK:
                    g_re, g_im = cmul(pr, pi, bbr, bbi)
                    g_pos = jnp.where(low, g_re, g_im)
                    r = (SSM_BLOCK - 1 - j) if d == 0 else j
                    wx_ref[g, r, :, lanes] = g_pos.astype(BF16)
                    wxs_ref[g, r, :, lanes] = jnp.where(low, g_im, g_re).astype(BF16)
                    blk = pl.ds(j * SSM_GROUP, SSM_GROUP)
                    if d == 0:
                        n_re, n_im = cmul(qr, qi, bbr, bbi)
                        clm[blk, :] = c_pos
                        lbm[blk, :] = jnp.where(low, n_re, n_im)
                    else:
                        m_re, m_im = cmul(cr, ci, qr, qi)
                        clm[blk, :] = jnp.where(low, m_re, -m_im)
                        lbm[blk, :] = g_pos
                else:
                    tab_ref[g, d, 0] = pr[0:SUBLANES]
                    tab_ref[g, d, 1] = jnp.where(low, -pi, pi)[0:SUBLANES]
                pr, pi = cmul(pr, pi, lr, li)
                qr, qi = cmul(qr, qi, ir, ii)
            t = _dot3(clm[...], lbm[...])
            t = jnp.where((col_blk <= row_blk) if d == 0 else (col_blk >= row_blk), t, 0.0)
            tile = t if tile is None else tile + t
        tt_ref[g] = tile.astype(BF16)
        return carry

    lax.fori_loop(0, n_groups, group, 0)


def _ssm_prep_call(a_re, a_im, log_dt, b_re, b_im, c_re, c_im):
    depth, ndir, g, p, h = b_re.shape
    halves = 2
    g_half = g // halves
    rows = g_half * h
    width = SSM_BLOCK * h

    def rows_lanes(a):
        a = a.astype(F32).reshape(depth, ndir, g * h, p)
        return jnp.concatenate([a, a], axis=-1)

    per_state = lambda a: rows_lanes(jnp.broadcast_to(a[:, :, :, None, :], (depth, ndir, g, h, p)))
    ldt = rows_lanes(jnp.broadcast_to(log_dt[:, :, :, None, None], (depth, ndir, g, h, p)))
    to_hp = lambda a: rows_lanes(jnp.swapaxes(a, -1, -2))
    in_spec = pl.BlockSpec((None, ndir, rows, LANES), lambda l, s: (l, 0, s, 0))
    out_blk = lambda *shape: pl.BlockSpec((None, g_half) + shape, lambda l, s: (l, s) + (0,) * len(shape))
    op4 = jax.ShapeDtypeStruct((depth, g, SSM_BLOCK, h, 2 * LANES), BF16)
    tab, wx, wxs, vt, tt = pl.pallas_call(
        _ssm_prep_kernel,
        grid=(depth, halves),
        in_specs=[in_spec] * 7,
        out_specs=(out_blk(2, 2, SUBLANES, LANES), out_blk(SSM_BLOCK, h, 2 * LANES),
                   out_blk(SSM_BLOCK, h, 2 * LANES), out_blk(SSM_BLOCK, h, 2 * LANES),
                   out_blk(width, width)),
        out_shape=(jax.ShapeDtypeStruct((depth, g, 2, 2, SUBLANES, LANES), F32), op4, op4, op4,
                   jax.ShapeDtypeStruct((depth, g, width, width), BF16)),
        scratch_shapes=[pltpu.VMEM((width, LANES), F32), pltpu.VMEM((width, LANES), F32)],
        compiler_params=pltpu.CompilerParams(
            dimension_semantics=("arbitrary", "arbitrary"), vmem_limit_bytes=VMEM_LIMIT),
        name="ssm_prepare",
    )(per_state(a_re), per_state(a_im), ldt, to_hp(b_re), to_hp(b_im), rows_lanes(c_re), rows_lanes(c_im))
    merge = lambda a: a.reshape(depth, g, width, 2 * LANES)
    return tab, merge(wx), merge(wxs), tt, merge(vt)


def _transpose_pieces(sets, piece):
    sets = [list(vs) for vs in sets]
    n = len(sets[0])
    s = n // 2
    while s:
        keep_low = (piece & s) == 0
        pairs = [(vs, i) for vs in sets for i in range(n) if not i & s]
        moved = [(pltpu.roll(vs[i + s], s * SSM_GROUP, axis=1),
                  pltpu.roll(vs[i], LANES - s * SSM_GROUP, axis=1)) for vs, i in pairs]
        for (vs, i), (from_hi, from_lo) in zip(pairs, moved):
            vs[i], vs[i + s] = jnp.where(keep_low, vs[i], from_hi), jnp.where(keep_low, from_lo, vs[i + s])
        s //= 2
    return sets


def _ssm_kernel(u_ref, h0_ref, tab_ref, wx_ref, wxs_ref, tt_ref, vt_ref, y_ref, ht_ref, ug, xb, xsb, yg,
                *, batch, seq, pitch):
    m_blk = seq // SSM_BLOCK
    chunk = 2 * SUBLANES
    chunks = m_blk // chunk
    per_step = 4
    piece = lax.broadcasted_iota(jnp.int32, (chunk, LANES), 1) // SSM_GROUP

    @pl.when(pl.program_id(0) == 0)
    def _():
        ug[...] = jnp.zeros(ug.shape, F32)

    def chunk_rows(idx):
        b, ch = idx // chunks, idx % chunks
        tok = b * seq + ch * (chunk * SSM_BLOCK)
        return tok, pl.ds(pl.multiple_of(b * pitch + ch * chunk, SUBLANES), chunk)

    def gather(i, carry):
        sets, dests = [], []
        for k in range(per_step):
            tok, rows = chunk_rows(i * per_step + k)
            for half in range(2):
                sets.append([u_ref[pl.ds(tok + half * SLAB_GROUPS + rr, chunk, stride=SSM_BLOCK), :]
                             for rr in range(SLAB_GROUPS)])
                dests.append((half, rows))
        for (half, rows), vs in zip(dests, _transpose_pieces(sets, piece)):
            for gl, v in enumerate(vs):
                ug[gl, half, rows, :] = v
        return carry

    lax.fori_loop(0, batch * chunks // per_step, gather, 0)

    for gl in range(SLAB_GROUPS):
        lhs = jnp.concatenate([ug[gl, 0], ug[gl, 1]], axis=1).astype(BF16)
        x = jnp.dot(lhs, wx_ref[gl], preferred_element_type=F32)
        xb[gl, 0] = x[:, :LANES]
        xb[gl, 1] = x[:, LANES:]
        x = jnp.dot(lhs, wxs_ref[gl], preferred_element_type=F32)
        xsb[gl, 0] = x[:, :LANES]
        xsb[gl, 1] = x[:, LANES:]
        y_in = lax.dot_general(lhs, tt_ref[gl], _NT, preferred_element_type=F32)
        yg[gl, 0] = y_in[:, :LANES]
        yg[gl, 1] = y_in[:, LANES:]

    n_bt = -(-batch // SUBLANES)
    rows_per = min(batch, SUBLANES)
    per_pass = max(1, 4 // n_bt)
    for g0 in range(0, SLAB_GROUPS, per_pass):
        keys = [(gl, d, bt) for gl in range(g0, g0 + per_pass) for d in range(2) for bt in range(n_bt)]
        lane0 = lambda gl, d: (gl * 2 + d) * LANES
        init = []
        for gl, d, bt in keys:
            h = h0_ref[bt * SUBLANES:bt * SUBLANES + rows_per, lane0(gl, d):lane0(gl, d) + LANES]
            init += [h, pltpu.roll(h, SSM_STATE, axis=1)]
        mult = {(gl, d): (tab_ref[gl, d, 0, 0:rows_per, :], tab_ref[gl, d, 1, 0:rows_per, :])
                for gl, d, _ in keys}

        def step(m, hs, keys=keys, mult=mult):
            out = []
            for k, (gl, d, bt) in enumerate(keys):
                h, h_sw = hs[2 * k], hs[2 * k + 1]
                row = m if d == 0 else m_blk - 1 - m
                idx = pl.ds(bt * SUBLANES * pitch + row, rows_per, stride=pitch)
                x = xb[gl, d, idx, :]
                x_sw = xsb[gl, d, idx, :]
                ug[gl, d, idx, :] = h
                a, b = mult[(gl, d)]
                out += [a * h + b * h_sw + x, a * h_sw - b * h + x_sw]
            return tuple(out)

        final = lax.fori_loop(0, m_blk, step, tuple(init), unroll=2)
        for k, (gl, d, bt) in enumerate(keys):
            ht_ref[bt * SUBLANES:bt * SUBLANES + rows_per, lane0(gl, d):lane0(gl, d) + LANES] = final[2 * k]

    for gl in range(SLAB_GROUPS):
        states = jnp.concatenate([ug[gl, 0], ug[gl, 1]], axis=1).astype(BF16)
        y_st = lax.dot_general(states, vt_ref[gl], _NT, preferred_element_type=F32)
        yg[gl, 0] = yg[gl, 0] + y_st[:, :LANES]
        yg[gl, 1] = yg[gl, 1] + y_st[:, LANES:]

    def scatter(i, carry):
        sets, dests = [], []
        for k in range(per_step):
            tok, rows = chunk_rows(i * per_step + k)
            for half in range(2):
                sets.append([yg[gl, half, rows, :] for gl in range(SLAB_GROUPS)])
                dests.append(tok + half * SLAB_GROUPS)
        for tok, vs in zip(dests, _transpose_pieces(sets, piece)):
            for rr, v in enumerate(vs):
                y_ref[pl.ds(tok + rr, chunk, stride=SSM_BLOCK), :] = v
        return carry

    lax.fori_loop(0, batch * chunks // per_step, scatter, 0)


def _ssm_call(u2d, h0, tab, wx, wxs, tt, vt, layer, batch, seq):
    n_tok, chans = u2d.shape
    n_slab = chans // LANES
    m_blk = seq // SSM_BLOCK
    pitch = m_blk + SUBLANES
    rows_p = batch * pitch
    st_lanes = SLAB_GROUPS * 2 * LANES
    op_spec = pl.BlockSpec((None, SLAB_GROUPS, MXU_DIM, MXU_DIM), lambda s: (layer, s, 0, 0))
    scratch = pltpu.VMEM((SLAB_GROUPS, 2, rows_p, LANES), F32)
    return pl.pallas_call(
        functools.partial(_ssm_kernel, batch=batch, seq=seq, pitch=pitch),
        grid=(n_slab,),
        in_specs=[
            pl.BlockSpec((n_tok, LANES), lambda s: (0, s)),
            pl.BlockSpec((batch, st_lanes), lambda s: (0, s)),
            pl.BlockSpec((None, SLAB_GROUPS, 2, 2, SUBLANES, LANES), lambda s: (layer, s, 0, 0, 0, 0)),
            op_spec, op_spec, op_spec, op_spec,
        ],
        out_specs=(pl.BlockSpec((n_tok, LANES), lambda s: (0, s)),
                   pl.BlockSpec((batch, st_lanes), lambda s: (0, s))),
        out_shape=(jax.ShapeDtypeStruct((n_tok, chans), F32),
                   jax.ShapeDtypeStruct((batch, n_slab * st_lanes), F32)),
        scratch_shapes=[scratch, scratch, scratch, scratch],
        compiler_params=pltpu.CompilerParams(
            dimension_semantics=("arbitrary",), vmem_limit_bytes=VMEM_LIMIT),
        name=f"ssm_scan_b{batch}",
    )(u2d, h0, tab, wx, wxs, tt, vt)


def _mix_ffn_kernel(attn_ref, y_ref, u_ref, x_ref, mod_ref, dskip_ref, wglu_ref, bglu_ref,
                    wout_ref, g2_ref, wffi_ref, wffo_ref, fn_ref, o_ref, *, final, ff_chunks):
    d_ff = wffo_ref.shape[0]
    y = y_ref[...] + dskip_ref[...] * u_ref[...]
    g = 0.5 * y * (1.0 + lax.erf(y * (2.0 ** -0.5)))
    z = jnp.dot(g.astype(BF16), wglu_ref[...], preferred_element_type=F32) + bglu_ref[...]
    ssm_out = g * jax.nn.sigmoid(z)
    mixed = (jnp.dot(attn_ref[...], wout_ref[0:D_ATTN, :], preferred_element_type=F32)
             + jnp.dot(ssm_out.astype(BF16), wout_ref[D_ATTN:, :], preferred_element_type=F32))
    mod = mod_ref[0]
    x1 = x_ref[...] + mod[2:3] * mixed
    h2 = (_rms(x1, g2_ref[...]) * (1.0 + mod[4:5]) + mod[3:4]).astype(BF16)
    fc = d_ff // ff_chunks
    acc = None
    for ci in range(ff_chunks):
        gate = jnp.dot(h2, wffi_ref[:, ci * fc:(ci + 1) * fc], preferred_element_type=F32)
        up = jnp.dot(h2, wffi_ref[:, d_ff + ci * fc:d_ff + (ci + 1) * fc], preferred_element_type=F32)
        act = (gate * jax.nn.sigmoid(gate) * up).astype(BF16)
        part = jnp.dot(act, wffo_ref[ci * fc:(ci + 1) * fc, :], preferred_element_type=F32)
        acc = part if acc is None else acc + part
    x2 = x1 + mod[5:6] * acc
    if final:
        x2 = _rms(x2, fn_ref[...])
    o_ref[...] = x2


def _mix_ffn_call(attn, y, u, x2d, mods, dskip, wglu_b, bglu, wout_b, g2, wffi_b, wffo_b, fnorm,
                  layer, mod_row, tokens_per_batch, tile, final):
    n_tok, d_model = x2d.shape
    d_ssm = u.shape[1]
    d_ff = wffo_b.shape[-2]
    tiles_per_batch = None if tokens_per_batch is None else tokens_per_batch // tile
    row = lambda width: pl.BlockSpec((tile, width), lambda i: (i, 0))
    return pl.pallas_call(
        functools.partial(_mix_ffn_kernel, final=final, ff_chunks=2),
        grid=(n_tok // tile,),
        in_specs=[
            row(D_ATTN), row(d_ssm), row(d_ssm), row(d_model),
            _mod_spec(d_model, layer, mod_row, tiles_per_batch),
            _layer_spec((1, d_ssm), layer),
            _layer_spec((d_ssm, d_ssm), layer),
            _layer_spec((1, d_ssm), layer),
            _layer_spec((D_ATTN + d_ssm, d_model), layer),
            _layer_spec((1, d_model), layer),
            _layer_spec((d_model, 2 * d_ff), layer),
            _layer_spec((d_ff, d_model), layer),
            _const_spec((1, d_model)),
        ],
        out_specs=row(d_model),
        out_shape=jax.ShapeDtypeStruct((n_tok, d_model), F32),
        compiler_params=pltpu.CompilerParams(
            dimension_semantics=("arbitrary",), vmem_limit_bytes=VMEM_LIMIT),
        name="mix_ffn_final" if final else "mix_ffn",
    )(attn, y, u, x2d, mods, dskip, wglu_b, bglu, wout_b, g2, wffi_b, wffo_b, fnorm)


def _rope_tables(n_tokens):
    axis_dim = HEAD_DIM // 2
    rows = n_tokens // GRID_W
    row = jnp.repeat(jnp.arange(rows, dtype=F32), GRID_W)
    col = jnp.tile(jnp.arange(GRID_W, dtype=F32), rows)
    inv_freq = ROPE_THETA ** (-jnp.arange(0, axis_dim, 2, dtype=F32) / axis_dim)
    ang = jnp.concatenate([row[:, None] * inv_freq, col[:, None] * inv_freq], axis=-1)
    cos = jnp.repeat(jnp.cos(ang), 2, axis=-1)
    sin = jnp.repeat(jnp.sin(ang), 2, axis=-1) * jnp.tile(jnp.array([-1.0, 1.0], F32), HEAD_DIM // 2)
    return jnp.tile(cos, (1, N_Q_HEADS)), jnp.tile(sin, (1, N_Q_HEADS))


def _states_to_lanes(st):
    return st.transpose(0, 3, 1, 2, 4).reshape(st.shape[0], -1)


def _lanes_to_states(rows, n_groups):
    return rows.reshape(rows.shape[0], n_groups, 2, 2, SSM_STATE).transpose(0, 2, 3, 1, 4)


def kernel(x_prompt, x_sample, cache_k, cache_v, state_ssm, c, c_ctx, w_mod, b_mod, norm1, norm2, w_in, q_norm, k_norm, ssm_a_re, ssm_a_im, ssm_log_dt, ssm_b_re, ssm_b_im, ssm_c_re, ssm_c_im, ssm_d, w_glu, b_glu, w_out, w_ffn_in, w_ffn_out, final_norm):
    batch, seq, d_model = x_prompt.shape
    dec_batch, dec_seq, _ = x_sample.shape
    depth = w_in.shape[0]
    past = cache_k.shape[2]
    n_groups = ssm_a_re.shape[2]
    d_ssm = n_groups * SSM_GROUP

    cond = jnp.zeros((SUBLANES, d_model), F32).at[0].set(c_ctx).at[1:1 + dec_batch].set(c)
    mods = _mods_call(cond, w_mod, b_mod).reshape(depth, SUBLANES, N_MOD, d_model)
    tab, wx, wxs, tt, vt = _ssm_prep_call(ssm_a_re, ssm_a_im, ssm_log_dt, ssm_b_re, ssm_b_im,
                                          ssm_c_re, ssm_c_im)
    rope_tabs = _rope_tables(dec_seq)
    head_ids = jnp.arange(D_ATTN) // HEAD_DIM
    ones_bd = (head_ids[:, None] == head_ids[None, :]).astype(BF16)

    xp = x_prompt.reshape(batch * seq, d_model)
    xs = x_sample.reshape(dec_batch * dec_seq, d_model)
    zero_state = jnp.zeros((batch, n_groups * 2 * LANES), F32)
    w_in_b = w_in.astype(BF16)
    wglu_b = w_glu.astype(BF16)
    wout_b = w_out.astype(BF16)
    wffi_b = w_ffn_in.astype(BF16)
    wffo_b = w_ffn_out.astype(BF16)
    g1 = norm1.reshape(depth, 1, d_model)
    g2 = norm2.reshape(depth, 1, d_model)
    qg = jnp.tile(q_norm, (1, N_Q_HEADS)).reshape(depth, 1, D_ATTN)
    kg = jnp.tile(k_norm, (1, N_KV_HEADS)).reshape(depth, 1, D_KV)
    dskip = ssm_d.reshape(depth, 1, d_ssm)
    bglu = b_glu.reshape(depth, 1, d_ssm)
    fnorm = final_norm.reshape(1, d_model)
    new_k, new_v, new_s = [], [], []
    for l in range(depth):
        final = l == depth - 1
        for is_ctx in (True, False):
            if is_ctx:
                x2d, n_b, n_l, tq = xp, batch, seq, seq
                mod_row, tokens_per_batch, tabs, ck, cv, h0 = 0, None, None, None, None, zero_state
            else:
                x2d, n_b, n_l, tq = xs, dec_batch, dec_seq, 512
                mod_row, tokens_per_batch, tabs = 1, dec_seq, rope_tabs
                ck = cache_k[:, l].reshape(dec_batch, past, D_KV)
                cv = cache_v[:, l].reshape(dec_batch, past, D_KV)
                h0 = _states_to_lanes(state_ssm[:, l])
            q, k, v, u = _inproj_call(x2d, mods, g1, w_in_b, qg, kg, ones_bd, tabs, l, mod_row,
                                      tokens_per_batch, TOKEN_TILE)
            attn = _attn_call(q.reshape(n_b, n_l, D_ATTN), k.reshape(n_b, n_l, D_KV),
                              v.reshape(n_b, n_l, D_KV), ck, cv, tq)
            y, ht = _ssm_call(u, h0, tab, wx, wxs, tt, vt, l, n_b, n_l)
            x_new = _mix_ffn_call(attn.reshape(-1, D_ATTN), y, u, x2d, mods, dskip, wglu_b, bglu, wout_b,
                                  g2, wffi_b, wffo_b, fnorm, l, mod_row, tokens_per_batch, TOKEN_TILE,
                                  final)
            if is_ctx:
                xp = x_new
                new_k.append(k.reshape(batch, seq, N_KV_HEADS, HEAD_DIM))
                new_v.append(v.reshape(batch, seq, N_KV_HEADS, HEAD_DIM))
                new_s.append(_lanes_to_states(ht, n_groups))
            else:
                xs = x_new
    return (xp.reshape(batch, seq, d_model), xs.reshape(dec_batch, dec_seq, d_model),
            jnp.stack(new_k, axis=1), jnp.stack(new_v, axis=1), jnp.stack(new_s, axis=1))
```

```python
import functools

import jax
import jax.numpy as jnp
from jax import lax
from jax.experimental import pallas as pl
from jax.experimental.pallas import tpu as pltpu

F32 = jnp.float32
BF16 = jnp.bfloat16

HEAD_DIM = 64
N_Q_HEADS = 8
N_KV_HEADS = 2
D_ATTN = N_Q_HEADS * HEAD_DIM
D_KV = N_KV_HEADS * HEAD_DIM
SSM_GROUP = 16
SSM_STATE = 64
GRID_W = 64
ROPE_THETA = 10000.0
N_MOD = 6
EPS = 1e-6

LANES = 128
SUBLANES = 8
MXU_DIM = 256
SSM_BLOCK = MXU_DIM // SSM_GROUP
SLAB_GROUPS = LANES // SSM_GROUP
VMEM_LIMIT = 56 * 1024 * 1024
TOKEN_TILE = 512

_NT = (((1,), (1,)), ((), ()))


def _const_spec(shape):
    nd = len(shape)
    return pl.BlockSpec(shape, lambda *_: (0,) * nd, pipeline_mode=pl.Buffered(1))


def _layer_spec(shape, layer):
    nd = len(shape)
    return pl.BlockSpec((None,) + shape, lambda *_: (layer,) + (0,) * nd,
                        pipeline_mode=pl.Buffered(1))


def _mod_spec(d_model, layer, first_row, tiles_per_batch):
    if tiles_per_batch is None:
        return pl.BlockSpec((None, 1, N_MOD, d_model), lambda i: (layer, first_row, 0, 0))
    return pl.BlockSpec((None, 1, N_MOD, d_model),
                        lambda i: (layer, first_row + i // tiles_per_batch, 0, 0))


def _rms(x, gain):
    ms = jnp.mean(x * x, axis=-1, keepdims=True)
    return x * lax.rsqrt(ms + EPS) * gain


def _mods_kernel(cond_ref, w_ref, b_ref, o_ref):
    c = cond_ref[...]
    act = c * jax.nn.sigmoid(c)
    o_ref[0] = jnp.dot(act.astype(BF16), w_ref[0].astype(BF16), preferred_element_type=F32) + b_ref[0]


def _mods_call(cond, w_mod, b_mod):
    depth, d_model, n_out = w_mod.shape
    rows = cond.shape[0]
    nt = 4
    tn = n_out // nt
    return pl.pallas_call(
        _mods_kernel,
        grid=(depth, nt),
        in_specs=[
            pl.BlockSpec((rows, d_model), lambda l, j: (0, 0)),
            pl.BlockSpec((1, d_model, tn), lambda l, j: (l, 0, j)),
            pl.BlockSpec((1, 1, tn), lambda l, j: (l, 0, j)),
        ],
        out_specs=pl.BlockSpec((1, rows, tn), lambda l, j: (l, 0, j)),
        out_shape=jax.ShapeDtypeStruct((depth, rows, n_out), F32),
        compiler_params=pltpu.CompilerParams(
            dimension_semantics=("arbitrary", "arbitrary"), vmem_limit_bytes=VMEM_LIMIT),
        name="adaln_mods",
    )(cond, w_mod, b_mod.reshape(depth, 1, n_out))


def _group_sumsq(z, ones_ref, width):
    z2 = z * z
    hi = z2.astype(BF16)
    lo = (z2 - hi.astype(F32)).astype(BF16)
    step = min(width, MXU_DIM)
    ones = ones_ref[0:step, 0:step]
    cols = [jnp.dot(hi[:, c:c + step], ones, preferred_element_type=F32)
            + jnp.dot(lo[:, c:c + step], ones, preferred_element_type=F32)
            for c in range(0, width, step)]
    return cols[0] if len(cols) == 1 else jnp.concatenate(cols, axis=1)


def _head_rms(z, gain, ones_ref):
    ss = _group_sumsq(z, ones_ref, z.shape[-1])
    return z * lax.rsqrt(ss * (1.0 / HEAD_DIM) + EPS) * gain


def _rope(z, cos, sin_signed):
    width = z.shape[-1]
    lane = lax.broadcasted_iota(jnp.int32, z.shape, 1)
    nxt = pltpu.roll(z, width - 1, axis=1)
    prv = pltpu.roll(z, 1, axis=1)
    partner = jnp.where((lane & 1) == 0, nxt, prv)
    return z * cos + partner * sin_signed


def _inproj_kernel(*refs, rope):
    if rope:
        (x_ref, mod_ref, g1_ref, w_ref, qg_ref, kg_ref, ones_ref, cos_ref, sin_ref,
         q_ref, k_ref, v_ref, u_ref) = refs
    else:
        (x_ref, mod_ref, g1_ref, w_ref, qg_ref, kg_ref, ones_ref,
         q_ref, k_ref, v_ref, u_ref) = refs
    mod = mod_ref[0]
    h = _rms(x_ref[...], g1_ref[...]) * (1.0 + mod[1:2]) + mod[0:1]
    proj = jnp.dot(h.astype(BF16), w_ref[...], preferred_element_type=F32)
    q = _head_rms(proj[:, :D_ATTN], qg_ref[...], ones_ref)
    k = _head_rms(proj[:, D_ATTN:D_ATTN + D_KV], kg_ref[...], ones_ref)
    if rope:
        q = _rope(q, cos_ref[...], sin_ref[...])
        k = _rope(k, cos_ref[:, 0:D_KV], sin_ref[:, 0:D_KV])
    q_ref[...] = (q * (HEAD_DIM ** -0.5)).astype(BF16)
    k_ref[...] = k
    v_ref[...] = proj[:, D_ATTN + D_KV:D_ATTN + 2 * D_KV]
    u_ref[...] = proj[:, D_ATTN + 2 * D_KV:]


def _inproj_call(x2d, mods, g1, w_in_b, qg, kg, ones_bd, rope_tabs, layer, mod_row, tokens_per_batch,
                 tile):
    n_tok, d_model = x2d.shape
    d_in = w_in_b.shape[-1]
    d_ssm = d_in - D_ATTN - 2 * D_KV
    tiles_per_batch = None if tokens_per_batch is None else tokens_per_batch // tile
    in_specs = [
        pl.BlockSpec((tile, d_model), lambda i: (i, 0)),
        _mod_spec(d_model, layer, mod_row, tiles_per_batch),
        _layer_spec((1, d_model), layer),
        _layer_spec((d_model, d_in), layer),
        _layer_spec((1, D_ATTN), layer),
        _layer_spec((1, D_KV), layer),
        _const_spec((D_ATTN, D_ATTN)),
    ]
    args = [x2d, mods, g1, w_in_b, qg, kg, ones_bd]
    if rope_tabs is not None:
        in_specs += [pl.BlockSpec((tile, D_ATTN), lambda i: (i % tiles_per_batch, 0))] * 2
        args += list(rope_tabs)
    row = lambda width: pl.BlockSpec((tile, width), lambda i: (i, 0))
    return pl.pallas_call(
        functools.partial(_inproj_kernel, rope=rope_tabs is not None),
        grid=(n_tok // tile,),
        in_specs=in_specs,
        out_specs=(row(D_ATTN), row(D_KV), row(D_KV), row(d_ssm)),
        out_shape=(jax.ShapeDtypeStruct((n_tok, D_ATTN), BF16),
                   jax.ShapeDtypeStruct((n_tok, D_KV), F32),
                   jax.ShapeDtypeStruct((n_tok, D_KV), F32),
                   jax.ShapeDtypeStruct((n_tok, d_ssm), F32)),
        compiler_params=pltpu.CompilerParams(
            dimension_semantics=("arbitrary",), vmem_limit_bytes=VMEM_LIMIT),
        name="inproj_rope" if rope_tabs is not None else "inproj",
    )(*args)


def _attn_kernel(*refs, has_cache):
    if has_cache:
        q_ref, kn_ref, vn_ref, ck_ref, cv_ref, o_ref, kvar, vvar = refs
    else:
        q_ref, kn_ref, vn_ref, o_ref, kvar, vvar = refs

    @pl.when(pl.program_id(1) == 0)
    def _():
        for new_ref, cache_ref, dst in ((kn_ref, ck_ref if has_cache else None, kvar),
                                        (vn_ref, cv_ref if has_cache else None, vvar)):
            src = new_ref[0]
            if has_cache:
                src = jnp.concatenate([cache_ref[0], src], axis=0)
            low = lax.broadcasted_iota(jnp.int32, src.shape, 1) < HEAD_DIM
            head0 = jnp.where(low, src, 0.0)
            head1 = jnp.where(low, 0.0, src)
            dst[0] = head0.astype(BF16)
            dst[1] = pltpu.roll(head0, HEAD_DIM, axis=1).astype(BF16)
            dst[2] = pltpu.roll(head1, HEAD_DIM, axis=1).astype(BF16)
            dst[3] = head1.astype(BF16)

    slabs = D_ATTN // LANES
    for slab in range(slabs):
        qs = q_ref[0, :, slab * LANES:(slab + 1) * LANES]
        kv = slab // (slabs // N_KV_HEADS)
        acc = None
        for half in range(2):
            idx = kv * 2 + half
            s = lax.dot_general(qs, kvar[idx], _NT, preferred_element_type=F32)
            m = jnp.max(s, axis=-1, keepdims=True)
            p = jnp.exp(s - m)
            denom = jnp.sum(p, axis=-1, keepdims=True)
            o = jnp.dot(p.astype(BF16), vvar[idx], preferred_element_type=F32) / denom
            acc = o if acc is None else acc + o
        o_ref[0, :, slab * LANES:(slab + 1) * LANES] = acc.astype(BF16)


def _attn_call(q3, k3, v3, ck3, cv3, tq):
    b, l, _ = q3.shape
    has_cache = ck3 is not None
    s_len = l + (ck3.shape[1] if has_cache else 0)
    in_specs = [
        pl.BlockSpec((1, tq, D_ATTN), lambda bi, qi: (bi, qi, 0)),
        pl.BlockSpec((1, l, D_KV), lambda bi, qi: (bi, 0, 0)),
        pl.BlockSpec((1, l, D_KV), lambda bi, qi: (bi, 0, 0)),
    ]
    args = [q3, k3, v3]
    if has_cache:
        in_specs += [pl.BlockSpec((1, ck3.shape[1], D_KV), lambda bi, qi: (bi, 0, 0))] * 2
        args += [ck3, cv3]
    return pl.pallas_call(
        functools.partial(_attn_kernel, has_cache=has_cache),
        grid=(b, l // tq),
        in_specs=in_specs,
        out_specs=pl.BlockSpec((1, tq, D_ATTN), lambda bi, qi: (bi, qi, 0)),
        out_shape=jax.ShapeDtypeStruct((b, l, D_ATTN), BF16),
        scratch_shapes=[pltpu.VMEM((4, s_len, D_KV), BF16), pltpu.VMEM((4, s_len, D_KV), BF16)],
        compiler_params=pltpu.CompilerParams(
            dimension_semantics=("arbitrary", "arbitrary"), vmem_limit_bytes=VMEM_LIMIT),
        name="attn_cache" if has_cache else "attn",
    )(*args)


def _dot3(a, b):
    a_hi = a.astype(BF16)
    b_hi = b.astype(BF16)
    a_lo = (a - a_hi.astype(F32)).astype(BF16)
    b_lo = (b - b_hi.astype(F32)).astype(BF16)
    dot = lambda x, y: lax.dot_general(x, y, _NT, preferred_element_type=F32)
    return dot(a_hi, b_hi) + (dot(a_hi, b_lo) + dot(a_lo, b_hi))


def _ssm_prep_kernel(are_ref, aim_ref, ldt_ref, bre_ref, bim_ref, cre_ref, cim_ref,
                     tab_ref, wx_ref, wxs_ref, vt_ref, tt_ref, clm, lbm):
    n_groups = wx_ref.shape[0]
    width = SSM_BLOCK * SSM_GROUP
    shape = (SSM_GROUP, LANES)
    low = lax.broadcasted_iota(jnp.int32, shape, 1) < SSM_STATE
    row_blk = lax.broadcasted_iota(jnp.int32, (width, width), 0) // SSM_GROUP
    col_blk = lax.broadcasted_iota(jnp.int32, (width, width), 1) // SSM_GROUP

    def cmul(xr, xi, yr, yi):
        return xr * yr - xi * yi, xr * yi + xi * yr

    def group(g, carry):
        rows = pl.ds(pl.multiple_of(g * SSM_GROUP, SSM_GROUP), SSM_GROUP)
        tile = None
        for d in range(2):
            lanes = slice(d * LANES, (d + 1) * LANES)
            dt = jnp.exp(ldt_ref[d, rows, :])
            ar = are_ref[d, rows, :]
            ai = aim_ref[d, rows, :]
            mag = jnp.exp(ar * dt)
            lr = mag * jnp.cos(ai * dt)
            li = mag * jnp.sin(ai * dt)
            den = ar * ar + ai * ai
            zr = ((lr - 1.0) * ar + li * ai) / den
            zi = (li * ar - (lr - 1.0) * ai) / den
            bbr, bbi = cmul(zr, zi, bre_ref[d, rows, :], bim_ref[d, rows, :])
            cr = cre_ref[d, rows, :]
            ci = cim_ref[d, rows, :]
            inv = 1.0 / (lr * lr + li * li)
            ir = lr * inv
            ii = -li * inv
            pr, pi = jnp.ones(shape, F32), jnp.zeros(shape, F32)
            qr, qi = pr, pi
            for j in range(SSM_BLOCK + 1):
                c_re, c_im = cmul(cr, ci, pr, pi)
                c_pos = jnp.where(low, c_re, -c_im)
                if j >= 1:
                    vt_ref[g, (j - 1) if d == 0 else (SSM_BLOCK - j), :, lanes] = c_pos.astype(BF16)
                if j < SSM_BLOCK:
                    g_re, g_im = cmul(pr, pi, bbr, bbi)
                    g_pos = jnp.where(low, g_re, g_im)
                    r = (SSM_BLOCK - 1 - j) if d == 0 else j
                    wx_ref[g, r, :, lanes] = g_pos.astype(BF16)
                    wxs_ref[g, r, :, lanes] = jnp.where(low, g_im, g_re).astype(BF16)
                    blk = pl.ds(j * SSM_GROUP, SSM_GROUP)
                    if d == 0:
                        n_re, n_im = cmul(qr, qi, bbr, bbi)
                        clm[blk, :] = c_pos
                        lbm[blk, :] = jnp.where(low, n_re, n_im)
                    else:
                        m_re, m_im = cmul(cr, ci, qr, qi)
                        clm[blk, :] = jnp.where(low, m_re, -m_im)
                        lbm[blk, :] = g_pos
                else:
                    tab_ref[g, d, 0] = pr[0:SUBLANES]
                    tab_ref[g, d, 1] = jnp.where(low, -pi, pi)[0:SUBLANES]
                pr, pi = cmul(pr, pi, lr, li)
                qr, qi = cmul(qr, qi, ir, ii)
            t = _dot3(clm[...], lbm[...])
            t = jnp.where((col_blk <= row_blk) if d == 0 else (col_blk >= row_blk), t, 0.0)
            tile = t if tile is None else tile + t
        tt_ref[g] = tile.astype(BF16)
        return carry

    lax.fori_loop(0, n_groups, group, 0)


def _ssm_prep_call(a_re, a_im, log_dt, b_re, b_im, c_re, c_im):
    depth, ndir, g, p, h = b_re.shape
    halves = 2
    g_half = g // halves
    rows = g_half * h
    width = SSM_BLOCK * h

    def rows_lanes(a):
        a = a.astype(F32).reshape(depth, ndir, g * h, p)
        return jnp.concatenate([a, a], axis=-1)

    per_state = lambda a: rows_lanes(jnp.broadcast_to(a[:, :, :, None, :], (depth, ndir, g, h, p)))
    ldt = rows_lanes(jnp.broadcast_to(log_dt[:, :, :, None, None], (depth, ndir, g, h, p)))
    to_hp = lambda a: rows_lanes(jnp.swapaxes(a, -1, -2))
    in_spec = pl.BlockSpec((None, ndir, rows, LANES), lambda l, s: (l, 0, s, 0))
    out_blk = lambda *shape: pl.BlockSpec((None, g_half) + shape, lambda l, s: (l, s) + (0,) * len(shape))
    op4 = jax.ShapeDtypeStruct((depth, g, SSM_BLOCK, h, 2 * LANES), BF16)
    tab, wx, wxs, vt, tt = pl.pallas_call(
        _ssm_prep_kernel,
        grid=(depth, halves),
        in_specs=[in_spec] * 7,
        out_specs=(out_blk(2, 2, SUBLANES, LANES), out_blk(SSM_BLOCK, h, 2 * LANES),
                   out_blk(SSM_BLOCK, h, 2 * LANES), out_blk(SSM_BLOCK, h, 2 * LANES),
                   out_blk(width, width)),
        out_shape=(jax.ShapeDtypeStruct((depth, g, 2, 2, SUBLANES, LANES), F32), op4, op4, op4,
                   jax.ShapeDtypeStruct((depth, g, width, width), BF16)),
        scratch_shapes=[pltpu.VMEM((width, LANES), F32), pltpu.VMEM((width, LANES), F32)],
        compiler_params=pltpu.CompilerParams(
            dimension_semantics=("arbitrary", "arbitrary"), vmem_limit_bytes=VMEM_LIMIT),
        name="ssm_prepare",
    )(per_state(a_re), per_state(a_im), ldt, to_hp(b_re), to_hp(b_im), rows_lanes(c_re), rows_lanes(c_im))
    merge = lambda a: a.reshape(depth, g, width, 2 * LANES)
    return tab, merge(wx), merge(wxs), tt, merge(vt)


def _transpose_pieces(sets, piece):
    sets = [list(vs) for vs in sets]
    n = len(sets[0])
    s = n // 2
    while s:
        keep_low = (piece & s) == 0
        pairs = [(vs, i) for vs in sets for i in range(n) if not i & s]
        moved = [(pltpu.roll(vs[i + s], s * SSM_GROUP, axis=1),
                  pltpu.roll(vs[i], LANES - s * SSM_GROUP, axis=1)) for vs, i in pairs]
        for (vs, i), (from_hi, from_lo) in zip(pairs, moved):
            vs[i], vs[i + s] = jnp.where(keep_low, vs[i], from_hi), jnp.where(keep_low, from_lo, vs[i + s])
        s //= 2
    return sets


def _ssm_kernel(u_ref, h0_ref, tab_ref, wx_ref, wxs_ref, tt_ref, vt_ref, y_ref, ht_ref, ug, xb, xsb, yg,
                *, batch, seq, pitch):
    m_blk = seq // SSM_BLOCK
    chunk = 2 * SUBLANES
    chunks = m_blk // chunk
    per_step = 4
    piece = lax.broadcasted_iota(jnp.int32, (chunk, LANES), 1) // SSM_GROUP

    @pl.when(pl.program_id(0) == 0)
    def _():
        ug[...] = jnp.zeros(ug.shape, F32)

    def chunk_rows(idx):
        b, ch = idx // chunks, idx % chunks
        tok = b * seq + ch * (chunk * SSM_BLOCK)
        return tok, pl.ds(pl.multiple_of(b * pitch + ch * chunk, SUBLANES), chunk)

    def gather(i, carry):
        sets, dests = [], []
        for k in range(per_step):
            tok, rows = chunk_rows(i * per_step + k)
            for half in range(2):
                sets.append([u_ref[pl.ds(tok + half * SLAB_GROUPS + rr, chunk, stride=SSM_BLOCK), :]
                             for rr in range(SLAB_GROUPS)])
                dests.append((half, rows))
        for (half, rows), vs in zip(dests, _transpose_pieces(sets, piece)):
            for gl, v in enumerate(vs):
                ug[gl, half, rows, :] = v
        return carry

    lax.fori_loop(0, batch * chunks // per_step, gather, 0)

    for gl in range(SLAB_GROUPS):
        lhs = jnp.concatenate([ug[gl, 0], ug[gl, 1]], axis=1).astype(BF16)
        x = jnp.dot(lhs, wx_ref[gl], preferred_element_type=F32)
        xb[gl, 0] = x[:, :LANES]
        xb[gl, 1] = x[:, LANES:]
        x = jnp.dot(lhs, wxs_ref[gl], preferred_element_type=F32)
        xsb[gl, 0] = x[:, :LANES]
        xsb[gl, 1] = x[:, LANES:]
        y_in = lax.dot_general(lhs, tt_ref[gl], _NT, preferred_element_type=F32)
        yg[gl, 0] = y_in[:, :LANES]
        yg[gl, 1] = y_in[:, LANES:]

    n_bt = -(-batch // SUBLANES)
    rows_per = min(batch, SUBLANES)
    per_pass = max(1, 4 // n_bt)
    for g0 in range(0, SLAB_GROUPS, per_pass):
        keys = [(gl, d, bt) for gl in range(g0, g0 + per_pass) for d in range(2) for bt in range(n_bt)]
        lane0 = lambda gl, d: (gl * 2 + d) * LANES
        init = []
        for gl, d, bt in keys:
            h = h0_ref[bt * SUBLANES:bt * SUBLANES + rows_per, lane0(gl, d):lane0(gl, d) + LANES]
            init += [h, pltpu.roll(h, SSM_STATE, axis=1)]
        mult = {(gl, d): (tab_ref[gl, d, 0, 0:rows_per, :], tab_ref[gl, d, 1, 0:rows_per, :])
                for gl, d, _ in keys}

        def step(m, hs, keys=keys, mult=mult):
            out = []
            for k, (gl, d, bt) in enumerate(keys):
                h, h_sw = hs[2 * k], hs[2 * k + 1]
                row = m if d == 0 else m_blk - 1 - m
                idx = pl.ds(bt * SUBLANES * pitch + row, rows_per, stride=pitch)
                x = xb[gl, d, idx, :]
                x_sw = xsb[gl, d, idx, :]
                ug[gl, d, idx, :] = h
                a, b = mult[(gl, d)]
                out += [a * h + b * h_sw + x, a * h_sw - b * h + x_sw]
            return tuple(out)

        final = lax.fori_loop(0, m_blk, step, tuple(init), unroll=2)
        for k, (gl, d, bt) in enumerate(keys):
            ht_ref[bt * SUBLANES:bt * SUBLANES + rows_per, lane0(gl, d):lane0(gl, d) + LANES] = final[2 * k]

    for gl in range(SLAB_GROUPS):
        states = jnp.concatenate([ug[gl, 0], ug[gl, 1]], axis=1).astype(BF16)
        y_st = lax.dot_general(states, vt_ref[gl], _NT, preferred_element_type=F32)
        yg[gl, 0] = yg[gl, 0] + y_st[:, :LANES]
        yg[gl, 1] = yg[gl, 1] + y_st[:, LANES:]

    def scatter(i, carry):
        sets, dests = [], []
        for k in range(per_step):
            tok, rows = chunk_rows(i * per_step + k)
            for half in range(2):
                sets.append([yg[gl, half, rows, :] for gl in range(SLAB_GROUPS)])
                dests.append(tok + half * SLAB_GROUPS)
        for tok, vs in zip(dests, _transpose_pieces(sets, piece)):
            for rr, v in enumerate(vs):
                y_ref[pl.ds(tok + rr, chunk, stride=SSM_BLOCK), :] = v
        return carry

    lax.fori_loop(0, batch * chunks // per_step, scatter, 0)


def _ssm_call(u2d, h0, tab, wx, wxs, tt, vt, layer, batch, seq):
    n_tok, chans = u2d.shape
    n_slab = chans // LANES
    m_blk = seq // SSM_BLOCK
    pitch = m_blk + SUBLANES
    rows_p = batch * pitch
    st_lanes = SLAB_GROUPS * 2 * LANES
    op_spec = pl.BlockSpec((None, SLAB_GROUPS, MXU_DIM, MXU_DIM), lambda s: (layer, s, 0, 0))
    scratch = pltpu.VMEM((SLAB_GROUPS, 2, rows_p, LANES), F32)
    return pl.pallas_call(
        functools.partial(_ssm_kernel, batch=batch, seq=seq, pitch=pitch),
        grid=(n_slab,),
        in_specs=[
            pl.BlockSpec((n_tok, LANES), lambda s: (0, s)),
            pl.BlockSpec((batch, st_lanes), lambda s: (0, s)),
            pl.BlockSpec((None, SLAB_GROUPS, 2, 2, SUBLANES, LANES), lambda s: (layer, s, 0, 0, 0, 0)),
            op_spec, op_spec, op_spec, op_spec,
        ],
        out_specs=(pl.BlockSpec((n_tok, LANES), lambda s: (0, s)),
                   pl.BlockSpec((batch, st_lanes), lambda s: (0, s))),
        out_shape=(jax.ShapeDtypeStruct((n_tok, chans), F32),
                   jax.ShapeDtypeStruct((batch, n_slab * st_lanes), F32)),
        scratch_shapes=[scratch, scratch, scratch, scratch],
        compiler_params=pltpu.CompilerParams(
            dimension_semantics=("arbitrary",), vmem_limit_bytes=VMEM_LIMIT),
        name=f"ssm_scan_b{batch}",
    )(u2d, h0, tab, wx, wxs, tt, vt)


def _mix_ffn_kernel(attn_ref, y_ref, u_ref, x_ref, mod_ref, dskip_ref, wglu_ref, bglu_ref,
                    wout_ref, g2_ref, wffi_ref, wffo_ref, fn_ref, o_ref, *, final, ff_chunks):
    d_ff = wffo_ref.shape[0]
    y = y_ref[...] + dskip_ref[...] * u_ref[...]
    g = 0.5 * y * (1.0 + lax.erf(y * (2.0 ** -0.5)))
    z = jnp.dot(g.astype(BF16), wglu_ref[...], preferred_element_type=F32) + bglu_ref[...]
    ssm_out = g * jax.nn.sigmoid(z)
    mixed = (jnp.dot(attn_ref[...], wout_ref[0:D_ATTN, :], preferred_element_type=F32)
             + jnp.dot(ssm_out.astype(BF16), wout_ref[D_ATTN:, :], preferred_element_type=F32))
    mod = mod_ref[0]
    x1 = x_ref[...] + mod[2:3] * mixed
    h2 = (_rms(x1, g2_ref[...]) * (1.0 + mod[4:5]) + mod[3:4]).astype(BF16)
    tiles = d_ff // MXU_DIM
    edges = [MXU_DIM * ((tiles * ci) // ff_chunks) for ci in range(ff_chunks)] + [d_ff]
    acc = None
    for c0, c1 in zip(edges[:-1], edges[1:]):
        gate = jnp.dot(h2, wffi_ref[:, c0:c1], preferred_element_type=F32)
        up = jnp.dot(h2, wffi_ref[:, d_ff + c0:d_ff + c1], preferred_element_type=F32)
        act = (gate * jax.nn.sigmoid(gate) * up).astype(BF16)
        part = jnp.dot(act, wffo_ref[c0:c1, :], preferred_element_type=F32)
        acc = part if acc is None else acc + part
    x2 = x1 + mod[5:6] * acc
    if final:
        x2 = _rms(x2, fn_ref[...])
    o_ref[...] = x2


def _mix_ffn_call(attn, y, u, x2d, mods, dskip, wglu_b, bglu, wout_b, g2, wffi_b, wffo_b, fnorm,
                  layer, mod_row, tokens_per_batch, tile, final):
    n_tok, d_model = x2d.shape
    d_ssm = u.shape[1]
    d_ff = wffo_b.shape[-2]
    tiles_per_batch = None if tokens_per_batch is None else tokens_per_batch // tile
    row = lambda width: pl.BlockSpec((tile, width), lambda i: (i, 0))
    return pl.pallas_call(
        functools.partial(_mix_ffn_kernel, final=final, ff_chunks=2),
        grid=(n_tok // tile,),
        in_specs=[
            row(D_ATTN), row(d_ssm), row(d_ssm), row(d_model),
            _mod_spec(d_model, layer, mod_row, tiles_per_batch),
            _layer_spec((1, d_ssm), layer),
            _layer_spec((d_ssm, d_ssm), layer),
            _layer_spec((1, d_ssm), layer),
            _layer_spec((D_ATTN + d_ssm, d_model), layer),
            _layer_spec((1, d_model), layer),
            _layer_spec((d_model, 2 * d_ff), layer),
            _layer_spec((d_ff, d_model), layer),
            _const_spec((1, d_model)),
        ],
        out_specs=row(d_model),
        out_shape=jax.ShapeDtypeStruct((n_tok, d_model), F32),
        compiler_params=pltpu.CompilerParams(
            dimension_semantics=("arbitrary",), vmem_limit_bytes=VMEM_LIMIT),
        name="mix_ffn_final" if final else "mix_ffn",
    )(attn, y, u, x2d, mods, dskip, wglu_b, bglu, wout_b, g2, wffi_b, wffo_b, fnorm)


def _rope_tables(n_tokens):
    axis_dim = HEAD_DIM // 2
    rows = n_tokens // GRID_W
    row = jnp.repeat(jnp.arange(rows, dtype=F32), GRID_W)
    col = jnp.tile(jnp.arange(GRID_W, dtype=F32), rows)
    inv_freq = ROPE_THETA ** (-jnp.arange(0, axis_dim, 2, dtype=F32) / axis_dim)
    ang = jnp.concatenate([row[:, None] * inv_freq, col[:, None] * inv_freq], axis=-1)
    cos = jnp.repeat(jnp.cos(ang), 2, axis=-1)
    sin = jnp.repeat(jnp.sin(ang), 2, axis=-1) * jnp.tile(jnp.array([-1.0, 1.0], F32), HEAD_DIM // 2)
    return jnp.tile(cos, (1, N_Q_HEADS)), jnp.tile(sin, (1, N_Q_HEADS))


def _states_to_lanes(st):
    return st.transpose(0, 3, 1, 2, 4).reshape(st.shape[0], -1)


def _lanes_to_states(rows, n_groups):
    return rows.reshape(rows.shape[0], n_groups, 2, 2, SSM_STATE).transpose(0, 2, 3, 1, 4)


def kernel(x_prompt, x_sample, cache_k, cache_v, state_ssm, c, c_ctx, w_mod, b_mod, norm1, norm2, w_in, q_norm, k_norm, ssm_a_re, ssm_a_im, ssm_log_dt, ssm_b_re, ssm_b_im, ssm_c_re, ssm_c_im, ssm_d, w_glu, b_glu, w_out, w_ffn_in, w_ffn_out, final_norm):
    batch, seq, d_model = x_prompt.shape
    dec_batch, dec_seq, _ = x_sample.shape
    depth = w_in.shape[0]
    past = cache_k.shape[2]
    n_groups = ssm_a_re.shape[2]
    d_ssm = n_groups * SSM_GROUP

    cond = jnp.zeros((SUBLANES, d_model), F32).at[0].set(c_ctx).at[1:1 + dec_batch].set(c)
    mods = _mods_call(cond, w_mod, b_mod).reshape(depth, SUBLANES, N_MOD, d_model)
    tab, wx, wxs, tt, vt = _ssm_prep_call(ssm_a_re, ssm_a_im, ssm_log_dt, ssm_b_re, ssm_b_im,
                                          ssm_c_re, ssm_c_im)
    rope_tabs = _rope_tables(dec_seq)
    head_ids = jnp.arange(D_ATTN) // HEAD_DIM
    ones_bd = (head_ids[:, None] == head_ids[None, :]).astype(BF16)

    xp = x_prompt.reshape(batch * seq, d_model)
    xs = x_sample.reshape(dec_batch * dec_seq, d_model)
    zero_state = jnp.zeros((batch, n_groups * 2 * LANES), F32)
    w_in_b = w_in.astype(BF16)
    wglu_b = w_glu.astype(BF16)
    wout_b = w_out.astype(BF16)
    wffi_b = w_ffn_in.astype(BF16)
    wffo_b = w_ffn_out.astype(BF16)
    g1 = norm1.reshape(depth, 1, d_model)
    g2 = norm2.reshape(depth, 1, d_model)
    qg = jnp.tile(q_norm, (1, N_Q_HEADS)).reshape(depth, 1, D_ATTN)
    kg = jnp.tile(k_norm, (1, N_KV_HEADS)).reshape(depth, 1, D_KV)
    dskip = ssm_d.reshape(depth, 1, d_ssm)
    bglu = b_glu.reshape(depth, 1, d_ssm)
    fnorm = final_norm.reshape(1, d_model)
    new_k, new_v, new_s = [], [], []
    for l in range(depth):
        final = l == depth - 1
        for is_ctx in (True, False):
            if is_ctx:
                x2d, n_b, n_l, tq = xp, batch, seq, seq
                mod_row, tokens_per_batch, tabs, ck, cv, h0 = 0, None, None, None, None, zero_state
            else:
                x2d, n_b, n_l, tq = xs, dec_batch, dec_seq, 512
                mod_row, tokens_per_batch, tabs = 1, dec_seq, rope_tabs
                ck = cache_k[:, l].reshape(dec_batch, past, D_KV)
                cv = cache_v[:, l].reshape(dec_batch, past, D_KV)
                h0 = _states_to_lanes(state_ssm[:, l])
            q, k, v, u = _inproj_call(x2d, mods, g1, w_in_b, qg, kg, ones_bd, tabs, l, mod_row,
                                      tokens_per_batch, TOKEN_TILE)
            attn = _attn_call(q.reshape(n_b, n_l, D_ATTN), k.reshape(n_b, n_l, D_KV),
                              v.reshape(n_b, n_l, D_KV), ck, cv, tq)
            y, ht = _ssm_call(u, h0, tab, wx, wxs, tt, vt, l, n_b, n_l)
            x_new = _mix_ffn_call(attn.reshape(-1, D_ATTN), y, u, x2d, mods, dskip, wglu_b, bglu, wout_b,
                                  g2, wffi_b, wffo_b, fnorm, l, mod_row, tokens_per_batch, TOKEN_TILE,
                                  final)
            if is_ctx:
                xp = x_new
                new_k.append(k.reshape(batch, seq, N_KV_HEADS, HEAD_DIM))
                new_v.append(v.reshape(batch, seq, N_KV_HEADS, HEAD_DIM))
                new_s.append(_lanes_to_states(ht, n_groups))
            else:
                xs = x_new
    return (xp.reshape(batch, seq, d_model), xs.reshape(dec_batch, dec_seq, d_model),
            jnp.stack(new_k, axis=1), jnp.stack(new_v, axis=1), jnp.stack(new_s, axis=1))
```

```python
import functools

import jax
import jax.numpy as jnp
from jax import lax
from jax.experimental import pallas as pl
from jax.experimental.pallas import tpu as pltpu

F32 = jnp.float32
BF16 = jnp.bfloat16

HEAD_DIM = 64
N_Q_HEADS = 8
N_KV_HEADS = 2
D_ATTN = N_Q_HEADS * HEAD_DIM
D_KV = N_KV_HEADS * HEAD_DIM
SSM_GROUP = 16
SSM_STATE = 64
GRID_W = 64
ROPE_THETA = 10000.0
N_MOD = 6
EPS = 1e-6

LANES = 128
SUBLANES = 8
MXU_DIM = 256
SSM_BLOCK = MXU_DIM // SSM_GROUP
SLAB_GROUPS = LANES // SSM_GROUP
VMEM_LIMIT = 56 * 1024 * 1024
TOKEN_TILE = 512

_NT = (((1,), (1,)), ((), ()))


def _const_spec(shape):
    nd = len(shape)
    return pl.BlockSpec(shape, lambda *_: (0,) * nd, pipeline_mode=pl.Buffered(1))


def _layer_spec(shape, layer):
    nd = len(shape)
    return pl.BlockSpec((None,) + shape, lambda *_: (layer,) + (0,) * nd,
                        pipeline_mode=pl.Buffered(1))


def _mod_spec(d_model, layer, first_row, tiles_per_batch):
    if tiles_per_batch is None:
        return pl.BlockSpec((None, 1, N_MOD, d_model), lambda i: (layer, first_row, 0, 0))
    return pl.BlockSpec((None, 1, N_MOD, d_model),
                        lambda i: (layer, first_row + i // tiles_per_batch, 0, 0))


def _rms(x, gain):
    ms = jnp.mean(x * x, axis=-1, keepdims=True)
    return x * lax.rsqrt(ms + EPS) * gain


def _mods_kernel(cond_ref, w_ref, b_ref, o_ref):
    c = cond_ref[...]
    act = c * jax.nn.sigmoid(c)
    o_ref[0] = jnp.dot(act.astype(BF16), w_ref[0].astype(BF16), preferred_element_type=F32) + b_ref[0]


def _mods_call(cond, w_mod, b_mod):
    depth, d_model, n_out = w_mod.shape
    rows = cond.shape[0]
    nt = 4
    tn = n_out // nt
    return pl.pallas_call(
        _mods_kernel,
        grid=(depth, nt),
        in_specs=[
            pl.BlockSpec((rows, d_model), lambda l, j: (0, 0)),
            pl.BlockSpec((1, d_model, tn), lambda l, j: (l, 0, j)),
            pl.BlockSpec((1, 1, tn), lambda l, j: (l, 0, j)),
        ],
        out_specs=pl.BlockSpec((1, rows, tn), lambda l, j: (l, 0, j)),
        out_shape=jax.ShapeDtypeStruct((depth, rows, n_out), F32),
        compiler_params=pltpu.CompilerParams(
            dimension_semantics=("arbitrary", "arbitrary"), vmem_limit_bytes=VMEM_LIMIT),
        name="adaln_mods",
    )(cond, w_mod, b_mod.reshape(depth, 1, n_out))


def _group_sumsq(z, ones_ref, width):
    z2 = z * z
    hi = z2.astype(BF16)
    lo = (z2 - hi.astype(F32)).astype(BF16)
    step = min(width, MXU_DIM)
    ones = ones_ref[0:step, 0:step]
    cols = [jnp.dot(hi[:, c:c + step], ones, preferred_element_type=F32)
            + jnp.dot(lo[:, c:c + step], ones, preferred_element_type=F32)
            for c in range(0, width, step)]
    return cols[0] if len(cols) == 1 else jnp.concatenate(cols, axis=1)


def _head_rms(z, gain, ones_ref):
    ss = _group_sumsq(z, ones_ref, z.shape[-1])
    return z * lax.rsqrt(ss * (1.0 / HEAD_DIM) + EPS) * gain


def _rope(z, cos, sin_signed):
    width = z.shape[-1]
    lane = lax.broadcasted_iota(jnp.int32, z.shape, 1)
    nxt = pltpu.roll(z, width - 1, axis=1)
    prv = pltpu.roll(z, 1, axis=1)
    partner = jnp.where((lane & 1) == 0, nxt, prv)
    return z * cos + partner * sin_signed


def _inproj_kernel(*refs, rope):
    if rope:
        (x_ref, mod_ref, g1_ref, w_ref, qg_ref, kg_ref, ones_ref, cos_ref, sin_ref,
         q_ref, k_ref, v_ref, u_ref) = refs
    else:
        (x_ref, mod_ref, g1_ref, w_ref, qg_ref, kg_ref, ones_ref,
         q_ref, k_ref, v_ref, u_ref) = refs
    mod = mod_ref[0]
    h = _rms(x_ref[...], g1_ref[...]) * (1.0 + mod[1:2]) + mod[0:1]
    proj = jnp.dot(h.astype(BF16), w_ref[...], preferred_element_type=F32)
    q = _head_rms(proj[:, :D_ATTN], qg_ref[...], ones_ref)
    k = _head_rms(proj[:, D_ATTN:D_ATTN + D_KV], kg_ref[...], ones_ref)
    if rope:
        q = _rope(q, cos_ref[...], sin_ref[...])
        k = _rope(k, cos_ref[:, 0:D_KV], sin_ref[:, 0:D_KV])
    q_ref[...] = (q * (HEAD_DIM ** -0.5)).astype(BF16)
    k_ref[...] = k
    v_ref[...] = proj[:, D_ATTN + D_KV:D_ATTN + 2 * D_KV]
    u_ref[...] = proj[:, D_ATTN + 2 * D_KV:]


def _inproj_call(x2d, mods, g1, w_in_b, qg, kg, ones_bd, rope_tabs, layer, mod_row, tokens_per_batch,
                 tile):
    n_tok, d_model = x2d.shape
    d_in = w_in_b.shape[-1]
    d_ssm = d_in - D_ATTN - 2 * D_KV
    tiles_per_batch = None if tokens_per_batch is None else tokens_per_batch // tile
    in_specs = [
        pl.BlockSpec((tile, d_model), lambda i: (i, 0)),
        _mod_spec(d_model, layer, mod_row, tiles_per_batch),
        _layer_spec((1, d_model), layer),
        _layer_spec((d_model, d_in), layer),
        _layer_spec((1, D_ATTN), layer),
        _layer_spec((1, D_KV), layer),
        _const_spec((D_ATTN, D_ATTN)),
    ]
    args = [x2d, mods, g1, w_in_b, qg, kg, ones_bd]
    if rope_tabs is not None:
        in_specs += [pl.BlockSpec((tile, D_ATTN), lambda i: (i % tiles_per_batch, 0))] * 2
        args += list(rope_tabs)
    row = lambda width: pl.BlockSpec((tile, width), lambda i: (i, 0))
    return pl.pallas_call(
        functools.partial(_inproj_kernel, rope=rope_tabs is not None),
        grid=(n_tok // tile,),
        in_specs=in_specs,
        out_specs=(row(D_ATTN), row(D_KV), row(D_KV), row(d_ssm)),
        out_shape=(jax.ShapeDtypeStruct((n_tok, D_ATTN), BF16),
                   jax.ShapeDtypeStruct((n_tok, D_KV), F32),
                   jax.ShapeDtypeStruct((n_tok, D_KV), F32),
                   jax.ShapeDtypeStruct((n_tok, d_ssm), F32)),
        compiler_params=pltpu.CompilerParams(
            dimension_semantics=("arbitrary",), vmem_limit_bytes=VMEM_LIMIT),
        name="inproj_rope" if rope_tabs is not None else "inproj",
    )(*args)


def _attn_kernel(*refs, has_cache, nb):
    if has_cache:
        q_ref, kn_ref, vn_ref, ck_ref, cv_ref, o_ref, kvar, vvar = refs
    else:
        q_ref, kn_ref, vn_ref, o_ref, kvar, vvar = refs

    @pl.when(pl.program_id(1) == 0)
    def _():
        for bi in range(nb):
            for new_ref, cache_ref, dst in ((kn_ref, ck_ref if has_cache else None, kvar),
                                            (vn_ref, cv_ref if has_cache else None, vvar)):
                src = new_ref[bi]
                if has_cache:
                    src = jnp.concatenate([cache_ref[bi], src], axis=0)
                low = lax.broadcasted_iota(jnp.int32, src.shape, 1) < HEAD_DIM
                head0 = jnp.where(low, src, 0.0)
                head1 = jnp.where(low, 0.0, src)
                dst[4 * bi + 0] = head0.astype(BF16)
                dst[4 * bi + 1] = pltpu.roll(head0, HEAD_DIM, axis=1).astype(BF16)
                dst[4 * bi + 2] = pltpu.roll(head1, HEAD_DIM, axis=1).astype(BF16)
                dst[4 * bi + 3] = head1.astype(BF16)

    slabs = D_ATTN // LANES
    for bi in range(nb):
        for slab in range(slabs):
            qs = q_ref[bi, :, slab * LANES:(slab + 1) * LANES]
            kv = slab // (slabs // N_KV_HEADS)
            acc = None
            for half in range(2):
                idx = 4 * bi + kv * 2 + half
                s = lax.dot_general(qs, kvar[idx], _NT, preferred_element_type=F32)
                m = jnp.max(s, axis=-1, keepdims=True)
                p = jnp.exp(s - m)
                denom = jnp.sum(p, axis=-1, keepdims=True)
                o = jnp.dot(p.astype(BF16), vvar[idx], preferred_element_type=F32) / denom
                acc = o if acc is None else acc + o
            o_ref[bi, :, slab * LANES:(slab + 1) * LANES] = acc.astype(BF16)


def _attn_call(q3, k3, v3, ck3, cv3, tq, nb):
    b, l, _ = q3.shape
    has_cache = ck3 is not None
    s_len = l + (ck3.shape[1] if has_cache else 0)
    in_specs = [
        pl.BlockSpec((nb, tq, D_ATTN), lambda bi, qi: (bi, qi, 0)),
        pl.BlockSpec((nb, l, D_KV), lambda bi, qi: (bi, 0, 0)),
        pl.BlockSpec((nb, l, D_KV), lambda bi, qi: (bi, 0, 0)),
    ]
    args = [q3, k3, v3]
    if has_cache:
        in_specs += [pl.BlockSpec((nb, ck3.shape[1], D_KV), lambda bi, qi: (bi, 0, 0))] * 2
        args += [ck3, cv3]
    return pl.pallas_call(
        functools.partial(_attn_kernel, has_cache=has_cache, nb=nb),
        grid=(b // nb, l // tq),
        in_specs=in_specs,
        out_specs=pl.BlockSpec((nb, tq, D_ATTN), lambda bi, qi: (bi, qi, 0)),
        out_shape=jax.ShapeDtypeStruct((b, l, D_ATTN), BF16),
        scratch_shapes=[pltpu.VMEM((4 * nb, s_len, D_KV), BF16), pltpu.VMEM((4 * nb, s_len, D_KV), BF16)],
        compiler_params=pltpu.CompilerParams(
            dimension_semantics=("arbitrary", "arbitrary"), vmem_limit_bytes=VMEM_LIMIT),
        name="attn_cache" if has_cache else "attn",
    )(*args)


def _dot3(a, b):
    a_hi = a.astype(BF16)
    b_hi = b.astype(BF16)
    a_lo = (a - a_hi.astype(F32)).astype(BF16)
    b_lo = (b - b_hi.astype(F32)).astype(BF16)
    dot = lambda x, y: lax.dot_general(x, y, _NT, preferred_element_type=F32)
    return dot(a_hi, b_hi) + (dot(a_hi, b_lo) + dot(a_lo, b_hi))


def _ssm_prep_kernel(are_ref, aim_ref, ldt_ref, bre_ref, bim_ref, cre_ref, cim_ref,
                     tab_ref, wx_ref, wxs_ref, vt_ref, tt_ref, clm, lbm):
    n_groups = wx_ref.shape[0]
    width = SSM_BLOCK * SSM_GROUP
    shape = (SSM_GROUP, LANES)
    low = lax.broadcasted_iota(jnp.int32, shape, 1) < SSM_STATE
    row_blk = lax.broadcasted_iota(jnp.int32, (width, width), 0) // SSM_GROUP
    col_blk = lax.broadcasted_iota(jnp.int32, (width, width), 1) // SSM_GROUP

    def cmul(xr, xi, yr, yi):
        return xr * yr - xi * yi, xr * yi + xi * yr

    def group(g, carry):
        rows = pl.ds(pl.multiple_of(g * SSM_GROUP, SSM_GROUP), SSM_GROUP)
        tile = None
        for d in range(2):
            lanes = slice(d * LANES, (d + 1) * LANES)
            dt = jnp.exp(ldt_ref[d, rows, :])
            ar = are_ref[d, rows, :]
            ai = aim_ref[d, rows, :]
            mag = jnp.exp(ar * dt)
            lr = mag * jnp.cos(ai * dt)
            li = mag * jnp.sin(ai * dt)
            den = ar * ar + ai * ai
            zr = ((lr - 1.0) * ar + li * ai) / den
            zi = (li * ar - (lr - 1.0) * ai) / den
            bbr, bbi = cmul(zr, zi, bre_ref[d, rows, :], bim_ref[d, rows, :])
            cr = cre_ref[d, rows, :]
            ci = cim_ref[d, rows, :]
            inv = 1.0 / (lr * lr + li * li)
            ir = lr * inv
            ii = -li * inv
            pr, pi = jnp.ones(shape, F32), jnp.zeros(shape, F32)
            qr, qi = pr, pi
            for j in range(SSM_BLOCK + 1):
                c_re, c_im = cmul(cr, ci, pr, pi)
                c_pos = jnp.where(low, c_re, -c_im)
                if j >= 1:
                    vt_ref[g, (j - 1) if d == 0 else (SSM_BLOCK - j), :, lanes] = c_pos.astype(BF16)
                if j < SSM_BLOCK:
                    g_re, g_im = cmul(pr, pi, bbr, bbi)
                    g_pos = jnp.where(low, g_re, g_im)
                    r = (SSM_BLOCK - 1 - j) if d == 0 else j
                    wx_ref[g, r, :, lanes] = g_pos.astype(BF16)
                    wxs_ref[g, r, :, lanes] = jnp.where(low, g_im, g_re).astype(BF16)
                    blk = pl.ds(j * SSM_GROUP, SSM_GROUP)
                    if d == 0:
                        n_re, n_im = cmul(qr, qi, bbr, bbi)
                        clm[blk, :] = c_pos
                        lbm[blk, :] = jnp.where(low, n_re, n_im)
                    else:
                        m_re, m_im = cmul(cr, ci, qr, qi)
                        clm[blk, :] = jnp.where(low, m_re, -m_im)
                        lbm[blk, :] = g_pos
                else:
                    tab_ref[g, d, 0] = pr[0:SUBLANES]
                    tab_ref[g, d, 1] = jnp.where(low, -pi, pi)[0:SUBLANES]
                pr, pi = cmul(pr, pi, lr, li)
                qr, qi = cmul(qr, qi, ir, ii)
            t = _dot3(clm[...], lbm[...])
            t = jnp.where((col_blk <= row_blk) if d == 0 else (col_blk >= row_blk), t, 0.0)
            tile = t if tile is None else tile + t
        tt_ref[g] = tile.astype(BF16)
        return carry

    lax.fori_loop(0, n_groups, group, 0)


def _ssm_prep_call(a_re, a_im, log_dt, b_re, b_im, c_re, c_im):
    depth, ndir, g, p, h = b_re.shape
    halves = 2
    g_half = g // halves
    rows = g_half * h
    width = SSM_BLOCK * h

    def rows_lanes(a):
        a = a.astype(F32).reshape(depth, ndir, g * h, p)
        return jnp.concatenate([a, a], axis=-1)

    per_state = lambda a: rows_lanes(jnp.broadcast_to(a[:, :, :, None, :], (depth, ndir, g, h, p)))
    ldt = rows_lanes(jnp.broadcast_to(log_dt[:, :, :, None, None], (depth, ndir, g, h, p)))
    to_hp = lambda a: rows_lanes(jnp.swapaxes(a, -1, -2))
    in_spec = pl.BlockSpec((None, ndir, rows, LANES), lambda l, s: (l, 0, s, 0))
    out_blk = lambda *shape: pl.BlockSpec((None, g_half) + shape, lambda l, s: (l, s) + (0,) * len(shape))
    op4 = jax.ShapeDtypeStruct((depth, g, SSM_BLOCK, h, 2 * LANES), BF16)
    tab, wx, wxs, vt, tt = pl.pallas_call(
        _ssm_prep_kernel,
        grid=(depth, halves),
        in_specs=[in_spec] * 7,
        out_specs=(out_blk(2, 2, SUBLANES, LANES), out_blk(SSM_BLOCK, h, 2 * LANES),
                   out_blk(SSM_BLOCK, h, 2 * LANES), out_blk(SSM_BLOCK, h, 2 * LANES),
                   out_blk(width, width)),
        out_shape=(jax.ShapeDtypeStruct((depth, g, 2, 2, SUBLANES, LANES), F32), op4, op4, op4,
                   jax.ShapeDtypeStruct((depth, g, width, width), BF16)),
        scratch_shapes=[pltpu.VMEM((width, LANES), F32), pltpu.VMEM((width, LANES), F32)],
        compiler_params=pltpu.CompilerParams(
            dimension_semantics=("arbitrary", "arbitrary"), vmem_limit_bytes=VMEM_LIMIT),
        name="ssm_prepare",
    )(per_state(a_re), per_state(a_im), ldt, to_hp(b_re), to_hp(b_im), rows_lanes(c_re), rows_lanes(c_im))
    merge = lambda a: a.reshape(depth, g, width, 2 * LANES)
    return tab, merge(wx), merge(wxs), tt, merge(vt)


def _transpose_pieces(sets, piece):
    sets = [list(vs) for vs in sets]
    n = len(sets[0])
    s = n // 2
    while s:
        keep_low = (piece & s) == 0
        pairs = [(vs, i) for vs in sets for i in range(n) if not i & s]
        moved = [(pltpu.roll(vs[i + s], s * SSM_GROUP, axis=1),
                  pltpu.roll(vs[i], LANES - s * SSM_GROUP, axis=1)) for vs, i in pairs]
        for (vs, i), (from_hi, from_lo) in zip(pairs, moved):
            vs[i], vs[i + s] = jnp.where(keep_low, vs[i], from_hi), jnp.where(keep_low, from_lo, vs[i + s])
        s //= 2
    return sets


def _ssm_kernel(u_ref, h0_ref, tab_ref, wx_ref, wxs_ref, tt_ref, vt_ref, y_ref, ht_ref, ug, xb, xsb, yg,
                *, batch, seq, pitch):
    m_blk = seq // SSM_BLOCK
    chunk = 2 * SUBLANES
    chunks = m_blk // chunk
    per_step = 4
    piece = lax.broadcasted_iota(jnp.int32, (chunk, LANES), 1) // SSM_GROUP

    @pl.when(pl.program_id(0) == 0)
    def _():
        ug[...] = jnp.zeros(ug.shape, F32)

    def chunk_rows(idx):
        b, ch = idx // chunks, idx % chunks
        tok = b * seq + ch * (chunk * SSM_BLOCK)
        return tok, pl.ds(pl.multiple_of(b * pitch + ch * chunk, SUBLANES), chunk)

    def gather(i, carry):
        sets, dests = [], []
        for k in range(per_step):
            tok, rows = chunk_rows(i * per_step + k)
            for half in range(2):
                sets.append([u_ref[pl.ds(tok + half * SLAB_GROUPS + rr, chunk, stride=SSM_BLOCK), :]
                             for rr in range(SLAB_GROUPS)])
                dests.append((half, rows))
        for (half, rows), vs in zip(dests, _transpose_pieces(sets, piece)):
            for gl, v in enumerate(vs):
                ug[gl, half, rows, :] = v
        return carry

    lax.fori_loop(0, batch * chunks // per_step, gather, 0)

    for gl in range(SLAB_GROUPS):
        lhs = jnp.concatenate([ug[gl, 0], ug[gl, 1]], axis=1).astype(BF16)
        x = jnp.dot(lhs, wx_ref[gl], preferred_element_type=F32)
        xb[gl, 0] = x[:, :LANES]
        xb[gl, 1] = x[:, LANES:]
        x = jnp.dot(lhs, wxs_ref[gl], preferred_element_type=F32)
        xsb[gl, 0] = x[:, :LANES]
        xsb[gl, 1] = x[:, LANES:]
        y_in = lax.dot_general(lhs, tt_ref[gl], _NT, preferred_element_type=F32)
        yg[gl, 0] = y_in[:, :LANES]
        yg[gl, 1] = y_in[:, LANES:]

    n_bt = -(-batch // SUBLANES)
    rows_per = min(batch, SUBLANES)
    per_pass = max(1, 4 // n_bt)
    for g0 in range(0, SLAB_GROUPS, per_pass):
        keys = [(gl, d, bt) for gl in range(g0, g0 + per_pass) for d in range(2) for bt in range(n_bt)]
        lane0 = lambda gl, d: (gl * 2 + d) * LANES
        init = []
        for gl, d, bt in keys:
            h = h0_ref[bt * SUBLANES:bt * SUBLANES + rows_per, lane0(gl, d):lane0(gl, d) + LANES]
            init += [h, pltpu.roll(h, SSM_STATE, axis=1)]
        mult = {(gl, d): (tab_ref[gl, d, 0, 0:rows_per, :], tab_ref[gl, d, 1, 0:rows_per, :])
                for gl, d, _ in keys}

        def step(m, hs, keys=keys, mult=mult):
            out = []
            for k, (gl, d, bt) in enumerate(keys):
                h, h_sw = hs[2 * k], hs[2 * k + 1]
                row = m if d == 0 else m_blk - 1 - m
                idx = pl.ds(bt * SUBLANES * pitch + row, rows_per, stride=pitch)
                x = xb[gl, d, idx, :]
                x_sw = xsb[gl, d, idx, :]
                ug[gl, d, idx, :] = h
                a, b = mult[(gl, d)]
                out += [a * h + b * h_sw + x, a * h_sw - b * h + x_sw]
            return tuple(out)

        final = lax.fori_loop(0, m_blk, step, tuple(init), unroll=2)
        for k, (gl, d, bt) in enumerate(keys):
            ht_ref[bt * SUBLANES:bt * SUBLANES + rows_per, lane0(gl, d):lane0(gl, d) + LANES] = final[2 * k]

    for gl in range(SLAB_GROUPS):
        states = jnp.concatenate([ug[gl, 0], ug[gl, 1]], axis=1).astype(BF16)
        y_st = lax.dot_general(states, vt_ref[gl], _NT, preferred_element_type=F32)
        yg[gl, 0] = yg[gl, 0] + y_st[:, :LANES]
        yg[gl, 1] = yg[gl, 1] + y_st[:, LANES:]

    def scatter(i, carry):
        sets, dests = [], []
        for k in range(per_step):
            tok, rows = chunk_rows(i * per_step + k)
            for half in range(2):
                sets.append([yg[gl, half, rows, :] for gl in range(SLAB_GROUPS)])
                dests.append(tok + half * SLAB_GROUPS)
        for tok, vs in zip(dests, _transpose_pieces(sets, piece)):
            for rr, v in enumerate(vs):
                y_ref[pl.ds(tok + rr, chunk, stride=SSM_BLOCK), :] = v
        return carry

    lax.fori_loop(0, batch * chunks // per_step, scatter, 0)


def _ssm_call(u2d, h0, tab, wx, wxs, tt, vt, layer, batch, seq):
    n_tok, chans = u2d.shape
    n_slab = chans // LANES
    m_blk = seq // SSM_BLOCK
    pitch = m_blk + SUBLANES
    rows_p = batch * pitch
    st_lanes = SLAB_GROUPS * 2 * LANES
    op_spec = pl.BlockSpec((None, SLAB_GROUPS, MXU_DIM, MXU_DIM), lambda s: (layer, s, 0, 0))
    scratch = pltpu.VMEM((SLAB_GROUPS, 2, rows_p, LANES), F32)
    return pl.pallas_call(
        functools.partial(_ssm_kernel, batch=batch, seq=seq, pitch=pitch),
        grid=(n_slab,),
        in_specs=[
            pl.BlockSpec((n_tok, LANES), lambda s: (0, s)),
            pl.BlockSpec((batch, st_lanes), lambda s: (0, s)),
            pl.BlockSpec((None, SLAB_GROUPS, 2, 2, SUBLANES, LANES), lambda s: (layer, s, 0, 0, 0, 0)),
            op_spec, op_spec, op_spec, op_spec,
        ],
        out_specs=(pl.BlockSpec((n_tok, LANES), lambda s: (0, s)),
                   pl.BlockSpec((batch, st_lanes), lambda s: (0, s))),
        out_shape=(jax.ShapeDtypeStruct((n_tok, chans), F32),
                   jax.ShapeDtypeStruct((batch, n_slab * st_lanes), F32)),
        scratch_shapes=[scratch, scratch, scratch, scratch],
        compiler_params=pltpu.CompilerParams(
            dimension_semantics=("arbitrary",), vmem_limit_bytes=VMEM_LIMIT),
        name=f"ssm_scan_b{batch}",
    )(u2d, h0, tab, wx, wxs, tt, vt)


def _mix_ffn_kernel(attn_ref, y_ref, u_ref, x_ref, mod_ref, dskip_ref, wglu_ref, bglu_ref,
                    wout_ref, g2_ref, wffi_ref, wffo_ref, fn_ref, o_ref, *, final, ff_chunks):
    d_ff = wffo_ref.shape[0]
    y = y_ref[...] + dskip_ref[...] * u_ref[...]
    g = 0.5 * y * (1.0 + lax.erf(y * (2.0 ** -0.5)))
    z = jnp.dot(g.astype(BF16), wglu_ref[...], preferred_element_type=F32) + bglu_ref[...]
    ssm_out = g * jax.nn.sigmoid(z)
    mixed = (jnp.dot(attn_ref[...], wout_ref[0:D_ATTN, :], preferred_element_type=F32)
             + jnp.dot(ssm_out.astype(BF16), wout_ref[D_ATTN:, :], preferred_element_type=F32))
    mod = mod_ref[0]
    x1 = x_ref[...] + mod[2:3] * mixed
    h2 = (_rms(x1, g2_ref[...]) * (1.0 + mod[4:5]) + mod[3:4]).astype(BF16)
    tiles = d_ff // MXU_DIM
    edges = [MXU_DIM * ((tiles * ci) // ff_chunks) for ci in range(ff_chunks)] + [d_ff]
    acc = None
    for c0, c1 in zip(edges[:-1], edges[1:]):
        gate = jnp.dot(h2, wffi_ref[:, c0:c1], preferred_element_type=F32)
        up = jnp.dot(h2, wffi_ref[:, d_ff + c0:d_ff + c1], preferred_element_type=F32)
        act = (gate * jax.nn.sigmoid(gate) * up).astype(BF16)
        part = jnp.dot(act, wffo_ref[c0:c1, :], preferred_element_type=F32)
        acc = part if acc is None else acc + part
    x2 = x1 + mod[5:6] * acc
    if final:
        x2 = _rms(x2, fn_ref[...])
    o_ref[...] = x2


def _mix_ffn_call(attn, y, u, x2d, mods, dskip, wglu_b, bglu, wout_b, g2, wffi_b, wffo_b, fnorm,
                  layer, mod_row, tokens_per_batch, tile, final):
    n_tok, d_model = x2d.shape
    d_ssm = u.shape[1]
    d_ff = wffo_b.shape[-2]
    tiles_per_batch = None if tokens_per_batch is None else tokens_per_batch // tile
    row = lambda width: pl.BlockSpec((tile, width), lambda i: (i, 0))
    return pl.pallas_call(
        functools.partial(_mix_ffn_kernel, final=final, ff_chunks=2),
        grid=(n_tok // tile,),
        in_specs=[
            row(D_ATTN), row(d_ssm), row(d_ssm), row(d_model),
            _mod_spec(d_model, layer, mod_row, tiles_per_batch),
            _layer_spec((1, d_ssm), layer),
            _layer_spec((d_ssm, d_ssm), layer),
            _layer_spec((1, d_ssm), layer),
            _layer_spec((D_ATTN + d_ssm, d_model), layer),
            _layer_spec((1, d_model), layer),
            _layer_spec((d_model, 2 * d_ff), layer),
            _layer_spec((d_ff, d_model), layer),
            _const_spec((1, d_model)),
        ],
        out_specs=row(d_model),
        out_shape=jax.ShapeDtypeStruct((n_tok, d_model), F32),
        compiler_params=pltpu.CompilerParams(
            dimension_semantics=("arbitrary",), vmem_limit_bytes=VMEM_LIMIT),
        name="mix_ffn_final" if final else "mix_ffn",
    )(attn, y, u, x2d, mods, dskip, wglu_b, bglu, wout_b, g2, wffi_b, wffo_b, fnorm)


def _rope_tables(n_tokens):
    axis_dim = HEAD_DIM // 2
    rows = n_tokens // GRID_W
    row = jnp.repeat(jnp.arange(rows, dtype=F32), GRID_W)
    col = jnp.tile(jnp.arange(GRID_W, dtype=F32), rows)
    inv_freq = ROPE_THETA ** (-jnp.arange(0, axis_dim, 2, dtype=F32) / axis_dim)
    ang = jnp.concatenate([row[:, None] * inv_freq, col[:, None] * inv_freq], axis=-1)
    cos = jnp.repeat(jnp.cos(ang), 2, axis=-1)
    sin = jnp.repeat(jnp.sin(ang), 2, axis=-1) * jnp.tile(jnp.array([-1.0, 1.0], F32), HEAD_DIM // 2)
    return jnp.tile(cos, (1, N_Q_HEADS)), jnp.tile(sin, (1, N_Q_HEADS))


def _states_to_lanes(st):
    return st.transpose(0, 3, 1, 2, 4).reshape(st.shape[0], -1)


def _lanes_to_states(rows, n_groups):
    return rows.reshape(rows.shape[0], n_groups, 2, 2, SSM_STATE).transpose(0, 2, 3, 1, 4)


def kernel(x_prompt, x_sample, cache_k, cache_v, state_ssm, c, c_ctx, w_mod, b_mod, norm1, norm2, w_in, q_norm, k_norm, ssm_a_re, ssm_a_im, ssm_log_dt, ssm_b_re, ssm_b_im, ssm_c_re, ssm_c_im, ssm_d, w_glu, b_glu, w_out, w_ffn_in, w_ffn_out, final_norm):
    batch, seq, d_model = x_prompt.shape
    dec_batch, dec_seq, _ = x_sample.shape
    depth = w_in.shape[0]
    past = cache_k.shape[2]
    n_groups = ssm_a_re.shape[2]
    d_ssm = n_groups * SSM_GROUP

    cond = jnp.zeros((SUBLANES, d_model), F32).at[0].set(c_ctx).at[1:1 + dec_batch].set(c)
    mods = _mods_call(cond, w_mod, b_mod).reshape(depth, SUBLANES, N_MOD, d_model)
    tab, wx, wxs, tt, vt = _ssm_prep_call(ssm_a_re, ssm_a_im, ssm_log_dt, ssm_b_re, ssm_b_im,
                                          ssm_c_re, ssm_c_im)
    rope_tabs = _rope_tables(dec_seq)
    head_ids = jnp.arange(D_ATTN) // HEAD_DIM
    ones_bd = (head_ids[:, None] == head_ids[None, :]).astype(BF16)

    xp = x_prompt.reshape(batch * seq, d_model)
    xs = x_sample.reshape(dec_batch * dec_seq, d_model)
    zero_state = jnp.zeros((batch, n_groups * 2 * LANES), F32)
    w_in_b = w_in.astype(BF16)
    wglu_b = w_glu.astype(BF16)
    wout_b = w_out.astype(BF16)
    wffi_b = w_ffn_in.astype(BF16)
    wffo_b = w_ffn_out.astype(BF16)
    g1 = norm1.reshape(depth, 1, d_model)
    g2 = norm2.reshape(depth, 1, d_model)
    qg = jnp.tile(q_norm, (1, N_Q_HEADS)).reshape(depth, 1, D_ATTN)
    kg = jnp.tile(k_norm, (1, N_KV_HEADS)).reshape(depth, 1, D_KV)
    dskip = ssm_d.reshape(depth, 1, d_ssm)
    bglu = b_glu.reshape(depth, 1, d_ssm)
    fnorm = final_norm.reshape(1, d_model)
    new_k, new_v, new_s = [], [], []
    for l in range(depth):
        final = l == depth - 1
        for is_ctx in (True, False):
            if is_ctx:
                x2d, n_b, n_l, tq, attn_nb = xp, batch, seq, seq, 4
                mod_row, tokens_per_batch, tabs, ck, cv, h0 = 0, None, None, None, None, zero_state
            else:
                x2d, n_b, n_l, tq, attn_nb = xs, dec_batch, dec_seq, 512, 1
                mod_row, tokens_per_batch, tabs = 1, dec_seq, rope_tabs
                ck = cache_k[:, l].reshape(dec_batch, past, D_KV)
                cv = cache_v[:, l].reshape(dec_batch, past, D_KV)
                h0 = _states_to_lanes(state_ssm[:, l])
            q, k, v, u = _inproj_call(x2d, mods, g1, w_in_b, qg, kg, ones_bd, tabs, l, mod_row,
                                      tokens_per_batch, TOKEN_TILE)
            attn = _attn_call(q.reshape(n_b, n_l, D_ATTN), k.reshape(n_b, n_l, D_KV),
                              v.reshape(n_b, n_l, D_KV), ck, cv, tq, attn_nb)
            y, ht = _ssm_call(u, h0, tab, wx, wxs, tt, vt, l, n_b, n_l)
            x_new = _mix_ffn_call(attn.reshape(-1, D_ATTN), y, u, x2d, mods, dskip, wglu_b, bglu, wout_b,
                                  g2, wffi_b, wffo_b, fnorm, l, mod_row, tokens_per_batch, TOKEN_TILE,
                                  final)
            if is_ctx:
                xp = x_new
                new_k.append(k.reshape(batch, seq, N_KV_HEADS, HEAD_DIM))
                new_v.append(v.reshape(batch, seq, N_KV_HEADS, HEAD_DIM))
                new_s.append(_lanes_to_states(ht, n_groups))
            else:
                xs = x_new
    return (xp.reshape(batch, seq, d_model), xs.reshape(dec_batch, dec_seq, d_model),
            jnp.stack(new_k, axis=1), jnp.stack(new_v, axis=1), jnp.stack(new_s, axis=1))
```

```python
import functools

import jax
import jax.numpy as jnp
from jax import lax
from jax.experimental import pallas as pl
from jax.experimental.pallas import tpu as pltpu

F32 = jnp.float32
BF16 = jnp.bfloat16

HEAD_DIM = 64
N_Q_HEADS = 8
N_KV_HEADS = 2
D_ATTN = N_Q_HEADS * HEAD_DIM
D_KV = N_KV_HEADS * HEAD_DIM
SSM_GROUP = 16
SSM_STATE = 64
GRID_W = 64
ROPE_THETA = 10000.0
N_MOD = 6
EPS = 1e-6

LANES = 128
SUBLANES = 8
MXU_DIM = 256
SSM_BLOCK = MXU_DIM // SSM_GROUP
SLAB_GROUPS = LANES // SSM_GROUP
VMEM_LIMIT = 56 * 1024 * 1024
TOKEN_TILE = 512

_NT = (((1,), (1,)), ((), ()))


def _const_spec(shape):
    nd = len(shape)
    return pl.BlockSpec(shape, lambda *_: (0,) * nd, pipeline_mode=pl.Buffered(1))


def _layer_spec(shape, layer):
    nd = len(shape)
    return pl.BlockSpec((None,) + shape, lambda *_: (layer,) + (0,) * nd,
                        pipeline_mode=pl.Buffered(1))


def _mod_spec(d_model, layer, first_row, tiles_per_batch):
    if tiles_per_batch is None:
        return pl.BlockSpec((None, 1, N_MOD, d_model), lambda i: (layer, first_row, 0, 0))
    return pl.BlockSpec((None, 1, N_MOD, d_model),
                        lambda i: (layer, first_row + i // tiles_per_batch, 0, 0))


def _rms(x, gain):
    ms = jnp.mean(x * x, axis=-1, keepdims=True)
    return x * lax.rsqrt(ms + EPS) * gain


def _mods_kernel(cond_ref, w_ref, b_ref, o_ref):
    c = cond_ref[...]
    act = c * jax.nn.sigmoid(c)
    o_ref[0] = jnp.dot(act.astype(BF16), w_ref[0].astype(BF16), preferred_element_type=F32) + b_ref[0]


def _mods_call(cond, w_mod, b_mod):
    depth, d_model, n_out = w_mod.shape
    rows = cond.shape[0]
    nt = 4
    tn = n_out // nt
    return pl.pallas_call(
        _mods_kernel,
        grid=(depth, nt),
        in_specs=[
            pl.BlockSpec((rows, d_model), lambda l, j: (0, 0)),
            pl.BlockSpec((1, d_model, tn), lambda l, j: (l, 0, j)),
            pl.BlockSpec((1, 1, tn), lambda l, j: (l, 0, j)),
        ],
        out_specs=pl.BlockSpec((1, rows, tn), lambda l, j: (l, 0, j)),
        out_shape=jax.ShapeDtypeStruct((depth, rows, n_out), F32),
        compiler_params=pltpu.CompilerParams(
            dimension_semantics=("arbitrary", "arbitrary"), vmem_limit_bytes=VMEM_LIMIT),
        name="adaln_mods",
    )(cond, w_mod, b_mod.reshape(depth, 1, n_out))


def _group_sumsq(z, ones_ref, width):
    z2 = z * z
    hi = z2.astype(BF16)
    lo = (z2 - hi.astype(F32)).astype(BF16)
    step = min(width, MXU_DIM)
    ones = ones_ref[0:step, 0:step]
    cols = [jnp.dot(hi[:, c:c + step], ones, preferred_element_type=F32)
            + jnp.dot(lo[:, c:c + step], ones, preferred_element_type=F32)
            for c in range(0, width, step)]
    return cols[0] if len(cols) == 1 else jnp.concatenate(cols, axis=1)


def _head_rms(z, gain, ones_ref):
    ss = _group_sumsq(z, ones_ref, z.shape[-1])
    return z * lax.rsqrt(ss * (1.0 / HEAD_DIM) + EPS) * gain


def _rope(z, cos, sin_signed):
    width = z.shape[-1]
    lane = lax.broadcasted_iota(jnp.int32, z.shape, 1)
    nxt = pltpu.roll(z, width - 1, axis=1)
    prv = pltpu.roll(z, 1, axis=1)
    partner = jnp.where((lane & 1) == 0, nxt, prv)
    return z * cos + partner * sin_signed


def _inproj_kernel(*refs, rope):
    if rope:
        (x_ref, mod_ref, g1_ref, w_ref, qg_ref, kg_ref, ones_ref, cos_ref, sin_ref,
         q_ref, k_ref, v_ref, u_ref) = refs
    else:
        (x_ref, mod_ref, g1_ref, w_ref, qg_ref, kg_ref, ones_ref,
         q_ref, k_ref, v_ref, u_ref) = refs
    mod = mod_ref[0]
    h = _rms(x_ref[...], g1_ref[...]) * (1.0 + mod[1:2]) + mod[0:1]
    proj = jnp.dot(h.astype(BF16), w_ref[...], preferred_element_type=F32)
    q = _head_rms(proj[:, :D_ATTN], qg_ref[...], ones_ref)
    k = _head_rms(proj[:, D_ATTN:D_ATTN + D_KV], kg_ref[...], ones_ref)
    if rope:
        q = _rope(q, cos_ref[...], sin_ref[...])
        k = _rope(k, cos_ref[:, 0:D_KV], sin_ref[:, 0:D_KV])
    q_ref[...] = (q * (HEAD_DIM ** -0.5)).astype(BF16)
    k_ref[...] = k
    v_ref[...] = proj[:, D_ATTN + D_KV:D_ATTN + 2 * D_KV]
    u_ref[...] = proj[:, D_ATTN + 2 * D_KV:]


def _inproj_call(x2d, mods, g1, w_in_b, qg, kg, ones_bd, rope_tabs, layer, mod_row, tokens_per_batch,
                 tile):
    n_tok, d_model = x2d.shape
    d_in = w_in_b.shape[-1]
    d_ssm = d_in - D_ATTN - 2 * D_KV
    tiles_per_batch = None if tokens_per_batch is None else tokens_per_batch // tile
    in_specs = [
        pl.BlockSpec((tile, d_model), lambda i: (i, 0)),
        _mod_spec(d_model, layer, mod_row, tiles_per_batch),
        _layer_spec((1, d_model), layer),
        _layer_spec((d_model, d_in), layer),
        _layer_spec((1, D_ATTN), layer),
        _layer_spec((1, D_KV), layer),
        _const_spec((D_ATTN, D_ATTN)),
    ]
    args = [x2d, mods, g1, w_in_b, qg, kg, ones_bd]
    if rope_tabs is not None:
        in_specs += [pl.BlockSpec((tile, D_ATTN), lambda i: (i % tiles_per_batch, 0))] * 2
        args += list(rope_tabs)
    row = lambda width: pl.BlockSpec((tile, width), lambda i: (i, 0))
    return pl.pallas_call(
        functools.partial(_inproj_kernel, rope=rope_tabs is not None),
        grid=(n_tok // tile,),
        in_specs=in_specs,
        out_specs=(row(D_ATTN), row(D_KV), row(D_KV), row(d_ssm)),
        out_shape=(jax.ShapeDtypeStruct((n_tok, D_ATTN), BF16),
                   jax.ShapeDtypeStruct((n_tok, D_KV), F32),
                   jax.ShapeDtypeStruct((n_tok, D_KV), F32),
                   jax.ShapeDtypeStruct((n_tok, d_ssm), F32)),
        compiler_params=pltpu.CompilerParams(
            dimension_semantics=("arbitrary",), vmem_limit_bytes=VMEM_LIMIT),
        name="inproj_rope" if rope_tabs is not None else "inproj",
    )(*args)


def _attn_kernel(*refs, has_cache, nb):
    if has_cache:
        q_ref, kn_ref, vn_ref, ck_ref, cv_ref, o_ref, kvar, vvar = refs
    else:
        q_ref, kn_ref, vn_ref, o_ref, kvar, vvar = refs

    @pl.when(pl.program_id(1) == 0)
    def _():
        for bi in range(nb):
            for new_ref, cache_ref, dst in ((kn_ref, ck_ref if has_cache else None, kvar),
                                            (vn_ref, cv_ref if has_cache else None, vvar)):
                src = new_ref[bi]
                if has_cache:
                    src = jnp.concatenate([cache_ref[bi], src], axis=0)
                low = lax.broadcasted_iota(jnp.int32, src.shape, 1) < HEAD_DIM
                head0 = jnp.where(low, src, 0.0)
                head1 = jnp.where(low, 0.0, src)
                dst[4 * bi + 0] = head0.astype(BF16)
                dst[4 * bi + 1] = pltpu.roll(head0, HEAD_DIM, axis=1).astype(BF16)
                dst[4 * bi + 2] = pltpu.roll(head1, HEAD_DIM, axis=1).astype(BF16)
                dst[4 * bi + 3] = head1.astype(BF16)

    slabs = D_ATTN // LANES
    for bi in range(nb):
        for slab in range(slabs):
            qs = q_ref[bi, :, slab * LANES:(slab + 1) * LANES]
            kv = slab // (slabs // N_KV_HEADS)
            acc = None
            for half in range(2):
                idx = 4 * bi + kv * 2 + half
                s = lax.dot_general(qs, kvar[idx], _NT, preferred_element_type=F32)
                m = jnp.max(s, axis=-1, keepdims=True)
                p = jnp.exp(s - m)
                denom = jnp.sum(p, axis=-1, keepdims=True)
                o = jnp.dot(p.astype(BF16), vvar[idx], preferred_element_type=F32) / denom
                acc = o if acc is None else acc + o
            o_ref[bi, :, slab * LANES:(slab + 1) * LANES] = acc.astype(BF16)


def _attn_call(q3, k3, v3, ck3, cv3, tq, nb):
    b, l, _ = q3.shape
    has_cache = ck3 is not None
    s_len = l + (ck3.shape[1] if has_cache else 0)
    in_specs = [
        pl.BlockSpec((nb, tq, D_ATTN), lambda bi, qi: (bi, qi, 0)),
        pl.BlockSpec((nb, l, D_KV), lambda bi, qi: (bi, 0, 0)),
        pl.BlockSpec((nb, l, D_KV), lambda bi, qi: (bi, 0, 0)),
    ]
    args = [q3, k3, v3]
    if has_cache:
        in_specs += [pl.BlockSpec((nb, ck3.shape[1], D_KV), lambda bi, qi: (bi, 0, 0))] * 2
        args += [ck3, cv3]
    return pl.pallas_call(
        functools.partial(_attn_kernel, has_cache=has_cache, nb=nb),
        grid=(b // nb, l // tq),
        in_specs=in_specs,
        out_specs=pl.BlockSpec((nb, tq, D_ATTN), lambda bi, qi: (bi, qi, 0)),
        out_shape=jax.ShapeDtypeStruct((b, l, D_ATTN), BF16),
        scratch_shapes=[pltpu.VMEM((4 * nb, s_len, D_KV), BF16), pltpu.VMEM((4 * nb, s_len, D_KV), BF16)],
        compiler_params=pltpu.CompilerParams(
            dimension_semantics=("arbitrary", "arbitrary"), vmem_limit_bytes=VMEM_LIMIT),
        name="attn_cache" if has_cache else "attn",
    )(*args)


def _dot3(a, b):
    a_hi = a.astype(BF16)
    b_hi = b.astype(BF16)
    a_lo = (a - a_hi.astype(F32)).astype(BF16)
    b_lo = (b - b_hi.astype(F32)).astype(BF16)
    dot = lambda x, y: lax.dot_general(x, y, _NT, preferred_element_type=F32)
    return dot(a_hi, b_hi) + (dot(a_hi, b_lo) + dot(a_lo, b_hi))


def _ssm_prep_kernel(are_ref, aim_ref, ldt_ref, bre_ref, bim_ref, cre_ref, cim_ref,
                     tab_ref, wx_ref, wxs_ref, vt_ref, tt_ref, clm, lbm):
    n_groups = wx_ref.shape[0]
    width = SSM_BLOCK * SSM_GROUP
    shape = (SSM_GROUP, LANES)
    low = lax.broadcasted_iota(jnp.int32, shape, 1) < SSM_STATE
    row_blk = lax.broadcasted_iota(jnp.int32, (width, width), 0) // SSM_GROUP
    col_blk = lax.broadcasted_iota(jnp.int32, (width, width), 1) // SSM_GROUP

    def cmul(xr, xi, yr, yi):
        return xr * yr - xi * yi, xr * yi + xi * yr

    def group(g, carry):
        rows = pl.ds(pl.multiple_of(g * SSM_GROUP, SSM_GROUP), SSM_GROUP)
        tile = None
        for d in range(2):
            lanes = slice(d * LANES, (d + 1) * LANES)
            dt = jnp.exp(ldt_ref[d, rows, :])
            ar = are_ref[d, rows, :]
            ai = aim_ref[d, rows, :]
            mag = jnp.exp(ar * dt)
            lr = mag * jnp.cos(ai * dt)
            li = mag * jnp.sin(ai * dt)
            den = ar * ar + ai * ai
            zr = ((lr - 1.0) * ar + li * ai) / den
            zi = (li * ar - (lr - 1.0) * ai) / den
            bbr, bbi = cmul(zr, zi, bre_ref[d, rows, :], bim_ref[d, rows, :])
            cr = cre_ref[d, rows, :]
            ci = cim_ref[d, rows, :]
            inv = 1.0 / (lr * lr + li * li)
            ir = lr * inv
            ii = -li * inv
            pr, pi = jnp.ones(shape, F32), jnp.zeros(shape, F32)
            qr, qi = pr, pi
            for j in range(SSM_BLOCK + 1):
                c_re, c_im = cmul(cr, ci, pr, pi)
                c_pos = jnp.where(low, c_re, -c_im)
                if j >= 1:
                    vt_ref[g, (j - 1) if d == 0 else (SSM_BLOCK - j), :, lanes] = c_pos.astype(BF16)
                if j < SSM_BLOCK:
                    g_re, g_im = cmul(pr, pi, bbr, bbi)
                    g_pos = jnp.where(low, g_re, g_im)
                    r = (SSM_BLOCK - 1 - j) if d == 0 else j
                    wx_ref[g, r, :, lanes] = g_pos.astype(BF16)
                    wxs_ref[g, r, :, lanes] = jnp.where(low, g_im, g_re).astype(BF16)
                    blk = pl.ds(j * SSM_GROUP, SSM_GROUP)
                    if d == 0:
                        n_re, n_im = cmul(qr, qi, bbr, bbi)
                        clm[blk, :] = c_pos
                        lbm[blk, :] = jnp.where(low, n_re, n_im)
                    else:
                        m_re, m_im = cmul(cr, ci, qr, qi)
                        clm[blk, :] = jnp.where(low, m_re, -m_im)
                        lbm[blk, :] = g_pos
                else:
                    tab_ref[g, d, 0] = pr[0:SUBLANES]
                    tab_ref[g, d, 1] = jnp.where(low, -pi, pi)[0:SUBLANES]
                pr, pi = cmul(pr, pi, lr, li)
                qr, qi = cmul(qr, qi, ir, ii)
            t = _dot3(clm[...], lbm[...])
            t = jnp.where((col_blk <= row_blk) if d == 0 else (col_blk >= row_blk), t, 0.0)
            tile = t if tile is None else tile + t
        tt_ref[g] = tile.astype(BF16)
        return carry

    lax.fori_loop(0, n_groups, group, 0)


def _ssm_prep_call(a_re, a_im, log_dt, b_re, b_im, c_re, c_im):
    depth, ndir, g, p, h = b_re.shape
    halves = 2
    g_half = g // halves
    rows = g_half * h
    width = SSM_BLOCK * h

    def rows_lanes(a):
        a = a.astype(F32).reshape(depth, ndir, g * h, p)
        return jnp.concatenate([a, a], axis=-1)

    per_state = lambda a: rows_lanes(jnp.broadcast_to(a[:, :, :, None, :], (depth, ndir, g, h, p)))
    ldt = rows_lanes(jnp.broadcast_to(log_dt[:, :, :, None, None], (depth, ndir, g, h, p)))
    to_hp = lambda a: rows_lanes(jnp.swapaxes(a, -1, -2))
    in_spec = pl.BlockSpec((None, ndir, rows, LANES), lambda l, s: (l, 0, s, 0))
    out_blk = lambda *shape: pl.BlockSpec((None, g_half) + shape, lambda l, s: (l, s) + (0,) * len(shape))
    op4 = jax.ShapeDtypeStruct((depth, g, SSM_BLOCK, h, 2 * LANES), BF16)
    tab, wx, wxs, vt, tt = pl.pallas_call(
        _ssm_prep_kernel,
        grid=(depth, halves),
        in_specs=[in_spec] * 7,
        out_specs=(out_blk(2, 2, SUBLANES, LANES), out_blk(SSM_BLOCK, h, 2 * LANES),
                   out_blk(SSM_BLOCK, h, 2 * LANES), out_blk(SSM_BLOCK, h, 2 * LANES),
                   out_blk(width, width)),
        out_shape=(jax.ShapeDtypeStruct((depth, g, 2, 2, SUBLANES, LANES), F32), op4, op4, op4,
                   jax.ShapeDtypeStruct((depth, g, width, width), BF16)),
        scratch_shapes=[pltpu.VMEM((width, LANES), F32), pltpu.VMEM((width, LANES), F32)],
        compiler_params=pltpu.CompilerParams(
            dimension_semantics=("arbitrary", "arbitrary"), vmem_limit_bytes=VMEM_LIMIT),
        name="ssm_prepare",
    )(per_state(a_re), per_state(a_im), ldt, to_hp(b_re), to_hp(b_im), rows_lanes(c_re), rows_lanes(c_im))
    merge = lambda a: a.reshape(depth, g, width, 2 * LANES)
    return tab, merge(wx), merge(wxs), tt, merge(vt)


def _transpose_pieces(sets, piece):
    sets = [list(vs) for vs in sets]
    n = len(sets[0])
    s = n // 2
    while s:
        keep_low = (piece & s) == 0
        pairs = [(vs, i) for vs in sets for i in range(n) if not i & s]
        moved = [(pltpu.roll(vs[i + s], s * SSM_GROUP, axis=1),
                  pltpu.roll(vs[i], LANES - s * SSM_GROUP, axis=1)) for vs, i in pairs]
        for (vs, i), (from_hi, from_lo) in zip(pairs, moved):
            vs[i], vs[i + s] = jnp.where(keep_low, vs[i], from_hi), jnp.where(keep_low, from_lo, vs[i + s])
        s //= 2
    return sets


def _ssm_kernel(u_ref, h0_ref, tab_ref, wx_ref, wxs_ref, tt_ref, vt_ref, y_ref, ht_ref, ug, xb, xsb, yg,
                *, batch, seq, pitch):
    m_blk = seq // SSM_BLOCK
    chunk = 2 * SUBLANES
    chunks = m_blk // chunk
    per_step = 4
    piece = lax.broadcasted_iota(jnp.int32, (chunk, LANES), 1) // SSM_GROUP

    @pl.when(pl.program_id(0) == 0)
    def _():
        ug[...] = jnp.zeros(ug.shape, F32)

    def chunk_rows(idx):
        b, ch = idx // chunks, idx % chunks
        tok = b * seq + ch * (chunk * SSM_BLOCK)
        return tok, pl.ds(pl.multiple_of(b * pitch + ch * chunk, SUBLANES), chunk)

    def gather(i, carry):
        sets, dests = [], []
        for k in range(per_step):
            tok, rows = chunk_rows(i * per_step + k)
            for half in range(2):
                sets.append([u_ref[pl.ds(tok + half * SLAB_GROUPS + rr, chunk, stride=SSM_BLOCK), :]
                             for rr in range(SLAB_GROUPS)])
                dests.append((half, rows))
        for (half, rows), vs in zip(dests, _transpose_pieces(sets, piece)):
            for gl, v in enumerate(vs):
                ug[gl, half, rows, :] = v
        return carry

    lax.fori_loop(0, batch * chunks // per_step, gather, 0)

    for gl in range(SLAB_GROUPS):
        lhs = jnp.concatenate([ug[gl, 0], ug[gl, 1]], axis=1).astype(BF16)
        x = jnp.dot(lhs, wx_ref[gl], preferred_element_type=F32)
        xb[gl, 0] = x[:, :LANES]
        xb[gl, 1] = x[:, LANES:]
        x = jnp.dot(lhs, wxs_ref[gl], preferred_element_type=F32)
        xsb[gl, 0] = x[:, :LANES]
        xsb[gl, 1] = x[:, LANES:]
        y_in = lax.dot_general(lhs, tt_ref[gl], _NT, preferred_element_type=F32)
        yg[gl, 0] = y_in[:, :LANES]
        yg[gl, 1] = y_in[:, LANES:]

    n_bt = -(-batch // SUBLANES)
    rows_per = min(batch, SUBLANES)
    per_pass = max(1, 4 // n_bt)
    for g0 in range(0, SLAB_GROUPS, per_pass):
        keys = [(gl, d, bt) for gl in range(g0, g0 + per_pass) for d in range(2) for bt in range(n_bt)]
        lane0 = lambda gl, d: (gl * 2 + d) * LANES
        init = []
        for gl, d, bt in keys:
            h = h0_ref[bt * SUBLANES:bt * SUBLANES + rows_per, lane0(gl, d):lane0(gl, d) + LANES]
            init += [h, pltpu.roll(h, SSM_STATE, axis=1)]
        mult = {(gl, d): (tab_ref[gl, d, 0, 0:rows_per, :], tab_ref[gl, d, 1, 0:rows_per, :])
                for gl, d, _ in keys}

        def step(m, hs, keys=keys, mult=mult):
            out = []
            for k, (gl, d, bt) in enumerate(keys):
                h, h_sw = hs[2 * k], hs[2 * k + 1]
                row = m if d == 0 else m_blk - 1 - m
                idx = pl.ds(bt * SUBLANES * pitch + row, rows_per, stride=pitch)
                x = xb[gl, d, idx, :]
                x_sw = xsb[gl, d, idx, :]
                ug[gl, d, idx, :] = h
                a, b = mult[(gl, d)]
                out += [a * h + b * h_sw + x, a * h_sw - b * h + x_sw]
            return tuple(out)

        final = lax.fori_loop(0, m_blk, step, tuple(init), unroll=2)
        for k, (gl, d, bt) in enumerate(keys):
            ht_ref[bt * SUBLANES:bt * SUBLANES + rows_per, lane0(gl, d):lane0(gl, d) + LANES] = final[2 * k]

    for gl in range(SLAB_GROUPS):
        states = jnp.concatenate([ug[gl, 0], ug[gl, 1]], axis=1).astype(BF16)
        y_st = lax.dot_general(states, vt_ref[gl], _NT, preferred_element_type=F32)
        for half in range(2):
            total = yg[gl, half] + y_st[:, half * LANES:(half + 1) * LANES]
            for b in range(batch):
                y_ref[gl, half, b * m_blk:(b + 1) * m_blk, :] = total[b * pitch:b * pitch + m_blk]


def _ssm_call(u2d, h0, tab, wx, wxs, tt, vt, layer, batch, seq):
    n_tok, chans = u2d.shape
    n_slab = chans // LANES
    m_blk = seq // SSM_BLOCK
    pitch = m_blk + SUBLANES
    rows_p = batch * pitch
    st_lanes = SLAB_GROUPS * 2 * LANES
    op_spec = pl.BlockSpec((None, SLAB_GROUPS, MXU_DIM, MXU_DIM), lambda s: (layer, s, 0, 0))
    scratch = pltpu.VMEM((SLAB_GROUPS, 2, rows_p, LANES), F32)
    return pl.pallas_call(
        functools.partial(_ssm_kernel, batch=batch, seq=seq, pitch=pitch),
        grid=(n_slab,),
        in_specs=[
            pl.BlockSpec((n_tok, LANES), lambda s: (0, s)),
            pl.BlockSpec((batch, st_lanes), lambda s: (0, s)),
            pl.BlockSpec((None, SLAB_GROUPS, 2, 2, SUBLANES, LANES), lambda s: (layer, s, 0, 0, 0, 0)),
            op_spec, op_spec, op_spec, op_spec,
        ],
        out_specs=(pl.BlockSpec((None, SLAB_GROUPS, 2, batch * m_blk, LANES), lambda s: (s, 0, 0, 0, 0)),
                   pl.BlockSpec((batch, st_lanes), lambda s: (0, s))),
        out_shape=(jax.ShapeDtypeStruct((n_slab, SLAB_GROUPS, 2, batch * m_blk, LANES), F32),
                   jax.ShapeDtypeStruct((batch, n_slab * st_lanes), F32)),
        scratch_shapes=[scratch, scratch, scratch, scratch],
        compiler_params=pltpu.CompilerParams(
            dimension_semantics=("arbitrary",), vmem_limit_bytes=VMEM_LIMIT),
        name=f"ssm_scan_b{batch}",
    )(u2d, h0, tab, wx, wxs, tt, vt)


def _tokens_from_blocks(yb_ref, rows_ref, slot):
    n_slab, _, _, n_blk, _ = yb_ref.shape
    chunk = 2 * SUBLANES
    piece = lax.broadcasted_iota(jnp.int32, (chunk, LANES), 1) // SSM_GROUP
    for s in range(n_slab):
        sets, dests = [], []
        for c in range(n_blk // chunk):
            for half in range(2):
                sets.append([yb_ref[s, gl, half, c * chunk:(c + 1) * chunk, :] for gl in range(SLAB_GROUPS)])
                dests.append(c * chunk * SSM_BLOCK + half * SLAB_GROUPS)
        for tok, vs in zip(dests, _transpose_pieces(sets, piece)):
            for rr, v in enumerate(vs):
                rows_ref[slot, s, pl.ds(tok + rr, chunk, stride=SSM_BLOCK), :] = v


def _mix_ffn_kernel(attn_ref, yb_first_ref, yb_next_ref, u_ref, x_ref, mod_ref, dskip_ref, wglu_ref,
                    bglu_ref, wout_ref, g2_ref, wffi_ref, wffo_ref, fn_ref, o_ref, y_rows,
                    *, final, ff_chunks):
    d_ff = wffo_ref.shape[0]
    i = pl.program_id(0)
    slot = i % 2

    @pl.when(i == 0)
    def _():
        _tokens_from_blocks(yb_first_ref, y_rows, 0)

    y_ssm = jnp.concatenate([y_rows[slot, s] for s in range(y_rows.shape[1])], axis=1)
    _tokens_from_blocks(yb_next_ref, y_rows, 1 - slot)
    y = y_ssm + dskip_ref[...] * u_ref[...]
    g = 0.5 * y * (1.0 + lax.erf(y * (2.0 ** -0.5)))
    z = jnp.dot(g.astype(BF16), wglu_ref[...], preferred_element_type=F32) + bglu_ref[...]
    ssm_out = g * jax.nn.sigmoid(z)
    mixed = (jnp.dot(attn_ref[...], wout_ref[0:D_ATTN, :], preferred_element_type=F32)
             + jnp.dot(ssm_out.astype(BF16), wout_ref[D_ATTN:, :], preferred_element_type=F32))
    mod = mod_ref[0]
    x1 = x_ref[...] + mod[2:3] * mixed
    h2 = (_rms(x1, g2_ref[...]) * (1.0 + mod[4:5]) + mod[3:4]).astype(BF16)
    tiles = d_ff // MXU_DIM
    edges = [MXU_DIM * ((tiles * ci) // ff_chunks) for ci in range(ff_chunks)] + [d_ff]
    acc = None
    for c0, c1 in zip(edges[:-1], edges[1:]):
        gate = jnp.dot(h2, wffi_ref[:, c0:c1], preferred_element_type=F32)
        up = jnp.dot(h2, wffi_ref[:, d_ff + c0:d_ff + c1], preferred_element_type=F32)
        act = (gate * jax.nn.sigmoid(gate) * up).astype(BF16)
        part = jnp.dot(act, wffo_ref[c0:c1, :], preferred_element_type=F32)
        acc = part if acc is None else acc + part
    x2 = x1 + mod[5:6] * acc
    if final:
        x2 = _rms(x2, fn_ref[...])
    o_ref[...] = x2


def _mix_ffn_call(attn, y, u, x2d, mods, dskip, wglu_b, bglu, wout_b, g2, wffi_b, wffo_b, fnorm,
                  layer, mod_row, tokens_per_batch, tile, final):
    n_tok, d_model = x2d.shape
    d_ssm = u.shape[1]
    d_ff = wffo_b.shape[-2]
    tiles_per_batch = None if tokens_per_batch is None else tokens_per_batch // tile
    row = lambda width: pl.BlockSpec((tile, width), lambda i: (i, 0))
    n_slab = y.shape[0]
    n_tiles = n_tok // tile
    y_block = (n_slab, SLAB_GROUPS, 2, tile // SSM_BLOCK, LANES)
    y_first = pl.BlockSpec(y_block, lambda i: (0, 0, 0, 0, 0))
    y_next = pl.BlockSpec(y_block, lambda i: (0, 0, 0, jnp.minimum(i + 1, n_tiles - 1), 0))
    return pl.pallas_call(
        functools.partial(_mix_ffn_kernel, final=final, ff_chunks=2),
        grid=(n_tiles,),
        in_specs=[
            row(D_ATTN), y_first, y_next, row(d_ssm), row(d_model),
            _mod_spec(d_model, layer, mod_row, tiles_per_batch),
            _layer_spec((1, d_ssm), layer),
            _layer_spec((d_ssm, d_ssm), layer),
            _layer_spec((1, d_ssm), layer),
            _layer_spec((D_ATTN + d_ssm, d_model), layer),
            _layer_spec((1, d_model), layer),
            _layer_spec((d_model, 2 * d_ff), layer),
            _layer_spec((d_ff, d_model), layer),
            _const_spec((1, d_model)),
        ],
        out_specs=row(d_model),
        out_shape=jax.ShapeDtypeStruct((n_tok, d_model), F32),
        scratch_shapes=[pltpu.VMEM((2, n_slab, tile, LANES), F32)],
        compiler_params=pltpu.CompilerParams(
            dimension_semantics=("arbitrary",), vmem_limit_bytes=VMEM_LIMIT),
        name="mix_ffn_final" if final else "mix_ffn",
    )(attn, y, y, u, x2d, mods, dskip, wglu_b, bglu, wout_b, g2, wffi_b, wffo_b, fnorm)


def _rope_tables(n_tokens):
    axis_dim = HEAD_DIM // 2
    rows = n_tokens // GRID_W
    row = jnp.repeat(jnp.arange(rows, dtype=F32), GRID_W)
    col = jnp.tile(jnp.arange(GRID_W, dtype=F32), rows)
    inv_freq = ROPE_THETA ** (-jnp.arange(0, axis_dim, 2, dtype=F32) / axis_dim)
    ang = jnp.concatenate([row[:, None] * inv_freq, col[:, None] * inv_freq], axis=-1)
    cos = jnp.repeat(jnp.cos(ang), 2, axis=-1)
    sin = jnp.repeat(jnp.sin(ang), 2, axis=-1) * jnp.tile(jnp.array([-1.0, 1.0], F32), HEAD_DIM // 2)
    return jnp.tile(cos, (1, N_Q_HEADS)), jnp.tile(sin, (1, N_Q_HEADS))


def _states_to_lanes(st):
    return st.transpose(0, 3, 1, 2, 4).reshape(st.shape[0], -1)


def _lanes_to_states(rows, n_groups):
    return rows.reshape(rows.shape[0], n_groups, 2, 2, SSM_STATE).transpose(0, 2, 3, 1, 4)


def kernel(x_prompt, x_sample, cache_k, cache_v, state_ssm, c, c_ctx, w_mod, b_mod, norm1, norm2, w_in, q_norm, k_norm, ssm_a_re, ssm_a_im, ssm_log_dt, ssm_b_re, ssm_b_im, ssm_c_re, ssm_c_im, ssm_d, w_glu, b_glu, w_out, w_ffn_in, w_ffn_out, final_norm):
    batch, seq, d_model = x_prompt.shape
    dec_batch, dec_seq, _ = x_sample.shape
    depth = w_in.shape[0]
    past = cache_k.shape[2]
    n_groups = ssm_a_re.shape[2]
    d_ssm = n_groups * SSM_GROUP

    cond = jnp.zeros((SUBLANES, d_model), F32).at[0].set(c_ctx).at[1:1 + dec_batch].set(c)
    mods = _mods_call(cond, w_mod, b_mod).reshape(depth, SUBLANES, N_MOD, d_model)
    tab, wx, wxs, tt, vt = _ssm_prep_call(ssm_a_re, ssm_a_im, ssm_log_dt, ssm_b_re, ssm_b_im,
                                          ssm_c_re, ssm_c_im)
    rope_tabs = _rope_tables(dec_seq)
    head_ids = jnp.arange(D_ATTN) // HEAD_DIM
    ones_bd = (head_ids[:, None] == head_ids[None, :]).astype(BF16)

    xp = x_prompt.reshape(batch * seq, d_model)
    xs = x_sample.reshape(dec_batch * dec_seq, d_model)
    zero_state = jnp.zeros((batch, n_groups * 2 * LANES), F32)
    w_in_b = w_in.astype(BF16)
    wglu_b = w_glu.astype(BF16)
    wout_b = w_out.astype(BF16)
    wffi_b = w_ffn_in.astype(BF16)
    wffo_b = w_ffn_out.astype(BF16)
    g1 = norm1.reshape(depth, 1, d_model)
    g2 = norm2.reshape(depth, 1, d_model)
    qg = jnp.tile(q_norm, (1, N_Q_HEADS)).reshape(depth, 1, D_ATTN)
    kg = jnp.tile(k_norm, (1, N_KV_HEADS)).reshape(depth, 1, D_KV)
    dskip = ssm_d.reshape(depth, 1, d_ssm)
    bglu = b_glu.reshape(depth, 1, d_ssm)
    fnorm = final_norm.reshape(1, d_model)
    new_k, new_v, new_s = [], [], []
    for l in range(depth):
        final = l == depth - 1
        for is_ctx in (True, False):
            if is_ctx:
                x2d, n_b, n_l, tq, attn_nb = xp, batch, seq, seq, 4
                mod_row, tokens_per_batch, tabs, ck, cv, h0 = 0, None, None, None, None, zero_state
            else:
                x2d, n_b, n_l, tq, attn_nb = xs, dec_batch, dec_seq, 512, 1
                mod_row, tokens_per_batch, tabs = 1, dec_seq, rope_tabs
                ck = cache_k[:, l].reshape(dec_batch, past, D_KV)
                cv = cache_v[:, l].reshape(dec_batch, past, D_KV)
                h0 = _states_to_lanes(state_ssm[:, l])
            q, k, v, u = _inproj_call(x2d, mods, g1, w_in_b, qg, kg, ones_bd, tabs, l, mod_row,
                                      tokens_per_batch, TOKEN_TILE)
            attn = _attn_call(q.reshape(n_b, n_l, D_ATTN), k.reshape(n_b, n_l, D_KV),
                              v.reshape(n_b, n_l, D_KV), ck, cv, tq, attn_nb)
            y, ht = _ssm_call(u, h0, tab, wx, wxs, tt, vt, l, n_b, n_l)
            x_new = _mix_ffn_call(attn.reshape(-1, D_ATTN), y, u, x2d, mods, dskip, wglu_b, bglu, wout_b,
                                  g2, wffi_b, wffo_b, fnorm, l, mod_row, tokens_per_batch, TOKEN_TILE,
                                  final)
            if is_ctx:
                xp = x_new
                new_k.append(k.reshape(batch, seq, N_KV_HEADS, HEAD_DIM))
                new_v.append(v.reshape(batch, seq, N_KV_HEADS, HEAD_DIM))
                new_s.append(_lanes_to_states(ht, n_groups))
            else:
                xs = x_new
    return (xp.reshape(batch, seq, d_model), xs.reshape(dec_batch, dec_seq, d_model),
            jnp.stack(new_k, axis=1), jnp.stack(new_v, axis=1), jnp.stack(new_s, axis=1))
```

```python
import functools

import jax
import jax.numpy as jnp
from jax import lax
from jax.experimental import pallas as pl
from jax.experimental.pallas import tpu as pltpu

F32 = jnp.float32
BF16 = jnp.bfloat16

HEAD_DIM = 64
N_Q_HEADS = 8
N_KV_HEADS = 2
D_ATTN = N_Q_HEADS * HEAD_DIM
D_KV = N_KV_HEADS * HEAD_DIM
SSM_GROUP = 16
SSM_STATE = 64
GRID_W = 64
ROPE_THETA = 10000.0
N_MOD = 6
EPS = 1e-6

LANES = 128
SUBLANES = 8
MXU_DIM = 256
SSM_BLOCK = MXU_DIM // SSM_GROUP
SLAB_GROUPS = LANES // SSM_GROUP
VMEM_LIMIT = 56 * 1024 * 1024
TOKEN_TILE = 512

_NT = (((1,), (1,)), ((), ()))


def _const_spec(shape):
    nd = len(shape)
    return pl.BlockSpec(shape, lambda *_: (0,) * nd, pipeline_mode=pl.Buffered(1))


def _layer_spec(shape, layer):
    nd = len(shape)
    return pl.BlockSpec((None,) + shape, lambda *_: (layer,) + (0,) * nd,
                        pipeline_mode=pl.Buffered(1))


def _mod_spec(d_model, layer, first_row, tiles_per_batch):
    if tiles_per_batch is None:
        return pl.BlockSpec((None, 1, N_MOD, d_model), lambda i: (layer, first_row, 0, 0))
    return pl.BlockSpec((None, 1, N_MOD, d_model),
                        lambda i: (layer, first_row + i // tiles_per_batch, 0, 0))


def _rms(x, gain):
    ms = jnp.mean(x * x, axis=-1, keepdims=True)
    return x * lax.rsqrt(ms + EPS) * gain


def _mods_kernel(cond_ref, w_ref, b_ref, o_ref):
    c = cond_ref[...]
    act = c * jax.nn.sigmoid(c)
    o_ref[0] = jnp.dot(act.astype(BF16), w_ref[0].astype(BF16), preferred_element_type=F32) + b_ref[0]


def _mods_call(cond, w_mod, b_mod):
    depth, d_model, n_out = w_mod.shape
    rows = cond.shape[0]
    nt = 4
    tn = n_out // nt
    return pl.pallas_call(
        _mods_kernel,
        grid=(depth, nt),
        in_specs=[
            pl.BlockSpec((rows, d_model), lambda l, j: (0, 0)),
            pl.BlockSpec((1, d_model, tn), lambda l, j: (l, 0, j)),
            pl.BlockSpec((1, 1, tn), lambda l, j: (l, 0, j)),
        ],
        out_specs=pl.BlockSpec((1, rows, tn), lambda l, j: (l, 0, j)),
        out_shape=jax.ShapeDtypeStruct((depth, rows, n_out), F32),
        compiler_params=pltpu.CompilerParams(
            dimension_semantics=("arbitrary", "arbitrary"), vmem_limit_bytes=VMEM_LIMIT),
        name="adaln_mods",
    )(cond, w_mod, b_mod.reshape(depth, 1, n_out))


def _group_sumsq(z, ones_ref, width):
    z2 = z * z
    hi = z2.astype(BF16)
    lo = (z2 - hi.astype(F32)).astype(BF16)
    step = min(width, MXU_DIM)
    ones = ones_ref[0:step, 0:step]
    cols = [jnp.dot(hi[:, c:c + step], ones, preferred_element_type=F32)
            + jnp.dot(lo[:, c:c + step], ones, preferred_element_type=F32)
            for c in range(0, width, step)]
    return cols[0] if len(cols) == 1 else jnp.concatenate(cols, axis=1)


def _head_rms(z, gain, ones_ref):
    ss = _group_sumsq(z, ones_ref, z.shape[-1])
    return z * lax.rsqrt(ss * (1.0 / HEAD_DIM) + EPS) * gain


def _rope(z, cos, sin_signed):
    width = z.shape[-1]
    lane = lax.broadcasted_iota(jnp.int32, z.shape, 1)
    nxt = pltpu.roll(z, width - 1, axis=1)
    prv = pltpu.roll(z, 1, axis=1)
    partner = jnp.where((lane & 1) == 0, nxt, prv)
    return z * cos + partner * sin_signed


def _blocks_from_tokens(u_rows, ub_ref):
    n_slab, _, _, n_blk, _ = ub_ref.shape
    chunk = 2 * SUBLANES
    piece = lax.broadcasted_iota(jnp.int32, (chunk, LANES), 1) // SSM_GROUP
    for s in range(n_slab):
        sets, dests = [], []
        for c in range(n_blk // chunk):
            for half in range(2):
                tok = c * chunk * SSM_BLOCK + half * SLAB_GROUPS
                sets.append([u_rows[s, pl.ds(tok + rr, chunk, stride=SSM_BLOCK), :]
                             for rr in range(SLAB_GROUPS)])
                dests.append((half, c))
        for (half, c), vs in zip(dests, _transpose_pieces(sets, piece)):
            for gl, v in enumerate(vs):
                ub_ref[s, gl, half, c * chunk:(c + 1) * chunk, :] = v


def _inproj_kernel(*refs, rope):
    if rope:
        (x_ref, mod_ref, g1_ref, w_ref, qg_ref, kg_ref, ones_ref, cos_ref, sin_ref,
         q_ref, k_ref, v_ref, u_ref, ub_ref, u_rows) = refs
    else:
        (x_ref, mod_ref, g1_ref, w_ref, qg_ref, kg_ref, ones_ref,
         q_ref, k_ref, v_ref, u_ref, ub_ref, u_rows) = refs
    mod = mod_ref[0]
    h = _rms(x_ref[...], g1_ref[...]) * (1.0 + mod[1:2]) + mod[0:1]
    proj = jnp.dot(h.astype(BF16), w_ref[...], preferred_element_type=F32)
    q = _head_rms(proj[:, :D_ATTN], qg_ref[...], ones_ref)
    k = _head_rms(proj[:, D_ATTN:D_ATTN + D_KV], kg_ref[...], ones_ref)
    if rope:
        q = _rope(q, cos_ref[...], sin_ref[...])
        k = _rope(k, cos_ref[:, 0:D_KV], sin_ref[:, 0:D_KV])
    q_ref[...] = (q * (HEAD_DIM ** -0.5)).astype(BF16)
    k_ref[...] = k
    v_ref[...] = proj[:, D_ATTN + D_KV:D_ATTN + 2 * D_KV]
    u = proj[:, D_ATTN + 2 * D_KV:]
    u_ref[...] = u
    for s in range(u_rows.shape[0]):
        u_rows[s] = u[:, s * LANES:(s + 1) * LANES]
    _blocks_from_tokens(u_rows, ub_ref)


def _inproj_call(x2d, mods, g1, w_in_b, qg, kg, ones_bd, rope_tabs, layer, mod_row, tokens_per_batch,
                 tile):
    n_tok, d_model = x2d.shape
    d_in = w_in_b.shape[-1]
    d_ssm = d_in - D_ATTN - 2 * D_KV
    n_slab = d_ssm // LANES
    tiles_per_batch = None if tokens_per_batch is None else tokens_per_batch // tile
    in_specs = [
        pl.BlockSpec((tile, d_model), lambda i: (i, 0)),
        _mod_spec(d_model, layer, mod_row, tiles_per_batch),
        _layer_spec((1, d_model), layer),
        _layer_spec((d_model, d_in), layer),
        _layer_spec((1, D_ATTN), layer),
        _layer_spec((1, D_KV), layer),
        _const_spec((D_ATTN, D_ATTN)),
    ]
    args = [x2d, mods, g1, w_in_b, qg, kg, ones_bd]
    if rope_tabs is not None:
        in_specs += [pl.BlockSpec((tile, D_ATTN), lambda i: (i % tiles_per_batch, 0))] * 2
        args += list(rope_tabs)
    row = lambda width: pl.BlockSpec((tile, width), lambda i: (i, 0))
    return pl.pallas_call(
        functools.partial(_inproj_kernel, rope=rope_tabs is not None),
        grid=(n_tok // tile,),
        in_specs=in_specs,
        out_specs=(row(D_ATTN), row(D_KV), row(D_KV), row(d_ssm),
                   pl.BlockSpec((n_slab, SLAB_GROUPS, 2, tile // SSM_BLOCK, LANES),
                                lambda i: (0, 0, 0, i, 0))),
        out_shape=(jax.ShapeDtypeStruct((n_tok, D_ATTN), BF16),
                   jax.ShapeDtypeStruct((n_tok, D_KV), F32),
                   jax.ShapeDtypeStruct((n_tok, D_KV), F32),
                   jax.ShapeDtypeStruct((n_tok, d_ssm), F32),
                   jax.ShapeDtypeStruct((n_slab, SLAB_GROUPS, 2, n_tok // SSM_BLOCK, LANES), F32)),
        scratch_shapes=[pltpu.VMEM((n_slab, tile, LANES), F32)],
        compiler_params=pltpu.CompilerParams(
            dimension_semantics=("arbitrary",), vmem_limit_bytes=VMEM_LIMIT),
        name="inproj_rope" if rope_tabs is not None else "inproj",
    )(*args)


def _attn_kernel(*refs, has_cache, nb):
    if has_cache:
        q_ref, kn_ref, vn_ref, ck_ref, cv_ref, o_ref, kvar, vvar = refs
    else:
        q_ref, kn_ref, vn_ref, o_ref, kvar, vvar = refs

    @pl.when(pl.program_id(1) == 0)
    def _():
        for bi in range(nb):
            for new_ref, cache_ref, dst in ((kn_ref, ck_ref if has_cache else None, kvar),
                                            (vn_ref, cv_ref if has_cache else None, vvar)):
                src = new_ref[bi]
                if has_cache:
                    src = jnp.concatenate([cache_ref[bi], src], axis=0)
                low = lax.broadcasted_iota(jnp.int32, src.shape, 1) < HEAD_DIM
                head0 = jnp.where(low, src, 0.0)
                head1 = jnp.where(low, 0.0, src)
                dst[4 * bi + 0] = head0.astype(BF16)
                dst[4 * bi + 1] = pltpu.roll(head0, HEAD_DIM, axis=1).astype(BF16)
                dst[4 * bi + 2] = pltpu.roll(head1, HEAD_DIM, axis=1).astype(BF16)
                dst[4 * bi + 3] = head1.astype(BF16)

    slabs = D_ATTN // LANES
    for bi in range(nb):
        for slab in range(slabs):
            qs = q_ref[bi, :, slab * LANES:(slab + 1) * LANES]
            kv = slab // (slabs // N_KV_HEADS)
            acc = None
            for half in range(2):
                idx = 4 * bi + kv * 2 + half
                s = lax.dot_general(qs, kvar[idx], _NT, preferred_element_type=F32)
                m = jnp.max(s, axis=-1, keepdims=True)
                p = jnp.exp(s - m)
                denom = jnp.sum(p, axis=-1, keepdims=True)
                o = jnp.dot(p.astype(BF16), vvar[idx], preferred_element_type=F32) / denom
                acc = o if acc is None else acc + o
            o_ref[bi, :, slab * LANES:(slab + 1) * LANES] = acc.astype(BF16)


def _attn_call(q3, k3, v3, ck3, cv3, tq, nb):
    b, l, _ = q3.shape
    has_cache = ck3 is not None
    s_len = l + (ck3.shape[1] if has_cache else 0)
    in_specs = [
        pl.BlockSpec((nb, tq, D_ATTN), lambda bi, qi: (bi, qi, 0)),
        pl.BlockSpec((nb, l, D_KV), lambda bi, qi: (bi, 0, 0)),
        pl.BlockSpec((nb, l, D_KV), lambda bi, qi: (bi, 0, 0)),
    ]
    args = [q3, k3, v3]
    if has_cache:
        in_specs += [pl.BlockSpec((nb, ck3.shape[1], D_KV), lambda bi, qi: (bi, 0, 0))] * 2
        args += [ck3, cv3]
    return pl.pallas_call(
        functools.partial(_attn_kernel, has_cache=has_cache, nb=nb),
        grid=(b // nb, l // tq),
        in_specs=in_specs,
        out_specs=pl.BlockSpec((nb, tq, D_ATTN), lambda bi, qi: (bi, qi, 0)),
        out_shape=jax.ShapeDtypeStruct((b, l, D_ATTN), BF16),
        scratch_shapes=[pltpu.VMEM((4 * nb, s_len, D_KV), BF16), pltpu.VMEM((4 * nb, s_len, D_KV), BF16)],
        compiler_params=pltpu.CompilerParams(
            dimension_semantics=("arbitrary", "arbitrary"), vmem_limit_bytes=VMEM_LIMIT),
        name="attn_cache" if has_cache else "attn",
    )(*args)


def _dot3(a, b):
    a_hi = a.astype(BF16)
    b_hi = b.astype(BF16)
    a_lo = (a - a_hi.astype(F32)).astype(BF16)
    b_lo = (b - b_hi.astype(F32)).astype(BF16)
    dot = lambda x, y: lax.dot_general(x, y, _NT, preferred_element_type=F32)
    return dot(a_hi, b_hi) + (dot(a_hi, b_lo) + dot(a_lo, b_hi))


def _ssm_prep_kernel(are_ref, aim_ref, ldt_ref, bre_ref, bim_ref, cre_ref, cim_ref,
                     tab_ref, wx_ref, wxs_ref, vt_ref, tt_ref, clm, lbm):
    n_groups = wx_ref.shape[0]
    width = SSM_BLOCK * SSM_GROUP
    shape = (SSM_GROUP, LANES)
    low = lax.broadcasted_iota(jnp.int32, shape, 1) < SSM_STATE
    row_blk = lax.broadcasted_iota(jnp.int32, (width, width), 0) // SSM_GROUP
    col_blk = lax.broadcasted_iota(jnp.int32, (width, width), 1) // SSM_GROUP

    def cmul(xr, xi, yr, yi):
        return xr * yr - xi * yi, xr * yi + xi * yr

    def group(g, carry):
        rows = pl.ds(pl.multiple_of(g * SSM_GROUP, SSM_GROUP), SSM_GROUP)
        tile = None
        for d in range(2):
            lanes = slice(d * LANES, (d + 1) * LANES)
            dt = jnp.exp(ldt_ref[d, rows, :])
            ar = are_ref[d, rows, :]
            ai = aim_ref[d, rows, :]
            mag = jnp.exp(ar * dt)
            lr = mag * jnp.cos(ai * dt)
            li = mag * jnp.sin(ai * dt)
            den = ar * ar + ai * ai
            zr = ((lr - 1.0) * ar + li * ai) / den
            zi = (li * ar - (lr - 1.0) * ai) / den
            bbr, bbi = cmul(zr, zi, bre_ref[d, rows, :], bim_ref[d, rows, :])
            cr = cre_ref[d, rows, :]
            ci = cim_ref[d, rows, :]
            inv = 1.0 / (lr * lr + li * li)
            ir = lr * inv
            ii = -li * inv
            pr, pi = jnp.ones(shape, F32), jnp.zeros(shape, F32)
            qr, qi = pr, pi
            for j in range(SSM_BLOCK + 1):
                c_re, c_im = cmul(cr, ci, pr, pi)
                c_pos = jnp.where(low, c_re, -c_im)
                if j >= 1:
                    vt_ref[g, (j - 1) if d == 0 else (SSM_BLOCK - j), :, lanes] = c_pos.astype(BF16)
                if j < SSM_BLOCK:
                    g_re, g_im = cmul(pr, pi, bbr, bbi)
                    g_pos = jnp.where(low, g_re, g_im)
                    r = (SSM_BLOCK - 1 - j) if d == 0 else j
                    wx_ref[g, r, :, lanes] = g_pos.astype(BF16)
                    wxs_ref[g, r, :, lanes] = jnp.where(low, g_im, g_re).astype(BF16)
                    blk = pl.ds(j * SSM_GROUP, SSM_GROUP)
                    if d == 0:
                        n_re, n_im = cmul(qr, qi, bbr, bbi)
                        clm[blk, :] = c_pos
                        lbm[blk, :] = jnp.where(low, n_re, n_im)
                    else:
                        m_re, m_im = cmul(cr, ci, qr, qi)
                        clm[blk, :] = jnp.where(low, m_re, -m_im)
                        lbm[blk, :] = g_pos
                else:
                    tab_ref[g, d, 0] = pr[0:SUBLANES]
                    tab_ref[g, d, 1] = jnp.where(low, -pi, pi)[0:SUBLANES]
                pr, pi = cmul(pr, pi, lr, li)
                qr, qi = cmul(qr, qi, ir, ii)
            t = _dot3(clm[...], lbm[...])
            t = jnp.where((col_blk <= row_blk) if d == 0 else (col_blk >= row_blk), t, 0.0)
            tile = t if tile is None else tile + t
        tt_ref[g] = tile.astype(BF16)
        return carry

    lax.fori_loop(0, n_groups, group, 0)


def _ssm_prep_call(a_re, a_im, log_dt, b_re, b_im, c_re, c_im):
    depth, ndir, g, p, h = b_re.shape
    halves = 2
    g_half = g // halves
    rows = g_half * h
    width = SSM_BLOCK * h

    def rows_lanes(a):
        a = a.astype(F32).reshape(depth, ndir, g * h, p)
        return jnp.concatenate([a, a], axis=-1)

    per_state = lambda a: rows_lanes(jnp.broadcast_to(a[:, :, :, None, :], (depth, ndir, g, h, p)))
    ldt = rows_lanes(jnp.broadcast_to(log_dt[:, :, :, None, None], (depth, ndir, g, h, p)))
    to_hp = lambda a: rows_lanes(jnp.swapaxes(a, -1, -2))
    in_spec = pl.BlockSpec((None, ndir, rows, LANES), lambda l, s: (l, 0, s, 0))
    out_blk = lambda *shape: pl.BlockSpec((None, g_half) + shape, lambda l, s: (l, s) + (0,) * len(shape))
    op4 = jax.ShapeDtypeStruct((depth, g, SSM_BLOCK, h, 2 * LANES), BF16)
    tab, wx, wxs, vt, tt = pl.pallas_call(
        _ssm_prep_kernel,
        grid=(depth, halves),
        in_specs=[in_spec] * 7,
        out_specs=(out_blk(2, 2, SUBLANES, LANES), out_blk(SSM_BLOCK, h, 2 * LANES),
                   out_blk(SSM_BLOCK, h, 2 * LANES), out_blk(SSM_BLOCK, h, 2 * LANES),
                   out_blk(width, width)),
        out_shape=(jax.ShapeDtypeStruct((depth, g, 2, 2, SUBLANES, LANES), F32), op4, op4, op4,
                   jax.ShapeDtypeStruct((depth, g, width, width), BF16)),
        scratch_shapes=[pltpu.VMEM((width, LANES), F32), pltpu.VMEM((width, LANES), F32)],
        compiler_params=pltpu.CompilerParams(
            dimension_semantics=("arbitrary", "arbitrary"), vmem_limit_bytes=VMEM_LIMIT),
        name="ssm_prepare",
    )(per_state(a_re), per_state(a_im), ldt, to_hp(b_re), to_hp(b_im), rows_lanes(c_re), rows_lanes(c_im))
    merge = lambda a: a.reshape(depth, g, width, 2 * LANES)
    return tab, merge(wx), merge(wxs), tt, merge(vt)


def _transpose_pieces(sets, piece):
    sets = [list(vs) for vs in sets]
    n = len(sets[0])
    s = n // 2
    while s:
        keep_low = (piece & s) == 0
        pairs = [(vs, i) for vs in sets for i in range(n) if not i & s]
        moved = [(pltpu.roll(vs[i + s], s * SSM_GROUP, axis=1),
                  pltpu.roll(vs[i], LANES - s * SSM_GROUP, axis=1)) for vs, i in pairs]
        for (vs, i), (from_hi, from_lo) in zip(pairs, moved):
            vs[i], vs[i + s] = jnp.where(keep_low, vs[i], from_hi), jnp.where(keep_low, from_lo, vs[i + s])
        s //= 2
    return sets


def _ssm_kernel(u_ref, h0_ref, tab_ref, wx_ref, wxs_ref, tt_ref, vt_ref, y_ref, ht_ref, ug, xb, xsb, yg,
                *, batch, seq, pitch):
    m_blk = seq // SSM_BLOCK

    @pl.when(pl.program_id(0) == 0)
    def _():
        ug[...] = jnp.zeros(ug.shape, F32)

    for gl in range(SLAB_GROUPS):
        for half in range(2):
            for b in range(batch):
                ug[gl, half, b * pitch:b * pitch + m_blk, :] = u_ref[gl, half, b * m_blk:(b + 1) * m_blk, :]
        lhs = jnp.concatenate([ug[gl, 0], ug[gl, 1]], axis=1).astype(BF16)
        x = jnp.dot(lhs, wx_ref[gl], preferred_element_type=F32)
        xb[gl, 0] = x[:, :LANES]
        xb[gl, 1] = x[:, LANES:]
        x = jnp.dot(lhs, wxs_ref[gl], preferred_element_type=F32)
        xsb[gl, 0] = x[:, :LANES]
        xsb[gl, 1] = x[:, LANES:]
        y_in = lax.dot_general(lhs, tt_ref[gl], _NT, preferred_element_type=F32)
        yg[gl, 0] = y_in[:, :LANES]
        yg[gl, 1] = y_in[:, LANES:]

    n_bt = -(-batch // SUBLANES)
    rows_per = min(batch, SUBLANES)
    per_pass = max(1, 4 // n_bt)
    for g0 in range(0, SLAB_GROUPS, per_pass):
        keys = [(gl, d, bt) for gl in range(g0, g0 + per_pass) for d in range(2) for bt in range(n_bt)]
        lane0 = lambda gl, d: (gl * 2 + d) * LANES
        init = []
        for gl, d, bt in keys:
            h = h0_ref[bt * SUBLANES:bt * SUBLANES + rows_per, lane0(gl, d):lane0(gl, d) + LANES]
            init += [h, pltpu.roll(h, SSM_STATE, axis=1)]
        mult = {(gl, d): (tab_ref[gl, d, 0, 0:rows_per, :], tab_ref[gl, d, 1, 0:rows_per, :])
                for gl, d, _ in keys}

        def step(m, hs, keys=keys, mult=mult):
            out = []
            for k, (gl, d, bt) in enumerate(keys):
                h, h_sw = hs[2 * k], hs[2 * k + 1]
                row = m if d == 0 else m_blk - 1 - m
                idx = pl.ds(bt * SUBLANES * pitch + row, rows_per, stride=pitch)
                x = xb[gl, d, idx, :]
                x_sw = xsb[gl, d, idx, :]
                ug[gl, d, idx, :] = h
                a, b = mult[(gl, d)]
                out += [a * h + b * h_sw + x, a * h_sw - b * h + x_sw]
            return tuple(out)

        final = lax.fori_loop(0, m_blk, step, tuple(init), unroll=2)
        for k, (gl, d, bt) in enumerate(keys):
            ht_ref[bt * SUBLANES:bt * SUBLANES + rows_per, lane0(gl, d):lane0(gl, d) + LANES] = final[2 * k]

    for gl in range(SLAB_GROUPS):
        states = jnp.concatenate([ug[gl, 0], ug[gl, 1]], axis=1).astype(BF16)
        y_st = lax.dot_general(states, vt_ref[gl], _NT, preferred_element_type=F32)
        for half in range(2):
            total = yg[gl, half] + y_st[:, half * LANES:(half + 1) * LANES]
            for b in range(batch):
                y_ref[gl, half, b * m_blk:(b + 1) * m_blk, :] = total[b * pitch:b * pitch + m_blk]


def _ssm_call(u_blk, h0, tab, wx, wxs, tt, vt, layer, batch, seq):
    n_slab = u_blk.shape[0]
    m_blk = seq // SSM_BLOCK
    blk_spec = pl.BlockSpec((None, SLAB_GROUPS, 2, batch * m_blk, LANES), lambda s: (s, 0, 0, 0, 0))
    pitch = m_blk + SUBLANES
    rows_p = batch * pitch
    st_lanes = SLAB_GROUPS * 2 * LANES
    op_spec = pl.BlockSpec((None, SLAB_GROUPS, MXU_DIM, MXU_DIM), lambda s: (layer, s, 0, 0))
    scratch = pltpu.VMEM((SLAB_GROUPS, 2, rows_p, LANES), F32)
    return pl.pallas_call(
        functools.partial(_ssm_kernel, batch=batch, seq=seq, pitch=pitch),
        grid=(n_slab,),
        in_specs=[
            blk_spec,
            pl.BlockSpec((batch, st_lanes), lambda s: (0, s)),
            pl.BlockSpec((None, SLAB_GROUPS, 2, 2, SUBLANES, LANES), lambda s: (layer, s, 0, 0, 0, 0)),
            op_spec, op_spec, op_spec, op_spec,
        ],
        out_specs=(blk_spec,
                   pl.BlockSpec((batch, st_lanes), lambda s: (0, s))),
        out_shape=(jax.ShapeDtypeStruct((n_slab, SLAB_GROUPS, 2, batch * m_blk, LANES), F32),
                   jax.ShapeDtypeStruct((batch, n_slab * st_lanes), F32)),
        scratch_shapes=[scratch, scratch, scratch, scratch],
        compiler_params=pltpu.CompilerParams(
            dimension_semantics=("arbitrary",), vmem_limit_bytes=VMEM_LIMIT),
        name=f"ssm_scan_b{batch}",
    )(u_blk, h0, tab, wx, wxs, tt, vt)


def _tokens_from_blocks(yb_ref, rows_ref, slot):
    n_slab, _, _, n_blk, _ = yb_ref.shape
    chunk = 2 * SUBLANES
    piece = lax.broadcasted_iota(jnp.int32, (chunk, LANES), 1) // SSM_GROUP
    for s in range(n_slab):
        sets, dests = [], []
        for c in range(n_blk // chunk):
            for half in range(2):
                sets.append([yb_ref[s, gl, half, c * chunk:(c + 1) * chunk, :] for gl in range(SLAB_GROUPS)])
                dests.append(c * chunk * SSM_BLOCK + half * SLAB_GROUPS)
        for tok, vs in zip(dests, _transpose_pieces(sets, piece)):
            for rr, v in enumerate(vs):
                rows_ref[slot, s, pl.ds(tok + rr, chunk, stride=SSM_BLOCK), :] = v


def _mix_ffn_kernel(attn_ref, yb_first_ref, yb_next_ref, u_ref, x_ref, mod_ref, dskip_ref, wglu_ref,
                    bglu_ref, wout_ref, g2_ref, wffi_ref, wffo_ref, fn_ref, o_ref, y_rows,
                    *, final, ff_chunks):
    d_ff = wffo_ref.shape[0]
    i = pl.program_id(0)
    slot = i % 2

    @pl.when(i == 0)
    def _():
        _tokens_from_blocks(yb_first_ref, y_rows, 0)

    y_ssm = jnp.concatenate([y_rows[slot, s] for s in range(y_rows.shape[1])], axis=1)
    _tokens_from_blocks(yb_next_ref, y_rows, 1 - slot)
    y = y_ssm + dskip_ref[...] * u_ref[...]
    g = 0.5 * y * (1.0 + lax.erf(y * (2.0 ** -0.5)))
    z = jnp.dot(g.astype(BF16), wglu_ref[...], preferred_element_type=F32) + bglu_ref[...]
    ssm_out = g * jax.nn.sigmoid(z)
    mixed = (jnp.dot(attn_ref[...], wout_ref[0:D_ATTN, :], preferred_element_type=F32)
             + jnp.dot(ssm_out.astype(BF16), wout_ref[D_ATTN:, :], preferred_element_type=F32))
    mod = mod_ref[0]
    x1 = x_ref[...] + mod[2:3] * mixed
    h2 = (_rms(x1, g2_ref[...]) * (1.0 + mod[4:5]) + mod[3:4]).astype(BF16)
    tiles = d_ff // MXU_DIM
    edges = [MXU_DIM * ((tiles * ci) // ff_chunks) for ci in range(ff_chunks)] + [d_ff]
    acc = None
    for c0, c1 in zip(edges[:-1], edges[1:]):
        gate = jnp.dot(h2, wffi_ref[:, c0:c1], preferred_element_type=F32)
        up = jnp.dot(h2, wffi_ref[:, d_ff + c0:d_ff + c1], preferred_element_type=F32)
        act = (gate * jax.nn.sigmoid(gate) * up).astype(BF16)
        part = jnp.dot(act, wffo_ref[c0:c1, :], preferred_element_type=F32)
        acc = part if acc is None else acc + part
    x2 = x1 + mod[5:6] * acc
    if final:
        x2 = _rms(x2, fn_ref[...])
    o_ref[...] = x2


def _mix_ffn_call(attn, y, u, x2d, mods, dskip, wglu_b, bglu, wout_b, g2, wffi_b, wffo_b, fnorm,
                  layer, mod_row, tokens_per_batch, tile, final):
    n_tok, d_model = x2d.shape
    d_ssm = u.shape[1]
    d_ff = wffo_b.shape[-2]
    tiles_per_batch = None if tokens_per_batch is None else tokens_per_batch // tile
    row = lambda width: pl.BlockSpec((tile, width), lambda i: (i, 0))
    n_slab = y.shape[0]
    n_tiles = n_tok // tile
    y_block = (n_slab, SLAB_GROUPS, 2, tile // SSM_BLOCK, LANES)
    y_first = pl.BlockSpec(y_block, lambda i: (0, 0, 0, 0, 0))
    y_next = pl.BlockSpec(y_block, lambda i: (0, 0, 0, jnp.minimum(i + 1, n_tiles - 1), 0))
    return pl.pallas_call(
        functools.partial(_mix_ffn_kernel, final=final, ff_chunks=2),
        grid=(n_tiles,),
        in_specs=[
            row(D_ATTN), y_first, y_next, row(d_ssm), row(d_model),
            _mod_spec(d_model, layer, mod_row, tiles_per_batch),
            _layer_spec((1, d_ssm), layer),
            _layer_spec((d_ssm, d_ssm), layer),
            _layer_spec((1, d_ssm), layer),
            _layer_spec((D_ATTN + d_ssm, d_model), layer),
            _layer_spec((1, d_model), layer),
            _layer_spec((d_model, 2 * d_ff), layer),
            _layer_spec((d_ff, d_model), layer),
            _const_spec((1, d_model)),
        ],
        out_specs=row(d_model),
        out_shape=jax.ShapeDtypeStruct((n_tok, d_model), F32),
        scratch_shapes=[pltpu.VMEM((2, n_slab, tile, LANES), F32)],
        compiler_params=pltpu.CompilerParams(
            dimension_semantics=("arbitrary",), vmem_limit_bytes=VMEM_LIMIT),
        name="mix_ffn_final" if final else "mix_ffn",
    )(attn, y, y, u, x2d, mods, dskip, wglu_b, bglu, wout_b, g2, wffi_b, wffo_b, fnorm)


def _rope_tables(n_tokens):
    axis_dim = HEAD_DIM // 2
    rows = n_tokens // GRID_W
    row = jnp.repeat(jnp.arange(rows, dtype=F32), GRID_W)
    col = jnp.tile(jnp.arange(GRID_W, dtype=F32), rows)
    inv_freq = ROPE_THETA ** (-jnp.arange(0, axis_dim, 2, dtype=F32) / axis_dim)
    ang = jnp.concatenate([row[:, None] * inv_freq, col[:, None] * inv_freq], axis=-1)
    cos = jnp.repeat(jnp.cos(ang), 2, axis=-1)
    sin = jnp.repeat(jnp.sin(ang), 2, axis=-1) * jnp.tile(jnp.array([-1.0, 1.0], F32), HEAD_DIM // 2)
    return jnp.tile(cos, (1, N_Q_HEADS)), jnp.tile(sin, (1, N_Q_HEADS))


def _states_to_lanes(st):
    return st.transpose(0, 3, 1, 2, 4).reshape(st.shape[0], -1)


def _lanes_to_states(rows, n_groups):
    return rows.reshape(rows.shape[0], n_groups, 2, 2, SSM_STATE).transpose(0, 2, 3, 1, 4)


def kernel(x_prompt, x_sample, cache_k, cache_v, state_ssm, c, c_ctx, w_mod, b_mod, norm1, norm2, w_in, q_norm, k_norm, ssm_a_re, ssm_a_im, ssm_log_dt, ssm_b_re, ssm_b_im, ssm_c_re, ssm_c_im, ssm_d, w_glu, b_glu, w_out, w_ffn_in, w_ffn_out, final_norm):
    batch, seq, d_model = x_prompt.shape
    dec_batch, dec_seq, _ = x_sample.shape
    depth = w_in.shape[0]
    past = cache_k.shape[2]
    n_groups = ssm_a_re.shape[2]
    d_ssm = n_groups * SSM_GROUP

    cond = jnp.zeros((SUBLANES, d_model), F32).at[0].set(c_ctx).at[1:1 + dec_batch].set(c)
    mods = _mods_call(cond, w_mod, b_mod).reshape(depth, SUBLANES, N_MOD, d_model)
    tab, wx, wxs, tt, vt = _ssm_prep_call(ssm_a_re, ssm_a_im, ssm_log_dt, ssm_b_re, ssm_b_im,
                                          ssm_c_re, ssm_c_im)
    rope_tabs = _rope_tables(dec_seq)
    head_ids = jnp.arange(D_ATTN) // HEAD_DIM
    ones_bd = (head_ids[:, None] == head_ids[None, :]).astype(BF16)

    xp = x_prompt.reshape(batch * seq, d_model)
    xs = x_sample.reshape(dec_batch * dec_seq, d_model)
    zero_state = jnp.zeros((batch, n_groups * 2 * LANES), F32)
    w_in_b = w_in.astype(BF16)
    wglu_b = w_glu.astype(BF16)
    wout_b = w_out.astype(BF16)
    wffi_b = w_ffn_in.astype(BF16)
    wffo_b = w_ffn_out.astype(BF16)
    g1 = norm1.reshape(depth, 1, d_model)
    g2 = norm2.reshape(depth, 1, d_model)
    qg = jnp.tile(q_norm, (1, N_Q_HEADS)).reshape(depth, 1, D_ATTN)
    kg = jnp.tile(k_norm, (1, N_KV_HEADS)).reshape(depth, 1, D_KV)
    dskip = ssm_d.reshape(depth, 1, d_ssm)
    bglu = b_glu.reshape(depth, 1, d_ssm)
    fnorm = final_norm.reshape(1, d_model)
    new_k, new_v, new_s = [], [], []
    for l in range(depth):
        final = l == depth - 1
        for is_ctx in (True, False):
            if is_ctx:
                x2d, n_b, n_l, tq, attn_nb = xp, batch, seq, seq, 4
                mod_row, tokens_per_batch, tabs, ck, cv, h0 = 0, None, None, None, None, zero_state
            else:
                x2d, n_b, n_l, tq, attn_nb = xs, dec_batch, dec_seq, 512, 1
                mod_row, tokens_per_batch, tabs = 1, dec_seq, rope_tabs
                ck = cache_k[:, l].reshape(dec_batch, past, D_KV)
                cv = cache_v[:, l].reshape(dec_batch, past, D_KV)
                h0 = _states_to_lanes(state_ssm[:, l])
            q, k, v, u, u_blk = _inproj_call(x2d, mods, g1, w_in_b, qg, kg, ones_bd, tabs, l, mod_row,
                                             tokens_per_batch, TOKEN_TILE)
            attn = _attn_call(q.reshape(n_b, n_l, D_ATTN), k.reshape(n_b, n_l, D_KV),
                              v.reshape(n_b, n_l, D_KV), ck, cv, tq, attn_nb)
            y, ht = _ssm_call(u_blk, h0, tab, wx, wxs, tt, vt, l, n_b, n_l)
            x_new = _mix_ffn_call(attn.reshape(-1, D_ATTN), y, u, x2d, mods, dskip, wglu_b, bglu, wout_b,
                                  g2, wffi_b, wffo_b, fnorm, l, mod_row, tokens_per_batch, TOKEN_TILE,
                                  final)
            if is_ctx:
                xp = x_new
                new_k.append(k.reshape(batch, seq, N_KV_HEADS, HEAD_DIM))
                new_v.append(v.reshape(batch, seq, N_KV_HEADS, HEAD_DIM))
                new_s.append(_lanes_to_states(ht, n_groups))
            else:
                xs = x_new
    return (xp.reshape(batch, seq, d_model), xs.reshape(dec_batch, dec_seq, d_model),
            jnp.stack(new_k, axis=1), jnp.stack(new_v, axis=1), jnp.stack(new_s, axis=1))
```

```python
import functools

import jax
import jax.numpy as jnp
from jax import lax
from jax.experimental import pallas as pl
from jax.experimental.pallas import tpu as pltpu

F32 = jnp.float32
BF16 = jnp.bfloat16

HEAD_DIM = 64
N_Q_HEADS = 8
N_KV_HEADS = 2
D_ATTN = N_Q_HEADS * HEAD_DIM
D_KV = N_KV_HEADS * HEAD_DIM
SSM_GROUP = 16
SSM_STATE = 64
GRID_W = 64
ROPE_THETA = 10000.0
N_MOD = 6
EPS = 1e-6

LANES = 128
SUBLANES = 8
MXU_DIM = 256
SSM_BLOCK = MXU_DIM // SSM_GROUP
SLAB_GROUPS = LANES // SSM_GROUP
VMEM_LIMIT = 56 * 1024 * 1024
TOKEN_TILE = 512

_NT = (((1,), (1,)), ((), ()))


def _const_spec(shape):
    nd = len(shape)
    return pl.BlockSpec(shape, lambda *_: (0,) * nd, pipeline_mode=pl.Buffered(1))


def _layer_spec(shape, layer):
    nd = len(shape)
    return pl.BlockSpec((None,) + shape, lambda *_: (layer,) + (0,) * nd,
                        pipeline_mode=pl.Buffered(1))


def _mod_spec(d_model, layer, first_row, tiles_per_batch):
    if tiles_per_batch is None:
        return pl.BlockSpec((None, 1, N_MOD, d_model), lambda i: (layer, first_row, 0, 0))
    return pl.BlockSpec((None, 1, N_MOD, d_model),
                        lambda i: (layer, first_row + i // tiles_per_batch, 0, 0))


def _rms(x, gain):
    ms = jnp.mean(x * x, axis=-1, keepdims=True)
    return x * lax.rsqrt(ms + EPS) * gain


def _mods_kernel(cond_ref, w_ref, b_ref, o_ref):
    c = cond_ref[...]
    act = c * jax.nn.sigmoid(c)
    o_ref[0] = jnp.dot(act.astype(BF16), w_ref[0].astype(BF16), preferred_element_type=F32) + b_ref[0]


def _mods_call(cond, w_mod, b_mod):
    depth, d_model, n_out = w_mod.shape
    rows = cond.shape[0]
    nt = 4
    tn = n_out // nt
    return pl.pallas_call(
        _mods_kernel,
        grid=(depth, nt),
        in_specs=[
            pl.BlockSpec((rows, d_model), lambda l, j: (0, 0)),
            pl.BlockSpec((1, d_model, tn), lambda l, j: (l, 0, j)),
            pl.BlockSpec((1, 1, tn), lambda l, j: (l, 0, j)),
        ],
        out_specs=pl.BlockSpec((1, rows, tn), lambda l, j: (l, 0, j)),
        out_shape=jax.ShapeDtypeStruct((depth, rows, n_out), F32),
        compiler_params=pltpu.CompilerParams(
            dimension_semantics=("arbitrary", "arbitrary"), vmem_limit_bytes=VMEM_LIMIT),
        name="adaln_mods",
    )(cond, w_mod, b_mod.reshape(depth, 1, n_out))


def _group_sumsq(z, ones_ref, width):
    z2 = z * z
    hi = z2.astype(BF16)
    lo = (z2 - hi.astype(F32)).astype(BF16)
    step = min(width, MXU_DIM)
    ones = ones_ref[0:step, 0:step]
    cols = [jnp.dot(hi[:, c:c + step], ones, preferred_element_type=F32)
            + jnp.dot(lo[:, c:c + step], ones, preferred_element_type=F32)
            for c in range(0, width, step)]
    return cols[0] if len(cols) == 1 else jnp.concatenate(cols, axis=1)


def _head_rms(z, gain, ones_ref):
    ss = _group_sumsq(z, ones_ref, z.shape[-1])
    return z * lax.rsqrt(ss * (1.0 / HEAD_DIM) + EPS) * gain


def _rope(z, cos, sin_signed):
    width = z.shape[-1]
    lane = lax.broadcasted_iota(jnp.int32, z.shape, 1)
    nxt = pltpu.roll(z, width - 1, axis=1)
    prv = pltpu.roll(z, 1, axis=1)
    partner = jnp.where((lane & 1) == 0, nxt, prv)
    return z * cos + partner * sin_signed


def _blocks_from_tokens(u_rows, ub_ref):
    n_slab, _, _, n_blk, _ = ub_ref.shape
    chunk = 2 * SUBLANES
    piece = lax.broadcasted_iota(jnp.int32, (chunk, LANES), 1) // SSM_GROUP
    for s in range(n_slab):
        sets, dests = [], []
        for c in range(n_blk // chunk):
            for half in range(2):
                tok = c * chunk * SSM_BLOCK + half * SLAB_GROUPS
                sets.append([u_rows[s, pl.ds(tok + rr, chunk, stride=SSM_BLOCK), :]
                             for rr in range(SLAB_GROUPS)])
                dests.append((half, c))
        for (half, c), vs in zip(dests, _transpose_pieces(sets, piece)):
            for gl, v in enumerate(vs):
                ub_ref[s, gl, half, c * chunk:(c + 1) * chunk, :] = v


def _inproj_kernel(*refs, rope):
    if rope:
        (x_ref, mod_ref, g1_ref, w_ref, qg_ref, kg_ref, ones_ref, cos_ref, sin_ref,
         q_ref, k_ref, v_ref, u_ref, ub_ref, u_rows) = refs
    else:
        (x_ref, mod_ref, g1_ref, w_ref, qg_ref, kg_ref, ones_ref,
         q_ref, k_ref, v_ref, u_ref, ub_ref, u_rows) = refs
    mod = mod_ref[0]
    h = (_rms(x_ref[...], g1_ref[...]) * (1.0 + mod[1:2]) + mod[0:1]).astype(BF16)
    qkv_cols = D_ATTN + 2 * D_KV
    u = jnp.dot(h, w_ref[:, qkv_cols:], preferred_element_type=F32)
    u_ref[...] = u
    for s in range(u_rows.shape[0]):
        u_rows[s] = u[:, s * LANES:(s + 1) * LANES]
    _blocks_from_tokens(u_rows, ub_ref)
    proj = jnp.dot(h, w_ref[:, :qkv_cols], preferred_element_type=F32)
    q = _head_rms(proj[:, :D_ATTN], qg_ref[...], ones_ref)
    k = _head_rms(proj[:, D_ATTN:D_ATTN + D_KV], kg_ref[...], ones_ref)
    if rope:
        q = _rope(q, cos_ref[...], sin_ref[...])
        k = _rope(k, cos_ref[:, 0:D_KV], sin_ref[:, 0:D_KV])
    q_ref[...] = (q * (HEAD_DIM ** -0.5)).astype(BF16)
    k_ref[...] = k
    v_ref[...] = proj[:, D_ATTN + D_KV:qkv_cols]


def _inproj_call(x2d, mods, g1, w_in_b, qg, kg, ones_bd, rope_tabs, layer, mod_row, tokens_per_batch,
                 tile):
    n_tok, d_model = x2d.shape
    d_in = w_in_b.shape[-1]
    d_ssm = d_in - D_ATTN - 2 * D_KV
    n_slab = d_ssm // LANES
    tiles_per_batch = None if tokens_per_batch is None else tokens_per_batch // tile
    in_specs = [
        pl.BlockSpec((tile, d_model), lambda i: (i, 0)),
        _mod_spec(d_model, layer, mod_row, tiles_per_batch),
        _layer_spec((1, d_model), layer),
        _layer_spec((d_model, d_in), layer),
        _layer_spec((1, D_ATTN), layer),
        _layer_spec((1, D_KV), layer),
        _const_spec((D_ATTN, D_ATTN)),
    ]
    args = [x2d, mods, g1, w_in_b, qg, kg, ones_bd]
    if rope_tabs is not None:
        in_specs += [pl.BlockSpec((tile, D_ATTN), lambda i: (i % tiles_per_batch, 0))] * 2
        args += list(rope_tabs)
    row = lambda width: pl.BlockSpec((tile, width), lambda i: (i, 0))
    return pl.pallas_call(
        functools.partial(_inproj_kernel, rope=rope_tabs is not None),
        grid=(n_tok // tile,),
        in_specs=in_specs,
        out_specs=(row(D_ATTN), row(D_KV), row(D_KV), row(d_ssm),
                   pl.BlockSpec((n_slab, SLAB_GROUPS, 2, tile // SSM_BLOCK, LANES),
                                lambda i: (0, 0, 0, i, 0))),
        out_shape=(jax.ShapeDtypeStruct((n_tok, D_ATTN), BF16),
                   jax.ShapeDtypeStruct((n_tok, D_KV), F32),
                   jax.ShapeDtypeStruct((n_tok, D_KV), F32),
                   jax.ShapeDtypeStruct((n_tok, d_ssm), F32),
                   jax.ShapeDtypeStruct((n_slab, SLAB_GROUPS, 2, n_tok // SSM_BLOCK, LANES), F32)),
        scratch_shapes=[pltpu.VMEM((n_slab, tile, LANES), F32)],
        compiler_params=pltpu.CompilerParams(
            dimension_semantics=("arbitrary",), vmem_limit_bytes=VMEM_LIMIT),
        name="inproj_rope" if rope_tabs is not None else "inproj",
    )(*args)


def _attn_kernel(*refs, has_cache, nb):
    if has_cache:
        q_ref, kn_ref, vn_ref, ck_ref, cv_ref, o_ref, kvar, vvar = refs
    else:
        q_ref, kn_ref, vn_ref, o_ref, kvar, vvar = refs

    @pl.when(pl.program_id(1) == 0)
    def _():
        for bi in range(nb):
            for new_ref, cache_ref, dst in ((kn_ref, ck_ref if has_cache else None, kvar),
                                            (vn_ref, cv_ref if has_cache else None, vvar)):
                src = new_ref[bi]
                if has_cache:
                    src = jnp.concatenate([cache_ref[bi], src], axis=0)
                low = lax.broadcasted_iota(jnp.int32, src.shape, 1) < HEAD_DIM
                head0 = jnp.where(low, src, 0.0)
                head1 = jnp.where(low, 0.0, src)
                dst[4 * bi + 0] = head0.astype(BF16)
                dst[4 * bi + 1] = pltpu.roll(head0, HEAD_DIM, axis=1).astype(BF16)
                dst[4 * bi + 2] = pltpu.roll(head1, HEAD_DIM, axis=1).astype(BF16)
                dst[4 * bi + 3] = head1.astype(BF16)

    slabs = D_ATTN // LANES
    for bi in range(nb):
        for slab in range(slabs):
            qs = q_ref[bi, :, slab * LANES:(slab + 1) * LANES]
            kv = slab // (slabs // N_KV_HEADS)
            acc = None
            for half in range(2):
                idx = 4 * bi + kv * 2 + half
                s = lax.dot_general(qs, kvar[idx], _NT, preferred_element_type=F32)
                m = jnp.max(s, axis=-1, keepdims=True)
                p = jnp.exp(s - m)
                denom = jnp.sum(p, axis=-1, keepdims=True)
                o = jnp.dot(p.astype(BF16), vvar[idx], preferred_element_type=F32) / denom
                acc = o if acc is None else acc + o
            o_ref[bi, :, slab * LANES:(slab + 1) * LANES] = acc.astype(BF16)


def _attn_call(q3, k3, v3, ck3, cv3, tq, nb):
    b, l, _ = q3.shape
    has_cache = ck3 is not None
    s_len = l + (ck3.shape[1] if has_cache else 0)
    in_specs = [
        pl.BlockSpec((nb, tq, D_ATTN), lambda bi, qi: (bi, qi, 0)),
        pl.BlockSpec((nb, l, D_KV), lambda bi, qi: (bi, 0, 0)),
        pl.BlockSpec((nb, l, D_KV), lambda bi, qi: (bi, 0, 0)),
    ]
    args = [q3, k3, v3]
    if has_cache:
        in_specs += [pl.BlockSpec((nb, ck3.shape[1], D_KV), lambda bi, qi: (bi, 0, 0))] * 2
        args += [ck3, cv3]
    return pl.pallas_call(
        functools.partial(_attn_kernel, has_cache=has_cache, nb=nb),
        grid=(b // nb, l // tq),
        in_specs=in_specs,
        out_specs=pl.BlockSpec((nb, tq, D_ATTN), lambda bi, qi: (bi, qi, 0)),
        out_shape=jax.ShapeDtypeStruct((b, l, D_ATTN), BF16),
        scratch_shapes=[pltpu.VMEM((4 * nb, s_len, D_KV), BF16), pltpu.VMEM((4 * nb, s_len, D_KV), BF16)],
        compiler_params=pltpu.CompilerParams(
            dimension_semantics=("arbitrary", "arbitrary"), vmem_limit_bytes=VMEM_LIMIT),
        name="attn_cache" if has_cache else "attn",
    )(*args)


def _dot3(a, b):
    a_hi = a.astype(BF16)
    b_hi = b.astype(BF16)
    a_lo = (a - a_hi.astype(F32)).astype(BF16)
    b_lo = (b - b_hi.astype(F32)).astype(BF16)
    dot = lambda x, y: lax.dot_general(x, y, _NT, preferred_element_type=F32)
    return dot(a_hi, b_hi) + (dot(a_hi, b_lo) + dot(a_lo, b_hi))


def _ssm_prep_kernel(are_ref, aim_ref, ldt_ref, bre_ref, bim_ref, cre_ref, cim_ref,
                     tab_ref, wx_ref, wxs_ref, vt_ref, tt_ref, clm, lbm):
    n_groups = wx_ref.shape[0]
    width = SSM_BLOCK * SSM_GROUP
    shape = (SSM_GROUP, LANES)
    low = lax.broadcasted_iota(jnp.int32, shape, 1) < SSM_STATE
    row_blk = lax.broadcasted_iota(jnp.int32, (width, width), 0) // SSM_GROUP
    col_blk = lax.broadcasted_iota(jnp.int32, (width, width), 1) // SSM_GROUP

    def cmul(xr, xi, yr, yi):
        return xr * yr - xi * yi, xr * yi + xi * yr

    def group(g, carry):
        rows = pl.ds(pl.multiple_of(g * SSM_GROUP, SSM_GROUP), SSM_GROUP)
        tile = None
        for d in range(2):
            lanes = slice(d * LANES, (d + 1) * LANES)
            dt = jnp.exp(ldt_ref[d, rows, :])
            ar = are_ref[d, rows, :]
            ai = aim_ref[d, rows, :]
            mag = jnp.exp(ar * dt)
            lr = mag * jnp.cos(ai * dt)
            li = mag * jnp.sin(ai * dt)
            den = ar * ar + ai * ai
            zr = ((lr - 1.0) * ar + li * ai) / den
            zi = (li * ar - (lr - 1.0) * ai) / den
            bbr, bbi = cmul(zr, zi, bre_ref[d, rows, :], bim_ref[d, rows, :])
            cr = cre_ref[d, rows, :]
            ci = cim_ref[d, rows, :]
            inv = 1.0 / (lr * lr + li * li)
            ir = lr * inv
            ii = -li * inv
            pr, pi = jnp.ones(shape, F32), jnp.zeros(shape, F32)
            qr, qi = pr, pi
            for j in range(SSM_BLOCK + 1):
                c_re, c_im = cmul(cr, ci, pr, pi)
                c_pos = jnp.where(low, c_re, -c_im)
                if j >= 1:
                    vt_ref[g, (j - 1) if d == 0 else (SSM_BLOCK - j), :, lanes] = c_pos.astype(BF16)
                if j < SSM_BLOCK:
                    g_re, g_im = cmul(pr, pi, bbr, bbi)
                    g_pos = jnp.where(low, g_re, g_im)
                    r = (SSM_BLOCK - 1 - j) if d == 0 else j
                    wx_ref[g, r, :, lanes] = g_pos.astype(BF16)
                    wxs_ref[g, r, :, lanes] = jnp.where(low, g_im, g_re).astype(BF16)
                    blk = pl.ds(j * SSM_GROUP, SSM_GROUP)
                    if d == 0:
                        n_re, n_im = cmul(qr, qi, bbr, bbi)
                        clm[blk, :] = c_pos
                        lbm[blk, :] = jnp.where(low, n_re, n_im)
                    else:
                        m_re, m_im = cmul(cr, ci, qr, qi)
                        clm[blk, :] = jnp.where(low, m_re, -m_im)
                        lbm[blk, :] = g_pos
                else:
                    tab_ref[g, d, 0] = pr[0:SUBLANES]
                    tab_ref[g, d, 1] = jnp.where(low, -pi, pi)[0:SUBLANES]
                pr, pi = cmul(pr, pi, lr, li)
                qr, qi = cmul(qr, qi, ir, ii)
            t = _dot3(clm[...], lbm[...])
            t = jnp.where((col_blk <= row_blk) if d == 0 else (col_blk >= row_blk), t, 0.0)
            tile = t if tile is None else tile + t
        tt_ref[g] = tile.astype(BF16)
        return carry

    lax.fori_loop(0, n_groups, group, 0)


def _ssm_prep_call(a_re, a_im, log_dt, b_re, b_im, c_re, c_im):
    depth, ndir, g, p, h = b_re.shape
    halves = 2
    g_half = g // halves
    rows = g_half * h
    width = SSM_BLOCK * h

    def rows_lanes(a):
        a = a.astype(F32).reshape(depth, ndir, g * h, p)
        return jnp.concatenate([a, a], axis=-1)

    per_state = lambda a: rows_lanes(jnp.broadcast_to(a[:, :, :, None, :], (depth, ndir, g, h, p)))
    ldt = rows_lanes(jnp.broadcast_to(log_dt[:, :, :, None, None], (depth, ndir, g, h, p)))
    to_hp = lambda a: rows_lanes(jnp.swapaxes(a, -1, -2))
    in_spec = pl.BlockSpec((None, ndir, rows, LANES), lambda l, s: (l, 0, s, 0))
    out_blk = lambda *shape: pl.BlockSpec((None, g_half) + shape, lambda l, s: (l, s) + (0,) * len(shape))
    op4 = jax.ShapeDtypeStruct((depth, g, SSM_BLOCK, h, 2 * LANES), BF16)
    tab, wx, wxs, vt, tt = pl.pallas_call(
        _ssm_prep_kernel,
        grid=(depth, halves),
        in_specs=[in_spec] * 7,
        out_specs=(out_blk(2, 2, SUBLANES, LANES), out_blk(SSM_BLOCK, h, 2 * LANES),
                   out_blk(SSM_BLOCK, h, 2 * LANES), out_blk(SSM_BLOCK, h, 2 * LANES),
                   out_blk(width, width)),
        out_shape=(jax.ShapeDtypeStruct((depth, g, 2, 2, SUBLANES, LANES), F32), op4, op4, op4,
                   jax.ShapeDtypeStruct((depth, g, width, width), BF16)),
        scratch_shapes=[pltpu.VMEM((width, LANES), F32), pltpu.VMEM((width, LANES), F32)],
        compiler_params=pltpu.CompilerParams(
            dimension_semantics=("arbitrary", "arbitrary"), vmem_limit_bytes=VMEM_LIMIT),
        name="ssm_prepare",
    )(per_state(a_re), per_state(a_im), ldt, to_hp(b_re), to_hp(b_im), rows_lanes(c_re), rows_lanes(c_im))
    merge = lambda a: a.reshape(depth, g, width, 2 * LANES)
    return tab, merge(wx), merge(wxs), tt, merge(vt)


def _transpose_pieces(sets, piece):
    sets = [list(vs) for vs in sets]
    n = len(sets[0])
    s = n // 2
    while s:
        keep_low = (piece & s) == 0
        pairs = [(vs, i) for vs in sets for i in range(n) if not i & s]
        moved = [(pltpu.roll(vs[i + s], s * SSM_GROUP, axis=1),
                  pltpu.roll(vs[i], LANES - s * SSM_GROUP, axis=1)) for vs, i in pairs]
        for (vs, i), (from_hi, from_lo) in zip(pairs, moved):
            vs[i], vs[i + s] = jnp.where(keep_low, vs[i], from_hi), jnp.where(keep_low, from_lo, vs[i + s])
        s //= 2
    return sets


def _ssm_kernel(u_ref, h0_ref, tab_ref, wx_ref, wxs_ref, tt_ref, vt_ref, y_ref, ht_ref, ug, xb, xsb, yg,
                *, batch, seq, pitch):
    m_blk = seq // SSM_BLOCK

    @pl.when(pl.program_id(0) == 0)
    def _():
        ug[...] = jnp.zeros(ug.shape, F32)

    for gl in range(SLAB_GROUPS):
        for half in range(2):
            for b in range(batch):
                ug[gl, half, b * pitch:b * pitch + m_blk, :] = u_ref[gl, half, b * m_blk:(b + 1) * m_blk, :]
        lhs = jnp.concatenate([ug[gl, 0], ug[gl, 1]], axis=1).astype(BF16)
        x = jnp.dot(lhs, wx_ref[gl], preferred_element_type=F32)
        xb[gl, 0] = x[:, :LANES]
        xb[gl, 1] = x[:, LANES:]
        x = jnp.dot(lhs, wxs_ref[gl], preferred_element_type=F32)
        xsb[gl, 0] = x[:, :LANES]
        xsb[gl, 1] = x[:, LANES:]
        y_in = lax.dot_general(lhs, tt_ref[gl], _NT, preferred_element_type=F32)
        yg[gl, 0] = y_in[:, :LANES]
        yg[gl, 1] = y_in[:, LANES:]

    n_bt = -(-batch // SUBLANES)
    rows_per = min(batch, SUBLANES)
    per_pass = max(1, 4 // n_bt)
    for g0 in range(0, SLAB_GROUPS, per_pass):
        keys = [(gl, d, bt) for gl in range(g0, g0 + per_pass) for d in range(2) for bt in range(n_bt)]
        lane0 = lambda gl, d: (gl * 2 + d) * LANES
        init = []
        for gl, d, bt in keys:
            h = h0_ref[bt * SUBLANES:bt * SUBLANES + rows_per, lane0(gl, d):lane0(gl, d) + LANES]
            init += [h, pltpu.roll(h, SSM_STATE, axis=1)]
        mult = {(gl, d): (tab_ref[gl, d, 0, 0:rows_per, :], tab_ref[gl, d, 1, 0:rows_per, :])
                for gl, d, _ in keys}

        def step(m, hs, keys=keys, mult=mult):
            out = []
            for k, (gl, d, bt) in enumerate(keys):
                h, h_sw = hs[2 * k], hs[2 * k + 1]
                row = m if d == 0 else m_blk - 1 - m
                idx = pl.ds(bt * SUBLANES * pitch + row, rows_per, stride=pitch)
                x = xb[gl, d, idx, :]
                x_sw = xsb[gl, d, idx, :]
                ug[gl, d, idx, :] = h
                a, b = mult[(gl, d)]
                out += [a * h + b * h_sw + x, a * h_sw - b * h + x_sw]
            return tuple(out)

        final = lax.fori_loop(0, m_blk, step, tuple(init), unroll=2)
        for k, (gl, d, bt) in enumerate(keys):
            ht_ref[bt * SUBLANES:bt * SUBLANES + rows_per, lane0(gl, d):lane0(gl, d) + LANES] = final[2 * k]

    for gl in range(SLAB_GROUPS):
        states = jnp.concatenate([ug[gl, 0], ug[gl, 1]], axis=1).astype(BF16)
        y_st = lax.dot_general(states, vt_ref[gl], _NT, preferred_element_type=F32)
        for half in range(2):
            total = yg[gl, half] + y_st[:, half * LANES:(half + 1) * LANES]
            for b in range(batch):
                y_ref[gl, half, b * m_blk:(b + 1) * m_blk, :] = total[b * pitch:b * pitch + m_blk]


def _ssm_call(u_blk, h0, tab, wx, wxs, tt, vt, layer, batch, seq):
    n_slab = u_blk.shape[0]
    m_blk = seq // SSM_BLOCK
    blk_spec = pl.BlockSpec((None, SLAB_GROUPS, 2, batch * m_blk, LANES), lambda s: (s, 0, 0, 0, 0))
    pitch = m_blk + SUBLANES
    rows_p = batch * pitch
    st_lanes = SLAB_GROUPS * 2 * LANES
    op_spec = pl.BlockSpec((None, SLAB_GROUPS, MXU_DIM, MXU_DIM), lambda s: (layer, s, 0, 0))
    scratch = pltpu.VMEM((SLAB_GROUPS, 2, rows_p, LANES), F32)
    return pl.pallas_call(
        functools.partial(_ssm_kernel, batch=batch, seq=seq, pitch=pitch),
        grid=(n_slab,),
        in_specs=[
            blk_spec,
            pl.BlockSpec((batch, st_lanes), lambda s: (0, s)),
            pl.BlockSpec((None, SLAB_GROUPS, 2, 2, SUBLANES, LANES), lambda s: (layer, s, 0, 0, 0, 0)),
            op_spec, op_spec, op_spec, op_spec,
        ],
        out_specs=(blk_spec,
                   pl.BlockSpec((batch, st_lanes), lambda s: (0, s))),
        out_shape=(jax.ShapeDtypeStruct((n_slab, SLAB_GROUPS, 2, batch * m_blk, LANES), F32),
                   jax.ShapeDtypeStruct((batch, n_slab * st_lanes), F32)),
        scratch_shapes=[scratch, scratch, scratch, scratch],
        compiler_params=pltpu.CompilerParams(
            dimension_semantics=("arbitrary",), vmem_limit_bytes=VMEM_LIMIT),
        name=f"ssm_scan_b{batch}",
    )(u_blk, h0, tab, wx, wxs, tt, vt)


def _tokens_from_blocks(yb_ref, rows_ref, slot):
    n_slab, _, _, n_blk, _ = yb_ref.shape
    chunk = 2 * SUBLANES
    piece = lax.broadcasted_iota(jnp.int32, (chunk, LANES), 1) // SSM_GROUP
    for s in range(n_slab):
        sets, dests = [], []
        for c in range(n_blk // chunk):
            for half in range(2):
                sets.append([yb_ref[s, gl, half, c * chunk:(c + 1) * chunk, :] for gl in range(SLAB_GROUPS)])
                dests.append(c * chunk * SSM_BLOCK + half * SLAB_GROUPS)
        for tok, vs in zip(dests, _transpose_pieces(sets, piece)):
            for rr, v in enumerate(vs):
                rows_ref[slot, s, pl.ds(tok + rr, chunk, stride=SSM_BLOCK), :] = v


def _mix_ffn_kernel(attn_ref, yb_first_ref, yb_next_ref, u_ref, x_ref, mod_ref, dskip_ref, wglu_ref,
                    bglu_ref, wout_ref, g2_ref, wffi_ref, wffo_ref, fn_ref, o_ref, y_rows,
                    *, final, ff_chunks):
    d_ff = wffo_ref.shape[0]
    i = pl.program_id(0)
    slot = i % 2

    @pl.when(i == 0)
    def _():
        _tokens_from_blocks(yb_first_ref, y_rows, 0)

    y_ssm = jnp.concatenate([y_rows[slot, s] for s in range(y_rows.shape[1])], axis=1)
    _tokens_from_blocks(yb_next_ref, y_rows, 1 - slot)
    y = y_ssm + dskip_ref[...] * u_ref[...]
    g = 0.5 * y * (1.0 + lax.erf(y * (2.0 ** -0.5)))
    z = jnp.dot(g.astype(BF16), wglu_ref[...], preferred_element_type=F32) + bglu_ref[...]
    ssm_out = g * jax.nn.sigmoid(z)
    mixed = (jnp.dot(attn_ref[...], wout_ref[0:D_ATTN, :], preferred_element_type=F32)
             + jnp.dot(ssm_out.astype(BF16), wout_ref[D_ATTN:, :], preferred_element_type=F32))
    mod = mod_ref[0]
    x1 = x_ref[...] + mod[2:3] * mixed
    h2 = (_rms(x1, g2_ref[...]) * (1.0 + mod[4:5]) + mod[3:4]).astype(BF16)
    tiles = d_ff // MXU_DIM
    edges = [MXU_DIM * ((tiles * ci) // ff_chunks) for ci in range(ff_chunks)] + [d_ff]
    acc = None
    for c0, c1 in zip(edges[:-1], edges[1:]):
        gate = jnp.dot(h2, wffi_ref[:, c0:c1], preferred_element_type=F32)
        up = jnp.dot(h2, wffi_ref[:, d_ff + c0:d_ff + c1], preferred_element_type=F32)
        act = (gate * jax.nn.sigmoid(gate) * up).astype(BF16)
        part = jnp.dot(act, wffo_ref[c0:c1, :], preferred_element_type=F32)
        acc = part if acc is None else acc + part
    x2 = x1 + mod[5:6] * acc
    if final:
        x2 = _rms(x2, fn_ref[...])
    o_ref[...] = x2


def _mix_ffn_call(attn, y, u, x2d, mods, dskip, wglu_b, bglu, wout_b, g2, wffi_b, wffo_b, fnorm,
                  layer, mod_row, tokens_per_batch, tile, final):
    n_tok, d_model = x2d.shape
    d_ssm = u.shape[1]
    d_ff = wffo_b.shape[-2]
    tiles_per_batch = None if tokens_per_batch is None else tokens_per_batch // tile
    row = lambda width: pl.BlockSpec((tile, width), lambda i: (i, 0))
    n_slab = y.shape[0]
    n_tiles = n_tok // tile
    y_block = (n_slab, SLAB_GROUPS, 2, tile // SSM_BLOCK, LANES)
    y_first = pl.BlockSpec(y_block, lambda i: (0, 0, 0, 0, 0))
    y_next = pl.BlockSpec(y_block, lambda i: (0, 0, 0, jnp.minimum(i + 1, n_tiles - 1), 0))
    return pl.pallas_call(
        functools.partial(_mix_ffn_kernel, final=final, ff_chunks=2),
        grid=(n_tiles,),
        in_specs=[
            row(D_ATTN), y_first, y_next, row(d_ssm), row(d_model),
            _mod_spec(d_model, layer, mod_row, tiles_per_batch),
            _layer_spec((1, d_ssm), layer),
            _layer_spec((d_ssm, d_ssm), layer),
            _layer_spec((1, d_ssm), layer),
            _layer_spec((D_ATTN + d_ssm, d_model), layer),
            _layer_spec((1, d_model), layer),
            _layer_spec((d_model, 2 * d_ff), layer),
            _layer_spec((d_ff, d_model), layer),
            _const_spec((1, d_model)),
        ],
        out_specs=row(d_model),
        out_shape=jax.ShapeDtypeStruct((n_tok, d_model), F32),
        scratch_shapes=[pltpu.VMEM((2, n_slab, tile, LANES), F32)],
        compiler_params=pltpu.CompilerParams(
            dimension_semantics=("arbitrary",), vmem_limit_bytes=VMEM_LIMIT),
        name="mix_ffn_final" if final else "mix_ffn",
    )(attn, y, y, u, x2d, mods, dskip, wglu_b, bglu, wout_b, g2, wffi_b, wffo_b, fnorm)


def _rope_tables(n_tokens):
    axis_dim = HEAD_DIM // 2
    rows = n_tokens // GRID_W
    row = jnp.repeat(jnp.arange(rows, dtype=F32), GRID_W)
    col = jnp.tile(jnp.arange(GRID_W, dtype=F32), rows)
    inv_freq = ROPE_THETA ** (-jnp.arange(0, axis_dim, 2, dtype=F32) / axis_dim)
    ang = jnp.concatenate([row[:, None] * inv_freq, col[:, None] * inv_freq], axis=-1)
    cos = jnp.repeat(jnp.cos(ang), 2, axis=-1)
    sin = jnp.repeat(jnp.sin(ang), 2, axis=-1) * jnp.tile(jnp.array([-1.0, 1.0], F32), HEAD_DIM // 2)
    return jnp.tile(cos, (1, N_Q_HEADS)), jnp.tile(sin, (1, N_Q_HEADS))


def _states_to_lanes(st):
    return st.transpose(0, 3, 1, 2, 4).reshape(st.shape[0], -1)


def _lanes_to_states(rows, n_groups):
    return rows.reshape(rows.shape[0], n_groups, 2, 2, SSM_STATE).transpose(0, 2, 3, 1, 4)


def kernel(x_prompt, x_sample, cache_k, cache_v, state_ssm, c, c_ctx, w_mod, b_mod, norm1, norm2, w_in, q_norm, k_norm, ssm_a_re, ssm_a_im, ssm_log_dt, ssm_b_re, ssm_b_im, ssm_c_re, ssm_c_im, ssm_d, w_glu, b_glu, w_out, w_ffn_in, w_ffn_out, final_norm):
    batch, seq, d_model = x_prompt.shape
    dec_batch, dec_seq, _ = x_sample.shape
    depth = w_in.shape[0]
    past = cache_k.shape[2]
    n_groups = ssm_a_re.shape[2]
    d_ssm = n_groups * SSM_GROUP

    cond = jnp.zeros((SUBLANES, d_model), F32).at[0].set(c_ctx).at[1:1 + dec_batch].set(c)
    mods = _mods_call(cond, w_mod, b_mod).reshape(depth, SUBLANES, N_MOD, d_model)
    tab, wx, wxs, tt, vt = _ssm_prep_call(ssm_a_re, ssm_a_im, ssm_log_dt, ssm_b_re, ssm_b_im,
                                          ssm_c_re, ssm_c_im)
    rope_tabs = _rope_tables(dec_seq)
    head_ids = jnp.arange(D_ATTN) // HEAD_DIM
    ones_bd = (head_ids[:, None] == head_ids[None, :]).astype(BF16)

    xp = x_prompt.reshape(batch * seq, d_model)
    xs = x_sample.reshape(dec_batch * dec_seq, d_model)
    zero_state = jnp.zeros((batch, n_groups * 2 * LANES), F32)
    w_in_b = w_in.astype(BF16)
    wglu_b = w_glu.astype(BF16)
    wout_b = w_out.astype(BF16)
    wffi_b = w_ffn_in.astype(BF16)
    wffo_b = w_ffn_out.astype(BF16)
    g1 = norm1.reshape(depth, 1, d_model)
    g2 = norm2.reshape(depth, 1, d_model)
    qg = jnp.tile(q_norm, (1, N_Q_HEADS)).reshape(depth, 1, D_ATTN)
    kg = jnp.tile(k_norm, (1, N_KV_HEADS)).reshape(depth, 1, D_KV)
    dskip = ssm_d.reshape(depth, 1, d_ssm)
    bglu = b_glu.reshape(depth, 1, d_ssm)
    fnorm = final_norm.reshape(1, d_model)
    new_k, new_v, new_s = [], [], []
    for l in range(depth):
        final = l == depth - 1
        for is_ctx in (True, False):
            if is_ctx:
                x2d, n_b, n_l, tq, attn_nb = xp, batch, seq, seq, 4
                mod_row, tokens_per_batch, tabs, ck, cv, h0 = 0, None, None, None, None, zero_state
            else:
                x2d, n_b, n_l, tq, attn_nb = xs, dec_batch, dec_seq, 512, 1
                mod_row, tokens_per_batch, tabs = 1, dec_seq, rope_tabs
                ck = cache_k[:, l].reshape(dec_batch, past, D_KV)
                cv = cache_v[:, l].reshape(dec_batch, past, D_KV)
                h0 = _states_to_lanes(state_ssm[:, l])
            q, k, v, u, u_blk = _inproj_call(x2d, mods, g1, w_in_b, qg, kg, ones_bd, tabs, l, mod_row,
                                             tokens_per_batch, TOKEN_TILE)
            attn = _attn_call(q.reshape(n_b, n_l, D_ATTN), k.reshape(n_b, n_l, D_KV),
                              v.reshape(n_b, n_l, D_KV), ck, cv, tq, attn_nb)
            y, ht = _ssm_call(u_blk, h0, tab, wx, wxs, tt, vt, l, n_b, n_l)
            x_new = _mix_ffn_call(attn.reshape(-1, D_ATTN), y, u, x2d, mods, dskip, wglu_b, bglu, wout_b,
                                  g2, wffi_b, wffo_b, fnorm, l, mod_row, tokens_per_batch, TOKEN_TILE,
                                  final)
            if is_ctx:
                xp = x_new
                new_k.append(k.reshape(batch, seq, N_KV_HEADS, HEAD_DIM))
                new_v.append(v.reshape(batch, seq, N_KV_HEADS, HEAD_DIM))
                new_s.append(_lanes_to_states(ht, n_groups))
            else:
                xs = x_new
    return (xp.reshape(batch, seq, d_model), xs.reshape(dec_batch, dec_seq, d_model),
            jnp.stack(new_k, axis=1), jnp.stack(new_v, axis=1), jnp.stack(new_s, axis=1))
```

```python
import functools

import jax
import jax.numpy as jnp
from jax import lax
from jax.experimental import pallas as pl
from jax.experimental.pallas import tpu as pltpu

F32 = jnp.float32
BF16 = jnp.bfloat16

HEAD_DIM = 64
N_Q_HEADS = 8
N_KV_HEADS = 2
D_ATTN = N_Q_HEADS * HEAD_DIM
D_KV = N_KV_HEADS * HEAD_DIM
SSM_GROUP = 16
SSM_STATE = 64
GRID_W = 64
ROPE_THETA = 10000.0
N_MOD = 6
EPS = 1e-6

LANES = 128
SUBLANES = 8
MXU_DIM = 256
SSM_BLOCK = MXU_DIM // SSM_GROUP
SLAB_GROUPS = LANES // SSM_GROUP
VMEM_LIMIT = 56 * 1024 * 1024
TOKEN_TILE = 512

_NT = (((1,), (1,)), ((), ()))


def _const_spec(shape):
    nd = len(shape)
    return pl.BlockSpec(shape, lambda *_: (0,) * nd, pipeline_mode=pl.Buffered(1))


def _layer_spec(shape, layer):
    nd = len(shape)
    return pl.BlockSpec((None,) + shape, lambda *_: (layer,) + (0,) * nd,
                        pipeline_mode=pl.Buffered(1))


def _mod_spec(d_model, layer, first_row, tiles_per_batch):
    if tiles_per_batch is None:
        return pl.BlockSpec((None, 1, N_MOD, d_model), lambda i: (layer, first_row, 0, 0))
    return pl.BlockSpec((None, 1, N_MOD, d_model),
                        lambda i: (layer, first_row + i // tiles_per_batch, 0, 0))


def _rms(x, gain):
    ms = jnp.mean(x * x, axis=-1, keepdims=True)
    return x * lax.rsqrt(ms + EPS) * gain


def _mods_kernel(cond_ref, w_ref, b_ref, o_ref):
    c = cond_ref[...]
    act = c * jax.nn.sigmoid(c)
    o_ref[0] = jnp.dot(act.astype(BF16), w_ref[0].astype(BF16), preferred_element_type=F32) + b_ref[0]


def _mods_call(cond, w_mod, b_mod):
    depth, d_model, n_out = w_mod.shape
    rows = cond.shape[0]
    nt = 4
    tn = n_out // nt
    return pl.pallas_call(
        _mods_kernel,
        grid=(depth, nt),
        in_specs=[
            pl.BlockSpec((rows, d_model), lambda l, j: (0, 0)),
            pl.BlockSpec((1, d_model, tn), lambda l, j: (l, 0, j)),
            pl.BlockSpec((1, 1, tn), lambda l, j: (l, 0, j)),
        ],
        out_specs=pl.BlockSpec((1, rows, tn), lambda l, j: (l, 0, j)),
        out_shape=jax.ShapeDtypeStruct((depth, rows, n_out), F32),
        compiler_params=pltpu.CompilerParams(
            dimension_semantics=("arbitrary", "arbitrary"), vmem_limit_bytes=VMEM_LIMIT),
        name="adaln_mods",
    )(cond, w_mod, b_mod.reshape(depth, 1, n_out))


def _group_sumsq(z, ones_ref, width):
    z2 = z * z
    hi = z2.astype(BF16)
    lo = (z2 - hi.astype(F32)).astype(BF16)
    step = min(width, MXU_DIM)
    ones = ones_ref[0:step, 0:step]
    cols = [jnp.dot(hi[:, c:c + step], ones, preferred_element_type=F32)
            + jnp.dot(lo[:, c:c + step], ones, preferred_element_type=F32)
            for c in range(0, width, step)]
    return cols[0] if len(cols) == 1 else jnp.concatenate(cols, axis=1)


def _head_rms(z, gain, ones_ref):
    ss = _group_sumsq(z, ones_ref, z.shape[-1])
    return z * lax.rsqrt(ss * (1.0 / HEAD_DIM) + EPS) * gain


def _rope(z, cos, sin_signed):
    width = z.shape[-1]
    lane = lax.broadcasted_iota(jnp.int32, z.shape, 1)
    nxt = pltpu.roll(z, width - 1, axis=1)
    prv = pltpu.roll(z, 1, axis=1)
    partner = jnp.where((lane & 1) == 0, nxt, prv)
    return z * cos + partner * sin_signed


def _blocks_from_tokens(u_rows, ub_ref):
    n_slab, _, _, n_blk, _ = ub_ref.shape
    chunk = 2 * SUBLANES
    piece = lax.broadcasted_iota(jnp.int32, (chunk, LANES), 1) // SSM_GROUP
    for s in range(n_slab):
        sets, dests = [], []
        for c in range(n_blk // chunk):
            for half in range(2):
                tok = c * chunk * SSM_BLOCK + half * SLAB_GROUPS
                sets.append([u_rows[s, pl.ds(tok + rr, chunk, stride=SSM_BLOCK), :]
                             for rr in range(SLAB_GROUPS)])
                dests.append((half, c))
        for (half, c), vs in zip(dests, _transpose_pieces(sets, piece)):
            for gl, v in enumerate(vs):
                ub_ref[s, gl, half, c * chunk:(c + 1) * chunk, :] = v


def _inproj_kernel(*refs, rope):
    if rope:
        (x_ref, mod_ref, g1_ref, w_ref, qg_ref, kg_ref, ones_ref, cos_ref, sin_ref,
         q_ref, k_ref, v_ref, u_ref, ub_ref, u_rows) = refs
    else:
        (x_ref, mod_ref, g1_ref, w_ref, qg_ref, kg_ref, ones_ref,
         q_ref, k_ref, v_ref, u_ref, ub_ref, u_rows) = refs
    mod = mod_ref[0]
    h = (_rms(x_ref[...], g1_ref[...]) * (1.0 + mod[1:2]) + mod[0:1]).astype(BF16)
    qkv_cols = D_ATTN + 2 * D_KV
    u = jnp.dot(h, w_ref[:, qkv_cols:], preferred_element_type=F32)
    u_ref[...] = u
    for s in range(u_rows.shape[0]):
        u_rows[s] = u[:, s * LANES:(s + 1) * LANES]
    _blocks_from_tokens(u_rows, ub_ref)
    proj = jnp.dot(h, w_ref[:, :qkv_cols], preferred_element_type=F32)
    q = _head_rms(proj[:, :D_ATTN], qg_ref[...], ones_ref)
    k = _head_rms(proj[:, D_ATTN:D_ATTN + D_KV], kg_ref[...], ones_ref)
    if rope:
        q = _rope(q, cos_ref[...], sin_ref[...])
        k = _rope(k, cos_ref[:, 0:D_KV], sin_ref[:, 0:D_KV])
    q_ref[...] = (q * (HEAD_DIM ** -0.5)).astype(BF16)
    k_ref[...] = k
    v_ref[...] = proj[:, D_ATTN + D_KV:qkv_cols]


def _inproj_call(x2d, mods, g1, w_in_b, qg, kg, ones_bd, rope_tabs, layer, mod_row, tokens_per_batch,
                 tile):
    n_tok, d_model = x2d.shape
    d_in = w_in_b.shape[-1]
    d_ssm = d_in - D_ATTN - 2 * D_KV
    n_slab = d_ssm // LANES
    tiles_per_batch = None if tokens_per_batch is None else tokens_per_batch // tile
    in_specs = [
        pl.BlockSpec((tile, d_model), lambda i: (i, 0)),
        _mod_spec(d_model, layer, mod_row, tiles_per_batch),
        _layer_spec((1, d_model), layer),
        _layer_spec((d_model, d_in), layer),
        _layer_spec((1, D_ATTN), layer),
        _layer_spec((1, D_KV), layer),
        _const_spec((D_ATTN, D_ATTN)),
    ]
    args = [x2d, mods, g1, w_in_b, qg, kg, ones_bd]
    if rope_tabs is not None:
        in_specs += [pl.BlockSpec((tile, D_ATTN), lambda i: (i % tiles_per_batch, 0))] * 2
        args += list(rope_tabs)
    row = lambda width: pl.BlockSpec((tile, width), lambda i: (i, 0))
    return pl.pallas_call(
        functools.partial(_inproj_kernel, rope=rope_tabs is not None),
        grid=(n_tok // tile,),
        in_specs=in_specs,
        out_specs=(row(D_ATTN), row(D_KV), row(D_KV), row(d_ssm),
                   pl.BlockSpec((n_slab, SLAB_GROUPS, 2, tile // SSM_BLOCK, LANES),
                                lambda i: (0, 0, 0, i, 0))),
        out_shape=(jax.ShapeDtypeStruct((n_tok, D_ATTN), BF16),
                   jax.ShapeDtypeStruct((n_tok, D_KV), F32),
                   jax.ShapeDtypeStruct((n_tok, D_KV), F32),
                   jax.ShapeDtypeStruct((n_tok, d_ssm), F32),
                   jax.ShapeDtypeStruct((n_slab, SLAB_GROUPS, 2, n_tok // SSM_BLOCK, LANES), F32)),
        scratch_shapes=[pltpu.VMEM((n_slab, tile, LANES), F32)],
        compiler_params=pltpu.CompilerParams(
            dimension_semantics=("arbitrary",), vmem_limit_bytes=VMEM_LIMIT),
        name="inproj_rope" if rope_tabs is not None else "inproj",
    )(*args)


def _attn_kernel(*refs, has_cache, nb):
    if has_cache:
        q_ref, kn_ref, vn_ref, ck_ref, cv_ref, o_ref, kvar, vvar = refs
    else:
        q_ref, kn_ref, vn_ref, o_ref, kvar, vvar = refs

    @pl.when(pl.program_id(1) == 0)
    def _():
        for bi in range(nb):
            for new_ref, cache_ref, dst in ((kn_ref, ck_ref if has_cache else None, kvar),
                                            (vn_ref, cv_ref if has_cache else None, vvar)):
                src = new_ref[bi]
                if has_cache:
                    src = jnp.concatenate([cache_ref[bi], src], axis=0)
                low = lax.broadcasted_iota(jnp.int32, src.shape, 1) < HEAD_DIM
                head0 = jnp.where(low, src, 0.0)
                head1 = jnp.where(low, 0.0, src)
                dst[4 * bi + 0] = head0.astype(BF16)
                dst[4 * bi + 1] = pltpu.roll(head0, HEAD_DIM, axis=1).astype(BF16)
                dst[4 * bi + 2] = pltpu.roll(head1, HEAD_DIM, axis=1).astype(BF16)
                dst[4 * bi + 3] = head1.astype(BF16)

    slabs = D_ATTN // LANES
    for bi in range(nb):
        for slab in range(slabs):
            qs = q_ref[bi, :, slab * LANES:(slab + 1) * LANES]
            kv = slab // (slabs // N_KV_HEADS)
            acc = None
            for half in range(2):
                idx = 4 * bi + kv * 2 + half
                s = lax.dot_general(qs, kvar[idx], _NT, preferred_element_type=F32)
                m = jnp.max(s, axis=-1, keepdims=True)
                p = jnp.exp(s - m)
                denom = jnp.sum(p, axis=-1, keepdims=True)
                o = jnp.dot(p.astype(BF16), vvar[idx], preferred_element_type=F32) / denom
                acc = o if acc is None else acc + o
            o_ref[bi, :, slab * LANES:(slab + 1) * LANES] = acc.astype(BF16)


def _attn_call(q3, k3, v3, ck3, cv3, tq, nb):
    b, l, _ = q3.shape
    has_cache = ck3 is not None
    s_len = l + (ck3.shape[1] if has_cache else 0)
    in_specs = [
        pl.BlockSpec((nb, tq, D_ATTN), lambda bi, qi: (bi, qi, 0)),
        pl.BlockSpec((nb, l, D_KV), lambda bi, qi: (bi, 0, 0)),
        pl.BlockSpec((nb, l, D_KV), lambda bi, qi: (bi, 0, 0)),
    ]
    args = [q3, k3, v3]
    if has_cache:
        in_specs += [pl.BlockSpec((nb, ck3.shape[1], D_KV), lambda bi, qi: (bi, 0, 0))] * 2
        args += [ck3, cv3]
    return pl.pallas_call(
        functools.partial(_attn_kernel, has_cache=has_cache, nb=nb),
        grid=(b // nb, l // tq),
        in_specs=in_specs,
        out_specs=pl.BlockSpec((nb, tq, D_ATTN), lambda bi, qi: (bi, qi, 0)),
        out_shape=jax.ShapeDtypeStruct((b, l, D_ATTN), BF16),
        scratch_shapes=[pltpu.VMEM((4 * nb, s_len, D_KV), BF16), pltpu.VMEM((4 * nb, s_len, D_KV), BF16)],
        compiler_params=pltpu.CompilerParams(
            dimension_semantics=("arbitrary", "arbitrary"), vmem_limit_bytes=VMEM_LIMIT),
        name="attn_cache" if has_cache else "attn",
    )(*args)


def _dot3(a, b):
    a_hi = a.astype(BF16)
    b_hi = b.astype(BF16)
    a_lo = (a - a_hi.astype(F32)).astype(BF16)
    b_lo = (b - b_hi.astype(F32)).astype(BF16)
    dot = lambda x, y: lax.dot_general(x, y, _NT, preferred_element_type=F32)
    return dot(a_hi, b_hi) + (dot(a_hi, b_lo) + dot(a_lo, b_hi))


def _ssm_prep_kernel(are_ref, aim_ref, ldt_ref, bre_ref, bim_ref, cre_ref, cim_ref,
                     tab_ref, wx_ref, wxs_ref, vt_ref, tt_ref, clm, lbm):
    n_groups = wx_ref.shape[0]
    width = SSM_BLOCK * SSM_GROUP
    shape = (SSM_GROUP, LANES)
    low = lax.broadcasted_iota(jnp.int32, shape, 1) < SSM_STATE
    row_blk = lax.broadcasted_iota(jnp.int32, (width, width), 0) // SSM_GROUP
    col_blk = lax.broadcasted_iota(jnp.int32, (width, width), 1) // SSM_GROUP

    def cmul(xr, xi, yr, yi):
        return xr * yr - xi * yi, xr * yi + xi * yr

    def group(g, carry):
        rows = pl.ds(pl.multiple_of(g * SSM_GROUP, SSM_GROUP), SSM_GROUP)
        tile = None
        for d in range(2):
            lanes = slice(d * LANES, (d + 1) * LANES)
            dt = jnp.exp(ldt_ref[d, rows, :])
            ar = are_ref[d, rows, :]
            ai = aim_ref[d, rows, :]
            mag = jnp.exp(ar * dt)
            lr = mag * jnp.cos(ai * dt)
            li = mag * jnp.sin(ai * dt)
            den = ar * ar + ai * ai
            zr = ((lr - 1.0) * ar + li * ai) / den
            zi = (li * ar - (lr - 1.0) * ai) / den
            bbr, bbi = cmul(zr, zi, bre_ref[d, rows, :], bim_ref[d, rows, :])
            cr = cre_ref[d, rows, :]
            ci = cim_ref[d, rows, :]
            inv = 1.0 / (lr * lr + li * li)
            ir = lr * inv
            ii = -li * inv
            pr, pi = jnp.ones(shape, F32), jnp.zeros(shape, F32)
            qr, qi = pr, pi
            for j in range(SSM_BLOCK + 1):
                c_re, c_im = cmul(cr, ci, pr, pi)
                c_pos = jnp.where(low, c_re, -c_im)
                if j >= 1:
                    vt_ref[g, (j - 1) if d == 0 else (SSM_BLOCK - j), :, lanes] = c_pos.astype(BF16)
                if j < SSM_BLOCK:
                    g_re, g_im = cmul(pr, pi, bbr, bbi)
                    g_pos = jnp.where(low, g_re, g_im)
                    r = (SSM_BLOCK - 1 - j) if d == 0 else j
                    wx_ref[g, r, :, lanes] = g_pos.astype(BF16)
                    wxs_ref[g, r, :, lanes] = jnp.where(low, g_im, g_re).astype(BF16)
                    blk = pl.ds(j * SSM_GROUP, SSM_GROUP)
                    if d == 0:
                        n_re, n_im = cmul(qr, qi, bbr, bbi)
                        clm[blk, :] = c_pos
                        lbm[blk, :] = jnp.where(low, n_re, n_im)
                    else:
                        m_re, m_im = cmul(cr, ci, qr, qi)
                        clm[blk, :] = jnp.where(low, m_re, -m_im)
                        lbm[blk, :] = g_pos
                else:
                    tab_ref[g, d, 0] = pr[0:SUBLANES]
                    tab_ref[g, d, 1] = jnp.where(low, -pi, pi)[0:SUBLANES]
                pr, pi = cmul(pr, pi, lr, li)
                qr, qi = cmul(qr, qi, ir, ii)
            t = _dot3(clm[...], lbm[...])
            t = jnp.where((col_blk <= row_blk) if d == 0 else (col_blk >= row_blk), t, 0.0)
            tile = t if tile is None else tile + t
        tt_ref[g] = tile.astype(BF16)
        return carry

    lax.fori_loop(0, n_groups, group, 0)


def _ssm_prep_call(a_re, a_im, log_dt, b_re, b_im, c_re, c_im):
    depth, ndir, g, p, h = b_re.shape
    halves = 2
    g_half = g // halves
    rows = g_half * h
    width = SSM_BLOCK * h

    def rows_lanes(a):
        a = a.astype(F32).reshape(depth, ndir, g * h, p)
        return jnp.concatenate([a, a], axis=-1)

    per_state = lambda a: rows_lanes(jnp.broadcast_to(a[:, :, :, None, :], (depth, ndir, g, h, p)))
    ldt = rows_lanes(jnp.broadcast_to(log_dt[:, :, :, None, None], (depth, ndir, g, h, p)))
    to_hp = lambda a: rows_lanes(jnp.swapaxes(a, -1, -2))
    in_spec = pl.BlockSpec((None, ndir, rows, LANES), lambda l, s: (l, 0, s, 0))
    out_blk = lambda *shape: pl.BlockSpec((None, g_half) + shape, lambda l, s: (l, s) + (0,) * len(shape))
    op4 = jax.ShapeDtypeStruct((depth, g, SSM_BLOCK, h, 2 * LANES), BF16)
    tab, wx, wxs, vt, tt = pl.pallas_call(
        _ssm_prep_kernel,
        grid=(depth, halves),
        in_specs=[in_spec] * 7,
        out_specs=(out_blk(2, 2, SUBLANES, LANES), out_blk(SSM_BLOCK, h, 2 * LANES),
                   out_blk(SSM_BLOCK, h, 2 * LANES), out_blk(SSM_BLOCK, h, 2 * LANES),
                   out_blk(width, width)),
        out_shape=(jax.ShapeDtypeStruct((depth, g, 2, 2, SUBLANES, LANES), F32), op4, op4, op4,
                   jax.ShapeDtypeStruct((depth, g, width, width), BF16)),
        scratch_shapes=[pltpu.VMEM((width, LANES), F32), pltpu.VMEM((width, LANES), F32)],
        compiler_params=pltpu.CompilerParams(
            dimension_semantics=("arbitrary", "arbitrary"), vmem_limit_bytes=VMEM_LIMIT),
        name="ssm_prepare",
    )(per_state(a_re), per_state(a_im), ldt, to_hp(b_re), to_hp(b_im), rows_lanes(c_re), rows_lanes(c_im))
    merge = lambda a: a.reshape(depth, g, width, 2 * LANES)
    return tab, merge(wx), merge(wxs), tt, merge(vt)


def _transpose_pieces(sets, piece):
    sets = [list(vs) for vs in sets]
    n = len(sets[0])
    s = n // 2
    while s:
        keep_low = (piece & s) == 0
        pairs = [(vs, i) for vs in sets for i in range(n) if not i & s]
        moved = [(pltpu.roll(vs[i + s], s * SSM_GROUP, axis=1),
                  pltpu.roll(vs[i], LANES - s * SSM_GROUP, axis=1)) for vs, i in pairs]
        for (vs, i), (from_hi, from_lo) in zip(pairs, moved):
            vs[i], vs[i + s] = jnp.where(keep_low, vs[i], from_hi), jnp.where(keep_low, from_lo, vs[i + s])
        s //= 2
    return sets


def _ssm_kernel(u_ref, h0_ref, tab_ref, wx_ref, wxs_ref, tt_ref, vt_ref, y_ref, ht_ref, ug, xb, xsb, yg,
                *, batch, seq, pitch):
    m_blk = seq // SSM_BLOCK

    @pl.when(pl.program_id(0) == 0)
    def _():
        ug[...] = jnp.zeros(ug.shape, F32)

    for gl in range(SLAB_GROUPS):
        for half in range(2):
            for b in range(batch):
                ug[gl, half, b * pitch:b * pitch + m_blk, :] = u_ref[gl, half, b * m_blk:(b + 1) * m_blk, :]
        lhs = jnp.concatenate([ug[gl, 0], ug[gl, 1]], axis=1).astype(BF16)
        x = jnp.dot(lhs, wx_ref[gl], preferred_element_type=F32)
        xb[gl, 0] = x[:, :LANES]
        xb[gl, 1] = x[:, LANES:]
        x = jnp.dot(lhs, wxs_ref[gl], preferred_element_type=F32)
        xsb[gl, 0] = x[:, :LANES]
        xsb[gl, 1] = x[:, LANES:]
        y_in = lax.dot_general(lhs, tt_ref[gl], _NT, preferred_element_type=F32)
        yg[gl, 0] = y_in[:, :LANES]
        yg[gl, 1] = y_in[:, LANES:]

    n_bt = -(-batch // SUBLANES)
    rows_per = min(batch, SUBLANES)
    per_pass = max(1, 4 // n_bt)
    for g0 in range(0, SLAB_GROUPS, per_pass):
        keys = [(gl, d, bt) for gl in range(g0, g0 + per_pass) for d in range(2) for bt in range(n_bt)]
        lane0 = lambda gl, d: (gl * 2 + d) * LANES
        init = []
        for gl, d, bt in keys:
            h = h0_ref[bt * SUBLANES:bt * SUBLANES + rows_per, lane0(gl, d):lane0(gl, d) + LANES]
            init += [h, pltpu.roll(h, SSM_STATE, axis=1)]
        mult = {(gl, d): (tab_ref[gl, d, 0, 0:rows_per, :], tab_ref[gl, d, 1, 0:rows_per, :])
                for gl, d, _ in keys}

        def step(m, hs, keys=keys, mult=mult):
            out = []
            for k, (gl, d, bt) in enumerate(keys):
                h, h_sw = hs[2 * k], hs[2 * k + 1]
                row = m if d == 0 else m_blk - 1 - m
                idx = pl.ds(bt * SUBLANES * pitch + row, rows_per, stride=pitch)
                x = xb[gl, d, idx, :]
                x_sw = xsb[gl, d, idx, :]
                ug[gl, d, idx, :] = h
                a, b = mult[(gl, d)]
                out += [a * h + b * h_sw + x, a * h_sw - b * h + x_sw]
            return tuple(out)

        final = lax.fori_loop(0, m_blk, step, tuple(init), unroll=2)
        for k, (gl, d, bt) in enumerate(keys):
            ht_ref[bt * SUBLANES:bt * SUBLANES + rows_per, lane0(gl, d):lane0(gl, d) + LANES] = final[2 * k]

    for gl in range(SLAB_GROUPS):
        states = jnp.concatenate([ug[gl, 0], ug[gl, 1]], axis=1).astype(BF16)
        y_st = lax.dot_general(states, vt_ref[gl], _NT, preferred_element_type=F32)
        for half in range(2):
            total = yg[gl, half] + y_st[:, half * LANES:(half + 1) * LANES]
            for b in range(batch):
                y_ref[gl, half, b * m_blk:(b + 1) * m_blk, :] = total[b * pitch:b * pitch + m_blk]


def _ssm_call(u_blk, h0, tab, wx, wxs, tt, vt, layer, batch, seq):
    n_slab = u_blk.shape[0]
    m_blk = seq // SSM_BLOCK
    blk_spec = pl.BlockSpec((None, SLAB_GROUPS, 2, batch * m_blk, LANES), lambda s: (s, 0, 0, 0, 0))
    pitch = m_blk + SUBLANES
    rows_p = batch * pitch
    st_lanes = SLAB_GROUPS * 2 * LANES
    op_spec = pl.BlockSpec((None, SLAB_GROUPS, MXU_DIM, MXU_DIM), lambda s: (layer, s, 0, 0))
    scratch = pltpu.VMEM((SLAB_GROUPS, 2, rows_p, LANES), F32)
    return pl.pallas_call(
        functools.partial(_ssm_kernel, batch=batch, seq=seq, pitch=pitch),
        grid=(n_slab,),
        in_specs=[
            blk_spec,
            pl.BlockSpec((batch, st_lanes), lambda s: (0, s)),
            pl.BlockSpec((None, SLAB_GROUPS, 2, 2, SUBLANES, LANES), lambda s: (layer, s, 0, 0, 0, 0)),
            op_spec, op_spec, op_spec, op_spec,
        ],
        out_specs=(blk_spec,
                   pl.BlockSpec((batch, st_lanes), lambda s: (0, s))),
        out_shape=(jax.ShapeDtypeStruct((n_slab, SLAB_GROUPS, 2, batch * m_blk, LANES), F32),
                   jax.ShapeDtypeStruct((batch, n_slab * st_lanes), F32)),
        scratch_shapes=[scratch, scratch, scratch, scratch],
        compiler_params=pltpu.CompilerParams(
            dimension_semantics=("arbitrary",), vmem_limit_bytes=VMEM_LIMIT),
        name=f"ssm_scan_b{batch}",
    )(u_blk, h0, tab, wx, wxs, tt, vt)


def _tokens_from_blocks(yb_ref, rows_ref, slot):
    n_slab, _, _, n_blk, _ = yb_ref.shape
    chunk = 2 * SUBLANES
    piece = lax.broadcasted_iota(jnp.int32, (chunk, LANES), 1) // SSM_GROUP
    for s in range(n_slab):
        sets, dests = [], []
        for c in range(n_blk // chunk):
            for half in range(2):
                sets.append([yb_ref[s, gl, half, c * chunk:(c + 1) * chunk, :] for gl in range(SLAB_GROUPS)])
                dests.append(c * chunk * SSM_BLOCK + half * SLAB_GROUPS)
        for tok, vs in zip(dests, _transpose_pieces(sets, piece)):
            for rr, v in enumerate(vs):
                rows_ref[slot, s, pl.ds(tok + rr, chunk, stride=SSM_BLOCK), :] = v


def _mix_ffn_kernel(attn_ref, yb_first_ref, yb_next_ref, u_ref, x_ref, mod_ref, dskip_ref, wglu_ref,
                    bglu_ref, wout_ref, g2_ref, wffi_ref, wffo_ref, fn_ref, o_ref, y_rows,
                    *, final, ff_chunks):
    d_ff = wffo_ref.shape[0]
    i = pl.program_id(0)
    slot = i % 2

    @pl.when(i == 0)
    def _():
        _tokens_from_blocks(yb_first_ref, y_rows, 0)

    y_ssm = jnp.concatenate([y_rows[slot, s] for s in range(y_rows.shape[1])], axis=1)
    _tokens_from_blocks(yb_next_ref, y_rows, 1 - slot)
    y = y_ssm + dskip_ref[...] * u_ref[...]
    g = 0.5 * y * (1.0 + lax.erf(y * (2.0 ** -0.5)))
    z = jnp.dot(g.astype(BF16), wglu_ref[...], preferred_element_type=F32) + bglu_ref[...]
    ssm_out = g * jax.nn.sigmoid(z)
    mixed = (jnp.dot(attn_ref[...], wout_ref[0:D_ATTN, :], preferred_element_type=F32)
             + jnp.dot(ssm_out.astype(BF16), wout_ref[D_ATTN:, :], preferred_element_type=F32))
    mod = mod_ref[0]
    x1 = x_ref[...] + mod[2:3] * mixed
    h2 = (_rms(x1, g2_ref[...]) * (1.0 + mod[4:5]) + mod[3:4]).astype(BF16)
    tiles = d_ff // MXU_DIM
    edges = [MXU_DIM * ((tiles * ci) // ff_chunks) for ci in range(ff_chunks)] + [d_ff]
    acc = None
    for c0, c1 in zip(edges[:-1], edges[1:]):
        gate = jnp.dot(h2, wffi_ref[:, c0:c1], preferred_element_type=F32)
        up = jnp.dot(h2, wffi_ref[:, d_ff + c0:d_ff + c1], preferred_element_type=F32)
        act = (gate * jax.nn.sigmoid(gate) * up).astype(BF16)
        part = jnp.dot(act, wffo_ref[c0:c1, :], preferred_element_type=F32)
        acc = part if acc is None else acc + part
    x2 = x1 + mod[5:6] * acc
    if final:
        x2 = _rms(x2, fn_ref[...])
    o_ref[...] = x2


def _mix_ffn_call(attn, y, u, x2d, mods, dskip, wglu_b, bglu, wout_b, g2, wffi_b, wffo_b, fnorm,
                  layer, mod_row, tokens_per_batch, tile, final):
    n_tok, d_model = x2d.shape
    d_ssm = u.shape[1]
    d_ff = wffo_b.shape[-2]
    tiles_per_batch = None if tokens_per_batch is None else tokens_per_batch // tile
    row = lambda width: pl.BlockSpec((tile, width), lambda i: (i, 0))
    n_slab = y.shape[0]
    n_tiles = n_tok // tile
    y_block = (n_slab, SLAB_GROUPS, 2, tile // SSM_BLOCK, LANES)
    y_first = pl.BlockSpec(y_block, lambda i: (0, 0, 0, 0, 0))
    y_next = pl.BlockSpec(y_block, lambda i: (0, 0, 0, jnp.minimum(i + 1, n_tiles - 1), 0))
    return pl.pallas_call(
        functools.partial(_mix_ffn_kernel, final=final, ff_chunks=2),
        grid=(n_tiles,),
        in_specs=[
            row(D_ATTN), y_first, y_next, row(d_ssm), row(d_model),
            _mod_spec(d_model, layer, mod_row, tiles_per_batch),
            _layer_spec((1, d_ssm), layer),
            _layer_spec((d_ssm, d_ssm), layer),
            _layer_spec((1, d_ssm), layer),
            _layer_spec((D_ATTN + d_ssm, d_model), layer),
            _layer_spec((1, d_model), layer),
            _layer_spec((d_model, 2 * d_ff), layer),
            _layer_spec((d_ff, d_model), layer),
            _const_spec((1, d_model)),
        ],
        out_specs=row(d_model),
        out_shape=jax.ShapeDtypeStruct((n_tok, d_model), F32),
        scratch_shapes=[pltpu.VMEM((2, n_slab, tile, LANES), F32)],
        compiler_params=pltpu.CompilerParams(
            dimension_semantics=("arbitrary",), vmem_limit_bytes=VMEM_LIMIT),
        name="mix_ffn_final" if final else "mix_ffn",
    )(attn, y, y, u, x2d, mods, dskip, wglu_b, bglu, wout_b, g2, wffi_b, wffo_b, fnorm)


def _rope_tables(n_tokens):
    axis_dim = HEAD_DIM // 2
    rows = n_tokens // GRID_W
    row = jnp.repeat(jnp.arange(rows, dtype=F32), GRID_W)
    col = jnp.tile(jnp.arange(GRID_W, dtype=F32), rows)
    inv_freq = ROPE_THETA ** (-jnp.arange(0, axis_dim, 2, dtype=F32) / axis_dim)
    ang = jnp.concatenate([row[:, None] * inv_freq, col[:, None] * inv_freq], axis=-1)
    cos = jnp.repeat(jnp.cos(ang), 2, axis=-1)
    sin = jnp.repeat(jnp.sin(ang), 2, axis=-1) * jnp.tile(jnp.array([-1.0, 1.0], F32), HEAD_DIM // 2)
    return jnp.tile(cos, (1, N_Q_HEADS)), jnp.tile(sin, (1, N_Q_HEADS))


def _states_to_lanes(st):
    return st.transpose(0, 3, 1, 2, 4).reshape(st.shape[0], -1)


def _lanes_to_states(rows, n_groups):
    return rows.reshape(rows.shape[0], n_groups, 2, 2, SSM_STATE).transpose(0, 2, 3, 1, 4)


def kernel(x_prompt, x_sample, cache_k, cache_v, state_ssm, c, c_ctx, w_mod, b_mod, norm1, norm2, w_in, q_norm, k_norm, ssm_a_re, ssm_a_im, ssm_log_dt, ssm_b_re, ssm_b_im, ssm_c_re, ssm_c_im, ssm_d, w_glu, b_glu, w_out, w_ffn_in, w_ffn_out, final_norm):
    batch, seq, d_model = x_prompt.shape
    dec_batch, dec_seq, _ = x_sample.shape
    depth = w_in.shape[0]
    past = cache_k.shape[2]
    n_groups = ssm_a_re.shape[2]
    d_ssm = n_groups * SSM_GROUP

    cond = jnp.zeros((SUBLANES, d_model), F32).at[0].set(c_ctx).at[1:1 + dec_batch].set(c)
    mods = _mods_call(cond, w_mod, b_mod).reshape(depth, SUBLANES, N_MOD, d_model)
    tab, wx, wxs, tt, vt = _ssm_prep_call(ssm_a_re, ssm_a_im, ssm_log_dt, ssm_b_re, ssm_b_im,
                                          ssm_c_re, ssm_c_im)
    rope_tabs = _rope_tables(dec_seq)
    head_ids = jnp.arange(D_ATTN) // HEAD_DIM
    ones_bd = (head_ids[:, None] == head_ids[None, :]).astype(BF16)

    xp = x_prompt.reshape(batch * seq, d_model)
    xs = x_sample.reshape(dec_batch * dec_seq, d_model)
    zero_state = jnp.zeros((batch, n_groups * 2 * LANES), F32)
    w_in_b = w_in.astype(BF16)
    wglu_b = w_glu.astype(BF16)
    wout_b = w_out.astype(BF16)
    wffi_b = w_ffn_in.astype(BF16)
    wffo_b = w_ffn_out.astype(BF16)
    g1 = norm1.reshape(depth, 1, d_model)
    g2 = norm2.reshape(depth, 1, d_model)
    qg = jnp.tile(q_norm, (1, N_Q_HEADS)).reshape(depth, 1, D_ATTN)
    kg = jnp.tile(k_norm, (1, N_KV_HEADS)).reshape(depth, 1, D_KV)
    dskip = ssm_d.reshape(depth, 1, d_ssm)
    bglu = b_glu.reshape(depth, 1, d_ssm)
    fnorm = final_norm.reshape(1, d_model)
    new_k, new_v, new_s = [], [], []
    for l in range(depth):
        final = l == depth - 1
        for is_ctx in (True, False):
            if is_ctx:
                x2d, n_b, n_l, tq, attn_nb = xp, batch, seq, seq, 8
                mod_row, tokens_per_batch, tabs, ck, cv, h0 = 0, None, None, None, None, zero_state
            else:
                x2d, n_b, n_l, tq, attn_nb = xs, dec_batch, dec_seq, 512, 1
                mod_row, tokens_per_batch, tabs = 1, dec_seq, rope_tabs
                ck = cache_k[:, l].reshape(dec_batch, past, D_KV)
                cv = cache_v[:, l].reshape(dec_batch, past, D_KV)
                h0 = _states_to_lanes(state_ssm[:, l])
            q, k, v, u, u_blk = _inproj_call(x2d, mods, g1, w_in_b, qg, kg, ones_bd, tabs, l, mod_row,
                                             tokens_per_batch, 2 * TOKEN_TILE)
            attn = _attn_call(q.reshape(n_b, n_l, D_ATTN), k.reshape(n_b, n_l, D_KV),
                              v.reshape(n_b, n_l, D_KV), ck, cv, tq, attn_nb)
            y, ht = _ssm_call(u_blk, h0, tab, wx, wxs, tt, vt, l, n_b, n_l)
            x_new = _mix_ffn_call(attn.reshape(-1, D_ATTN), y, u, x2d, mods, dskip, wglu_b, bglu, wout_b,
                                  g2, wffi_b, wffo_b, fnorm, l, mod_row, tokens_per_batch, TOKEN_TILE,
                                  final)
            if is_ctx:
                xp = x_new
                new_k.append(k.reshape(batch, seq, N_KV_HEADS, HEAD_DIM))
                new_v.append(v.reshape(batch, seq, N_KV_HEADS, HEAD_DIM))
                new_s.append(_lanes_to_states(ht, n_groups))
            else:
                xs = x_new
    return (xp.reshape(batch, seq, d_model), xs.reshape(dec_batch, dec_seq, d_model),
            jnp.stack(new_k, axis=1), jnp.stack(new_v, axis=1), jnp.stack(new_s, axis=1))
```

```python
import functools

import jax
import jax.numpy as jnp
from jax import lax
from jax.experimental import pallas as pl
from jax.experimental.pallas import tpu as pltpu

F32 = jnp.float32
BF16 = jnp.bfloat16

HEAD_DIM = 64
N_Q_HEADS = 8
N_KV_HEADS = 2
D_ATTN = N_Q_HEADS * HEAD_DIM
D_KV = N_KV_HEADS * HEAD_DIM
SSM_GROUP = 16
SSM_STATE = 64
GRID_W = 64
ROPE_THETA = 10000.0
N_MOD = 6
EPS = 1e-6

LANES = 128
SUBLANES = 8
MXU_DIM = 256
SSM_BLOCK = MXU_DIM // SSM_GROUP
SLAB_GROUPS = LANES // SSM_GROUP
VMEM_LIMIT = 56 * 1024 * 1024
TOKEN_TILE = 512

_NT = (((1,), (1,)), ((), ()))


def _const_spec(shape):
    nd = len(shape)
    return pl.BlockSpec(shape, lambda *_: (0,) * nd, pipeline_mode=pl.Buffered(1))


def _layer_spec(shape, layer):
    nd = len(shape)
    return pl.BlockSpec((None,) + shape, lambda *_: (layer,) + (0,) * nd,
                        pipeline_mode=pl.Buffered(1))


def _mod_spec(d_model, layer, first_row, tiles_per_batch):
    if tiles_per_batch is None:
        return pl.BlockSpec((None, 1, N_MOD, d_model), lambda i: (layer, first_row, 0, 0))
    return pl.BlockSpec((None, 1, N_MOD, d_model),
                        lambda i: (layer, first_row + i // tiles_per_batch, 0, 0))


def _rms(x, gain):
    ms = jnp.mean(x * x, axis=-1, keepdims=True)
    return x * lax.rsqrt(ms + EPS) * gain


def _mods_kernel(cond_ref, w_ref, b_ref, o_ref):
    c = cond_ref[...]
    act = c * jax.nn.sigmoid(c)
    o_ref[0] = jnp.dot(act.astype(BF16), w_ref[0].astype(BF16), preferred_element_type=F32) + b_ref[0]


def _mods_call(cond, w_mod, b_mod):
    depth, d_model, n_out = w_mod.shape
    rows = cond.shape[0]
    nt = 4
    tn = n_out // nt
    return pl.pallas_call(
        _mods_kernel,
        grid=(depth, nt),
        in_specs=[
            pl.BlockSpec((rows, d_model), lambda l, j: (0, 0)),
            pl.BlockSpec((1, d_model, tn), lambda l, j: (l, 0, j)),
            pl.BlockSpec((1, 1, tn), lambda l, j: (l, 0, j)),
        ],
        out_specs=pl.BlockSpec((1, rows, tn), lambda l, j: (l, 0, j)),
        out_shape=jax.ShapeDtypeStruct((depth, rows, n_out), F32),
        compiler_params=pltpu.CompilerParams(
            dimension_semantics=("arbitrary", "arbitrary"), vmem_limit_bytes=VMEM_LIMIT),
        name="adaln_mods",
    )(cond, w_mod, b_mod.reshape(depth, 1, n_out))


def _group_sumsq(z, ones_ref, width):
    z2 = z * z
    hi = z2.astype(BF16)
    lo = (z2 - hi.astype(F32)).astype(BF16)
    step = min(width, MXU_DIM)
    ones = ones_ref[0:step, 0:step]
    cols = [jnp.dot(hi[:, c:c + step], ones, preferred_element_type=F32)
            + jnp.dot(lo[:, c:c + step], ones, preferred_element_type=F32)
            for c in range(0, width, step)]
    return cols[0] if len(cols) == 1 else jnp.concatenate(cols, axis=1)


def _head_rms(z, gain, ones_ref):
    ss = _group_sumsq(z, ones_ref, z.shape[-1])
    return z * lax.rsqrt(ss * (1.0 / HEAD_DIM) + EPS) * gain


def _rope(z, cos, sin_signed):
    width = z.shape[-1]
    lane = lax.broadcasted_iota(jnp.int32, z.shape, 1)
    nxt = pltpu.roll(z, width - 1, axis=1)
    prv = pltpu.roll(z, 1, axis=1)
    partner = jnp.where((lane & 1) == 0, nxt, prv)
    return z * cos + partner * sin_signed


def _blocks_from_tokens(u_rows, ub_ref):
    n_slab, _, _, n_blk, _ = ub_ref.shape
    chunk = 2 * SUBLANES
    piece = lax.broadcasted_iota(jnp.int32, (chunk, LANES), 1) // SSM_GROUP
    for s in range(n_slab):
        sets, dests = [], []
        for c in range(n_blk // chunk):
            for half in range(2):
                tok = c * chunk * SSM_BLOCK + half * SLAB_GROUPS
                sets.append([u_rows[s, pl.ds(tok + rr, chunk, stride=SSM_BLOCK), :]
                             for rr in range(SLAB_GROUPS)])
                dests.append((half, c))
        for (half, c), vs in zip(dests, _transpose_pieces(sets, piece)):
            for gl, v in enumerate(vs):
                ub_ref[s, gl, half, c * chunk:(c + 1) * chunk, :] = v


def _inproj_kernel(*refs, rope):
    if rope:
        (x_ref, mod_ref, g1_ref, w_ref, qg_ref, kg_ref, ones_ref, cos_ref, sin_ref,
         q_ref, k_ref, v_ref, u_ref, ub_ref, u_rows) = refs
    else:
        (x_ref, mod_ref, g1_ref, w_ref, qg_ref, kg_ref, ones_ref,
         q_ref, k_ref, v_ref, u_ref, ub_ref, u_rows) = refs
    mod = mod_ref[0]
    h = (_rms(x_ref[...], g1_ref[...]) * (1.0 + mod[1:2]) + mod[0:1]).astype(BF16)
    qkv_cols = D_ATTN + 2 * D_KV
    u = jnp.dot(h, w_ref[:, qkv_cols:], preferred_element_type=F32)
    u_ref[...] = u
    for s in range(u_rows.shape[0]):
        u_rows[s] = u[:, s * LANES:(s + 1) * LANES]
    _blocks_from_tokens(u_rows, ub_ref)
    proj = jnp.dot(h, w_ref[:, :qkv_cols], preferred_element_type=F32)
    q = _head_rms(proj[:, :D_ATTN], qg_ref[...], ones_ref)
    k = _head_rms(proj[:, D_ATTN:D_ATTN + D_KV], kg_ref[...], ones_ref)
    if rope:
        q = _rope(q, cos_ref[...], sin_ref[...])
        k = _rope(k, cos_ref[:, 0:D_KV], sin_ref[:, 0:D_KV])
    q_ref[...] = (q * (HEAD_DIM ** -0.5)).astype(BF16)
    k_ref[...] = k
    v_ref[...] = proj[:, D_ATTN + D_KV:qkv_cols]


def _inproj_call(x2d, mods, g1, w_in_b, qg, kg, ones_bd, rope_tabs, layer, mod_row, tokens_per_batch,
                 tile):
    n_tok, d_model = x2d.shape
    d_in = w_in_b.shape[-1]
    d_ssm = d_in - D_ATTN - 2 * D_KV
    n_slab = d_ssm // LANES
    tiles_per_batch = None if tokens_per_batch is None else tokens_per_batch // tile
    in_specs = [
        pl.BlockSpec((tile, d_model), lambda i: (i, 0)),
        _mod_spec(d_model, layer, mod_row, tiles_per_batch),
        _layer_spec((1, d_model), layer),
        _layer_spec((d_model, d_in), layer),
        _layer_spec((1, D_ATTN), layer),
        _layer_spec((1, D_KV), layer),
        _const_spec((D_ATTN, D_ATTN)),
    ]
    args = [x2d, mods, g1, w_in_b, qg, kg, ones_bd]
    if rope_tabs is not None:
        in_specs += [pl.BlockSpec((tile, D_ATTN), lambda i: (i % tiles_per_batch, 0))] * 2
        args += list(rope_tabs)
    row = lambda width: pl.BlockSpec((tile, width), lambda i: (i, 0))
    return pl.pallas_call(
        functools.partial(_inproj_kernel, rope=rope_tabs is not None),
        grid=(n_tok // tile,),
        in_specs=in_specs,
        out_specs=(row(D_ATTN), row(D_KV), row(D_KV), row(d_ssm),
                   pl.BlockSpec((n_slab, SLAB_GROUPS, 2, tile // SSM_BLOCK, LANES),
                                lambda i: (0, 0, 0, i, 0))),
        out_shape=(jax.ShapeDtypeStruct((n_tok, D_ATTN), BF16),
                   jax.ShapeDtypeStruct((n_tok, D_KV), F32),
                   jax.ShapeDtypeStruct((n_tok, D_KV), F32),
                   jax.ShapeDtypeStruct((n_tok, d_ssm), F32),
                   jax.ShapeDtypeStruct((n_slab, SLAB_GROUPS, 2, n_tok // SSM_BLOCK, LANES), F32)),
        scratch_shapes=[pltpu.VMEM((n_slab, tile, LANES), F32)],
        compiler_params=pltpu.CompilerParams(
            dimension_semantics=("arbitrary",), vmem_limit_bytes=VMEM_LIMIT),
        name="inproj_rope" if rope_tabs is not None else "inproj",
    )(*args)


def _attn_kernel(*refs, has_cache, nb):
    if has_cache:
        q_ref, kn_ref, vn_ref, ck_ref, cv_ref, o_ref, kvar, vvar = refs
    else:
        q_ref, kn_ref, vn_ref, o_ref, kvar, vvar = refs

    @pl.when(pl.program_id(1) == 0)
    def _():
        for bi in range(nb):
            for new_ref, cache_ref, dst in ((kn_ref, ck_ref if has_cache else None, kvar),
                                            (vn_ref, cv_ref if has_cache else None, vvar)):
                src = new_ref[bi]
                if has_cache:
                    src = jnp.concatenate([cache_ref[bi], src], axis=0)
                low = lax.broadcasted_iota(jnp.int32, src.shape, 1) < HEAD_DIM
                head0 = jnp.where(low, src, 0.0)
                head1 = jnp.where(low, 0.0, src)
                dst[4 * bi + 0] = head0.astype(BF16)
                dst[4 * bi + 1] = pltpu.roll(head0, HEAD_DIM, axis=1).astype(BF16)
                dst[4 * bi + 2] = pltpu.roll(head1, HEAD_DIM, axis=1).astype(BF16)
                dst[4 * bi + 3] = head1.astype(BF16)

    slabs = D_ATTN // LANES
    for bi in range(nb):
        for slab in range(slabs):
            qs = q_ref[bi, :, slab * LANES:(slab + 1) * LANES]
            kv = slab // (slabs // N_KV_HEADS)
            acc = None
            for half in range(2):
                idx = 4 * bi + kv * 2 + half
                s = lax.dot_general(qs, kvar[idx], _NT, preferred_element_type=F32)
                m = jnp.max(s, axis=-1, keepdims=True)
                p = jnp.exp(s - m)
                denom = jnp.sum(p, axis=-1, keepdims=True)
                o = jnp.dot(p.astype(BF16), vvar[idx], preferred_element_type=F32) / denom
                acc = o if acc is None else acc + o
            o_ref[bi, :, slab * LANES:(slab + 1) * LANES] = acc.astype(BF16)


def _attn_call(q3, k3, v3, ck3, cv3, tq, nb):
    b, l, _ = q3.shape
    has_cache = ck3 is not None
    s_len = l + (ck3.shape[1] if has_cache else 0)
    in_specs = [
        pl.BlockSpec((nb, tq, D_ATTN), lambda bi, qi: (bi, qi, 0)),
        pl.BlockSpec((nb, l, D_KV), lambda bi, qi: (bi, 0, 0)),
        pl.BlockSpec((nb, l, D_KV), lambda bi, qi: (bi, 0, 0)),
    ]
    args = [q3, k3, v3]
    if has_cache:
        in_specs += [pl.BlockSpec((nb, ck3.shape[1], D_KV), lambda bi, qi: (bi, 0, 0))] * 2
        args += [ck3, cv3]
    return pl.pallas_call(
        functools.partial(_attn_kernel, has_cache=has_cache, nb=nb),
        grid=(b // nb, l // tq),
        in_specs=in_specs,
        out_specs=pl.BlockSpec((nb, tq, D_ATTN), lambda bi, qi: (bi, qi, 0)),
        out_shape=jax.ShapeDtypeStruct((b, l, D_ATTN), BF16),
        scratch_shapes=[pltpu.VMEM((4 * nb, s_len, D_KV), BF16), pltpu.VMEM((4 * nb, s_len, D_KV), BF16)],
        compiler_params=pltpu.CompilerParams(
            dimension_semantics=("arbitrary", "arbitrary"), vmem_limit_bytes=VMEM_LIMIT),
        name="attn_cache" if has_cache else "attn",
    )(*args)


def _dot3(a, b):
    a_hi = a.astype(BF16)
    b_hi = b.astype(BF16)
    a_lo = (a - a_hi.astype(F32)).astype(BF16)
    b_lo = (b - b_hi.astype(F32)).astype(BF16)
    dot = lambda x, y: lax.dot_general(x, y, _NT, preferred_element_type=F32)
    return dot(a_hi, b_hi) + (dot(a_hi, b_lo) + dot(a_lo, b_hi))


def _ssm_prep_kernel(are_ref, aim_ref, ldt_ref, bre_ref, bim_ref, cre_ref, cim_ref,
                     tab_ref, wx_ref, wxs_ref, vt_ref, tt_ref, clm, lbm):
    n_groups = wx_ref.shape[0]
    width = SSM_BLOCK * SSM_GROUP
    shape = (SSM_GROUP, LANES)
    low = lax.broadcasted_iota(jnp.int32, shape, 1) < SSM_STATE
    row_blk = lax.broadcasted_iota(jnp.int32, (width, width), 0) // SSM_GROUP
    col_blk = lax.broadcasted_iota(jnp.int32, (width, width), 1) // SSM_GROUP

    def cmul(xr, xi, yr, yi):
        return xr * yr - xi * yi, xr * yi + xi * yr

    def group(g, carry):
        rows = pl.ds(pl.multiple_of(g * SSM_GROUP, SSM_GROUP), SSM_GROUP)
        tile = None
        for d in range(2):
            lanes = slice(d * LANES, (d + 1) * LANES)
            dt = jnp.exp(ldt_ref[d, rows, :])
            ar = are_ref[d, rows, :]
            ai = aim_ref[d, rows, :]
            mag = jnp.exp(ar * dt)
            lr = mag * jnp.cos(ai * dt)
            li = mag * jnp.sin(ai * dt)
            den = ar * ar + ai * ai
            zr = ((lr - 1.0) * ar + li * ai) / den
            zi = (li * ar - (lr - 1.0) * ai) / den
            bbr, bbi = cmul(zr, zi, bre_ref[d, rows, :], bim_ref[d, rows, :])
            cr = cre_ref[d, rows, :]
            ci = cim_ref[d, rows, :]
            inv = 1.0 / (lr * lr + li * li)
            ir = lr * inv
            ii = -li * inv
            pr, pi = jnp.ones(shape, F32), jnp.zeros(shape, F32)
            qr, qi = pr, pi
            for j in range(SSM_BLOCK + 1):
                c_re, c_im = cmul(cr, ci, pr, pi)
                c_pos = jnp.where(low, c_re, -c_im)
                if j >= 1:
                    vt_ref[g, (j - 1) if d == 0 else (SSM_BLOCK - j), :, lanes] = c_pos.astype(BF16)
                if j < SSM_BLOCK:
                    g_re, g_im = cmul(pr, pi, bbr, bbi)
                    g_pos = jnp.where(low, g_re, g_im)
                    r = (SSM_BLOCK - 1 - j) if d == 0 else j
                    wx_ref[g, r, :, lanes] = g_pos.astype(BF16)
                    wxs_ref[g, r, :, lanes] = jnp.where(low, g_im, g_re).astype(BF16)
                    blk = pl.ds(j * SSM_GROUP, SSM_GROUP)
                    if d == 0:
                        n_re, n_im = cmul(qr, qi, bbr, bbi)
                        clm[blk, :] = c_pos
                        lbm[blk, :] = jnp.where(low, n_re, n_im)
                    else:
                        m_re, m_im = cmul(cr, ci, qr, qi)
                        clm[blk, :] = jnp.where(low, m_re, -m_im)
                        lbm[blk, :] = g_pos
                else:
                    tab_ref[g, d, 0] = pr[0:SUBLANES]
                    tab_ref[g, d, 1] = jnp.where(low, -pi, pi)[0:SUBLANES]
                pr, pi = cmul(pr, pi, lr, li)
                qr, qi = cmul(qr, qi, ir, ii)
            t = _dot3(clm[...], lbm[...])
            t = jnp.where((col_blk <= row_blk) if d == 0 else (col_blk >= row_blk), t, 0.0)
            tile = t if tile is None else tile + t
        tt_ref[g] = tile.astype(BF16)
        return carry

    lax.fori_loop(0, n_groups, group, 0, unroll=2)


def _ssm_prep_call(a_re, a_im, log_dt, b_re, b_im, c_re, c_im):
    depth, ndir, g, p, h = b_re.shape
    halves = 2
    g_half = g // halves
    rows = g_half * h
    width = SSM_BLOCK * h

    def rows_lanes(a):
        a = a.astype(F32).reshape(depth, ndir, g * h, p)
        return jnp.concatenate([a, a], axis=-1)

    per_state = lambda a: rows_lanes(jnp.broadcast_to(a[:, :, :, None, :], (depth, ndir, g, h, p)))
    ldt = rows_lanes(jnp.broadcast_to(log_dt[:, :, :, None, None], (depth, ndir, g, h, p)))
    to_hp = lambda a: rows_lanes(jnp.swapaxes(a, -1, -2))
    in_spec = pl.BlockSpec((None, ndir, rows, LANES), lambda l, s: (l, 0, s, 0))
    out_blk = lambda *shape: pl.BlockSpec((None, g_half) + shape, lambda l, s: (l, s) + (0,) * len(shape))
    op4 = jax.ShapeDtypeStruct((depth, g, SSM_BLOCK, h, 2 * LANES), BF16)
    tab, wx, wxs, vt, tt = pl.pallas_call(
        _ssm_prep_kernel,
        grid=(depth, halves),
        in_specs=[in_spec] * 7,
        out_specs=(out_blk(2, 2, SUBLANES, LANES), out_blk(SSM_BLOCK, h, 2 * LANES),
                   out_blk(SSM_BLOCK, h, 2 * LANES), out_blk(SSM_BLOCK, h, 2 * LANES),
                   out_blk(width, width)),
        out_shape=(jax.ShapeDtypeStruct((depth, g, 2, 2, SUBLANES, LANES), F32), op4, op4, op4,
                   jax.ShapeDtypeStruct((depth, g, width, width), BF16)),
        scratch_shapes=[pltpu.VMEM((width, LANES), F32), pltpu.VMEM((width, LANES), F32)],
        compiler_params=pltpu.CompilerParams(
            dimension_semantics=("arbitrary", "arbitrary"), vmem_limit_bytes=VMEM_LIMIT),
        name="ssm_prepare",
    )(per_state(a_re), per_state(a_im), ldt, to_hp(b_re), to_hp(b_im), rows_lanes(c_re), rows_lanes(c_im))
    merge = lambda a: a.reshape(depth, g, width, 2 * LANES)
    return tab, merge(wx), merge(wxs), tt, merge(vt)


def _transpose_pieces(sets, piece):
    sets = [list(vs) for vs in sets]
    n = len(sets[0])
    s = n // 2
    while s:
        keep_low = (piece & s) == 0
        pairs = [(vs, i) for vs in sets for i in range(n) if not i & s]
        moved = [(pltpu.roll(vs[i + s], s * SSM_GROUP, axis=1),
                  pltpu.roll(vs[i], LANES - s * SSM_GROUP, axis=1)) for vs, i in pairs]
        for (vs, i), (from_hi, from_lo) in zip(pairs, moved):
            vs[i], vs[i + s] = jnp.where(keep_low, vs[i], from_hi), jnp.where(keep_low, from_lo, vs[i + s])
        s //= 2
    return sets


def _ssm_kernel(u_ref, h0_ref, tab_ref, wx_ref, wxs_ref, tt_ref, vt_ref, y_ref, ht_ref, ug, xb, xsb, yg,
                *, batch, seq, pitch):
    m_blk = seq // SSM_BLOCK

    @pl.when(pl.program_id(0) == 0)
    def _():
        ug[...] = jnp.zeros(ug.shape, F32)

    for gl in range(SLAB_GROUPS):
        for half in range(2):
            for b in range(batch):
                ug[gl, half, b * pitch:b * pitch + m_blk, :] = u_ref[gl, half, b * m_blk:(b + 1) * m_blk, :]
        lhs = jnp.concatenate([ug[gl, 0], ug[gl, 1]], axis=1).astype(BF16)
        x = jnp.dot(lhs, wx_ref[gl], preferred_element_type=F32)
        xb[gl, 0] = x[:, :LANES]
        xb[gl, 1] = x[:, LANES:]
        x = jnp.dot(lhs, wxs_ref[gl], preferred_element_type=F32)
        xsb[gl, 0] = x[:, :LANES]
        xsb[gl, 1] = x[:, LANES:]
        y_in = lax.dot_general(lhs, tt_ref[gl], _NT, preferred_element_type=F32)
        yg[gl, 0] = y_in[:, :LANES]
        yg[gl, 1] = y_in[:, LANES:]

    n_bt = -(-batch // SUBLANES)
    rows_per = min(batch, SUBLANES)
    per_pass = max(1, 4 // n_bt)
    for g0 in range(0, SLAB_GROUPS, per_pass):
        keys = [(gl, d, bt) for gl in range(g0, g0 + per_pass) for d in range(2) for bt in range(n_bt)]
        lane0 = lambda gl, d: (gl * 2 + d) * LANES
        init = []
        for gl, d, bt in keys:
            h = h0_ref[bt * SUBLANES:bt * SUBLANES + rows_per, lane0(gl, d):lane0(gl, d) + LANES]
            init += [h, pltpu.roll(h, SSM_STATE, axis=1)]
        mult = {(gl, d): (tab_ref[gl, d, 0, 0:rows_per, :], tab_ref[gl, d, 1, 0:rows_per, :])
                for gl, d, _ in keys}

        def step(m, hs, keys=keys, mult=mult):
            out = []
            for k, (gl, d, bt) in enumerate(keys):
                h, h_sw = hs[2 * k], hs[2 * k + 1]
                row = m if d == 0 else m_blk - 1 - m
                idx = pl.ds(bt * SUBLANES * pitch + row, rows_per, stride=pitch)
                x = xb[gl, d, idx, :]
                x_sw = xsb[gl, d, idx, :]
                ug[gl, d, idx, :] = h
                a, b = mult[(gl, d)]
                out += [a * h + b * h_sw + x, a * h_sw - b * h + x_sw]
            return tuple(out)

        final = lax.fori_loop(0, m_blk, step, tuple(init), unroll=2)
        for k, (gl, d, bt) in enumerate(keys):
            ht_ref[bt * SUBLANES:bt * SUBLANES + rows_per, lane0(gl, d):lane0(gl, d) + LANES] = final[2 * k]

    for gl in range(SLAB_GROUPS):
        states = jnp.concatenate([ug[gl, 0], ug[gl, 1]], axis=1).astype(BF16)
        y_st = lax.dot_general(states, vt_ref[gl], _NT, preferred_element_type=F32)
        for half in range(2):
            total = yg[gl, half] + y_st[:, half * LANES:(half + 1) * LANES]
            for b in range(batch):
                y_ref[gl, half, b * m_blk:(b + 1) * m_blk, :] = total[b * pitch:b * pitch + m_blk]


def _ssm_call(u_blk, h0, tab, wx, wxs, tt, vt, layer, batch, seq):
    n_slab = u_blk.shape[0]
    m_blk = seq // SSM_BLOCK
    blk_spec = pl.BlockSpec((None, SLAB_GROUPS, 2, batch * m_blk, LANES), lambda s: (s, 0, 0, 0, 0))
    pitch = m_blk + SUBLANES
    rows_p = batch * pitch
    st_lanes = SLAB_GROUPS * 2 * LANES
    op_spec = pl.BlockSpec((None, SLAB_GROUPS, MXU_DIM, MXU_DIM), lambda s: (layer, s, 0, 0))
    scratch = pltpu.VMEM((SLAB_GROUPS, 2, rows_p, LANES), F32)
    return pl.pallas_call(
        functools.partial(_ssm_kernel, batch=batch, seq=seq, pitch=pitch),
        grid=(n_slab,),
        in_specs=[
            blk_spec,
            pl.BlockSpec((batch, st_lanes), lambda s: (0, s)),
            pl.BlockSpec((None, SLAB_GROUPS, 2, 2, SUBLANES, LANES), lambda s: (layer, s, 0, 0, 0, 0)),
            op_spec, op_spec, op_spec, op_spec,
        ],
        out_specs=(blk_spec,
                   pl.BlockSpec((batch, st_lanes), lambda s: (0, s))),
        out_shape=(jax.ShapeDtypeStruct((n_slab, SLAB_GROUPS, 2, batch * m_blk, LANES), F32),
                   jax.ShapeDtypeStruct((batch, n_slab * st_lanes), F32)),
        scratch_shapes=[scratch, scratch, scratch, scratch],
        compiler_params=pltpu.CompilerParams(
            dimension_semantics=("arbitrary",), vmem_limit_bytes=VMEM_LIMIT),
        name=f"ssm_scan_b{batch}",
    )(u_blk, h0, tab, wx, wxs, tt, vt)


def _tokens_from_blocks(yb_ref, rows_ref, slot):
    n_slab, _, _, n_blk, _ = yb_ref.shape
    chunk = 2 * SUBLANES
    piece = lax.broadcasted_iota(jnp.int32, (chunk, LANES), 1) // SSM_GROUP
    for s in range(n_slab):
        sets, dests = [], []
        for c in range(n_blk // chunk):
            for half in range(2):
                sets.append([yb_ref[s, gl, half, c * chunk:(c + 1) * chunk, :] for gl in range(SLAB_GROUPS)])
                dests.append(c * chunk * SSM_BLOCK + half * SLAB_GROUPS)
        for tok, vs in zip(dests, _transpose_pieces(sets, piece)):
            for rr, v in enumerate(vs):
                rows_ref[slot, s, pl.ds(tok + rr, chunk, stride=SSM_BLOCK), :] = v


def _mix_ffn_kernel(attn_ref, yb_first_ref, yb_next_ref, u_ref, x_ref, mod_ref, dskip_ref, wglu_ref,
                    bglu_ref, wout_ref, g2_ref, wffi_ref, wffo_ref, fn_ref, o_ref, y_rows,
                    *, final, ff_chunks):
    d_ff = wffo_ref.shape[0]
    i = pl.program_id(0)
    slot = i % 2

    @pl.when(i == 0)
    def _():
        _tokens_from_blocks(yb_first_ref, y_rows, 0)

    y_ssm = jnp.concatenate([y_rows[slot, s] for s in range(y_rows.shape[1])], axis=1)
    _tokens_from_blocks(yb_next_ref, y_rows, 1 - slot)
    y = y_ssm + dskip_ref[...] * u_ref[...]
    g = 0.5 * y * (1.0 + lax.erf(y * (2.0 ** -0.5)))
    z = jnp.dot(g.astype(BF16), wglu_ref[...], preferred_element_type=F32) + bglu_ref[...]
    ssm_out = g * jax.nn.sigmoid(z)
    mixed = (jnp.dot(attn_ref[...], wout_ref[0:D_ATTN, :], preferred_element_type=F32)
             + jnp.dot(ssm_out.astype(BF16), wout_ref[D_ATTN:, :], preferred_element_type=F32))
    mod = mod_ref[0]
    x1 = x_ref[...] + mod[2:3] * mixed
    h2 = (_rms(x1, g2_ref[...]) * (1.0 + mod[4:5]) + mod[3:4]).astype(BF16)
    tiles = d_ff // MXU_DIM
    edges = [MXU_DIM * ((tiles * ci) // ff_chunks) for ci in range(ff_chunks)] + [d_ff]
    acc = None
    for c0, c1 in zip(edges[:-1], edges[1:]):
        gate = jnp.dot(h2, wffi_ref[:, c0:c1], preferred_element_type=F32)
        up = jnp.dot(h2, wffi_ref[:, d_ff + c0:d_ff + c1], preferred_element_type=F32)
        act = (gate * jax.nn.sigmoid(gate) * up).astype(BF16)
        part = jnp.dot(act, wffo_ref[c0:c1, :], preferred_element_type=F32)
        acc = part if acc is None else acc + part
    x2 = x1 + mod[5:6] * acc
    if final:
        x2 = _rms(x2, fn_ref[...])
    o_ref[...] = x2


def _mix_ffn_call(attn, y, u, x2d, mods, dskip, wglu_b, bglu, wout_b, g2, wffi_b, wffo_b, fnorm,
                  layer, mod_row, tokens_per_batch, tile, final):
    n_tok, d_model = x2d.shape
    d_ssm = u.shape[1]
    d_ff = wffo_b.shape[-2]
    tiles_per_batch = None if tokens_per_batch is None else tokens_per_batch // tile
    row = lambda width: pl.BlockSpec((tile, width), lambda i: (i, 0))
    n_slab = y.shape[0]
    n_tiles = n_tok // tile
    y_block = (n_slab, SLAB_GROUPS, 2, tile // SSM_BLOCK, LANES)
    y_first = pl.BlockSpec(y_block, lambda i: (0, 0, 0, 0, 0))
    y_next = pl.BlockSpec(y_block, lambda i: (0, 0, 0, jnp.minimum(i + 1, n_tiles - 1), 0))
    return pl.pallas_call(
        functools.partial(_mix_ffn_kernel, final=final, ff_chunks=2),
        grid=(n_tiles,),
        in_specs=[
            row(D_ATTN), y_first, y_next, row(d_ssm), row(d_model),
            _mod_spec(d_model, layer, mod_row, tiles_per_batch),
            _layer_spec((1, d_ssm), layer),
            _layer_spec((d_ssm, d_ssm), layer),
            _layer_spec((1, d_ssm), layer),
            _layer_spec((D_ATTN + d_ssm, d_model), layer),
            _layer_spec((1, d_model), layer),
            _layer_spec((d_model, 2 * d_ff), layer),
            _layer_spec((d_ff, d_model), layer),
            _const_spec((1, d_model)),
        ],
        out_specs=row(d_model),
        out_shape=jax.ShapeDtypeStruct((n_tok, d_model), F32),
        scratch_shapes=[pltpu.VMEM((2, n_slab, tile, LANES), F32)],
        compiler_params=pltpu.CompilerParams(
            dimension_semantics=("arbitrary",), vmem_limit_bytes=VMEM_LIMIT),
        name="mix_ffn_final" if final else "mix_ffn",
    )(attn, y, y, u, x2d, mods, dskip, wglu_b, bglu, wout_b, g2, wffi_b, wffo_b, fnorm)


def _rope_tables(n_tokens):
    axis_dim = HEAD_DIM // 2
    rows = n_tokens // GRID_W
    row = jnp.repeat(jnp.arange(rows, dtype=F32), GRID_W)
    col = jnp.tile(jnp.arange(GRID_W, dtype=F32), rows)
    inv_freq = ROPE_THETA ** (-jnp.arange(0, axis_dim, 2, dtype=F32) / axis_dim)
    ang = jnp.concatenate([row[:, None] * inv_freq, col[:, None] * inv_freq], axis=-1)
    cos = jnp.repeat(jnp.cos(ang), 2, axis=-1)
    sin = jnp.repeat(jnp.sin(ang), 2, axis=-1) * jnp.tile(jnp.array([-1.0, 1.0], F32), HEAD_DIM // 2)
    return jnp.tile(cos, (1, N_Q_HEADS)), jnp.tile(sin, (1, N_Q_HEADS))


def _states_to_lanes(st):
    return st.transpose(0, 3, 1, 2, 4).reshape(st.shape[0], -1)


def _lanes_to_states(rows, n_groups):
    return rows.reshape(rows.shape[0], n_groups, 2, 2, SSM_STATE).transpose(0, 2, 3, 1, 4)


def kernel(x_prompt, x_sample, cache_k, cache_v, state_ssm, c, c_ctx, w_mod, b_mod, norm1, norm2, w_in, q_norm, k_norm, ssm_a_re, ssm_a_im, ssm_log_dt, ssm_b_re, ssm_b_im, ssm_c_re, ssm_c_im, ssm_d, w_glu, b_glu, w_out, w_ffn_in, w_ffn_out, final_norm):
    batch, seq, d_model = x_prompt.shape
    dec_batch, dec_seq, _ = x_sample.shape
    depth = w_in.shape[0]
    past = cache_k.shape[2]
    n_groups = ssm_a_re.shape[2]
    d_ssm = n_groups * SSM_GROUP

    cond = jnp.zeros((SUBLANES, d_model), F32).at[0].set(c_ctx).at[1:1 + dec_batch].set(c)
    mods = _mods_call(cond, w_mod, b_mod).reshape(depth, SUBLANES, N_MOD, d_model)
    tab, wx, wxs, tt, vt = _ssm_prep_call(ssm_a_re, ssm_a_im, ssm_log_dt, ssm_b_re, ssm_b_im,
                                          ssm_c_re, ssm_c_im)
    rope_tabs = _rope_tables(dec_seq)
    head_ids = jnp.arange(D_ATTN) // HEAD_DIM
    ones_bd = (head_ids[:, None] == head_ids[None, :]).astype(BF16)

    xp = x_prompt.reshape(batch * seq, d_model)
    xs = x_sample.reshape(dec_batch * dec_seq, d_model)
    zero_state = jnp.zeros((batch, n_groups * 2 * LANES), F32)
    w_in_b = w_in.astype(BF16)
    wglu_b = w_glu.astype(BF16)
    wout_b = w_out.astype(BF16)
    wffi_b = w_ffn_in.astype(BF16)
    wffo_b = w_ffn_out.astype(BF16)
    g1 = norm1.reshape(depth, 1, d_model)
    g2 = norm2.reshape(depth, 1, d_model)
    qg = jnp.tile(q_norm, (1, N_Q_HEADS)).reshape(depth, 1, D_ATTN)
    kg = jnp.tile(k_norm, (1, N_KV_HEADS)).reshape(depth, 1, D_KV)
    dskip = ssm_d.reshape(depth, 1, d_ssm)
    bglu = b_glu.reshape(depth, 1, d_ssm)
    fnorm = final_norm.reshape(1, d_model)
    new_k, new_v, new_s = [], [], []
    for l in range(depth):
        final = l == depth - 1
        for is_ctx in (True, False):
            if is_ctx:
                x2d, n_b, n_l, tq, attn_nb = xp, batch, seq, seq, 8
                mod_row, tokens_per_batch, tabs, ck, cv, h0 = 0, None, None, None, None, zero_state
            else:
                x2d, n_b, n_l, tq, attn_nb = xs, dec_batch, dec_seq, 512, 1
                mod_row, tokens_per_batch, tabs = 1, dec_seq, rope_tabs
                ck = cache_k[:, l].reshape(dec_batch, past, D_KV)
                cv = cache_v[:, l].reshape(dec_batch, past, D_KV)
                h0 = _states_to_lanes(state_ssm[:, l])
            q, k, v, u, u_blk = _inproj_call(x2d, mods, g1, w_in_b, qg, kg, ones_bd, tabs, l, mod_row,
                                             tokens_per_batch, 2 * TOKEN_TILE)
            attn = _attn_call(q.reshape(n_b, n_l, D_ATTN), k.reshape(n_b, n_l, D_KV),
                              v.reshape(n_b, n_l, D_KV), ck, cv, tq, attn_nb)
            y, ht = _ssm_call(u_blk, h0, tab, wx, wxs, tt, vt, l, n_b, n_l)
            x_new = _mix_ffn_call(attn.reshape(-1, D_ATTN), y, u, x2d, mods, dskip, wglu_b, bglu, wout_b,
                                  g2, wffi_b, wffo_b, fnorm, l, mod_row, tokens_per_batch, TOKEN_TILE,
                                  final)
            if is_ctx:
                xp = x_new
                new_k.append(k.reshape(batch, seq, N_KV_HEADS, HEAD_DIM))
                new_v.append(v.reshape(batch, seq, N_KV_HEADS, HEAD_DIM))
                new_s.append(_lanes_to_states(ht, n_groups))
            else:
                xs = x_new
    return (xp.reshape(batch, seq, d_model), xs.reshape(dec_batch, dec_seq, d_model),
            jnp.stack(new_k, axis=1), jnp.stack(new_v, axis=1), jnp.stack(new_s, axis=1))
```

```python
import functools

import jax
import jax.numpy as jnp
from jax import lax
from jax.experimental import pallas as pl
from jax.experimental.pallas import tpu as pltpu

F32 = jnp.float32
BF16 = jnp.bfloat16

HEAD_DIM = 64
N_Q_HEADS = 8
N_KV_HEADS = 2
D_ATTN = N_Q_HEADS * HEAD_DIM
D_KV = N_KV_HEADS * HEAD_DIM
SSM_GROUP = 16
SSM_STATE = 64
GRID_W = 64
ROPE_THETA = 10000.0
N_MOD = 6
EPS = 1e-6

LANES = 128
SUBLANES = 8
MXU_DIM = 256
SSM_BLOCK = MXU_DIM // SSM_GROUP
SLAB_GROUPS = LANES // SSM_GROUP
VMEM_LIMIT = 56 * 1024 * 1024
TOKEN_TILE = 512

_NT = (((1,), (1,)), ((), ()))


def _const_spec(shape):
    nd = len(shape)
    return pl.BlockSpec(shape, lambda *_: (0,) * nd, pipeline_mode=pl.Buffered(1))


def _layer_spec(shape, layer):
    nd = len(shape)
    return pl.BlockSpec((None,) + shape, lambda *_: (layer,) + (0,) * nd,
                        pipeline_mode=pl.Buffered(1))


def _mod_spec(d_model, layer, first_row, tiles_per_batch):
    if tiles_per_batch is None:
        return pl.BlockSpec((None, 1, N_MOD, d_model), lambda i: (layer, first_row, 0, 0))
    return pl.BlockSpec((None, 1, N_MOD, d_model),
                        lambda i: (layer, first_row + i // tiles_per_batch, 0, 0))


def _rms(x, gain):
    ms = jnp.mean(x * x, axis=-1, keepdims=True)
    return x * lax.rsqrt(ms + EPS) * gain


def _mods_kernel(cond_ref, w_ref, b_ref, o_ref):
    c = cond_ref[...]
    act = c * jax.nn.sigmoid(c)
    o_ref[0] = jnp.dot(act.astype(BF16), w_ref[0].astype(BF16), preferred_element_type=F32) + b_ref[0]


def _mods_call(cond, w_mod, b_mod):
    depth, d_model, n_out = w_mod.shape
    rows = cond.shape[0]
    nt = 4
    tn = n_out // nt
    return pl.pallas_call(
        _mods_kernel,
        grid=(depth, nt),
        in_specs=[
            pl.BlockSpec((rows, d_model), lambda l, j: (0, 0)),
            pl.BlockSpec((1, d_model, tn), lambda l, j: (l, 0, j)),
            pl.BlockSpec((1, 1, tn), lambda l, j: (l, 0, j)),
        ],
        out_specs=pl.BlockSpec((1, rows, tn), lambda l, j: (l, 0, j)),
        out_shape=jax.ShapeDtypeStruct((depth, rows, n_out), F32),
        compiler_params=pltpu.CompilerParams(
            dimension_semantics=("arbitrary", "arbitrary"), vmem_limit_bytes=VMEM_LIMIT),
        name="adaln_mods",
    )(cond, w_mod, b_mod.reshape(depth, 1, n_out))


def _group_sumsq(z, ones_ref, width):
    z2 = z * z
    hi = z2.astype(BF16)
    lo = (z2 - hi.astype(F32)).astype(BF16)
    step = min(width, MXU_DIM)
    ones = ones_ref[0:step, 0:step]
    cols = [jnp.dot(hi[:, c:c + step], ones, preferred_element_type=F32)
            + jnp.dot(lo[:, c:c + step], ones, preferred_element_type=F32)
            for c in range(0, width, step)]
    return cols[0] if len(cols) == 1 else jnp.concatenate(cols, axis=1)


def _head_rms(z, gain, ones_ref):
    ss = _group_sumsq(z, ones_ref, z.shape[-1])
    return z * lax.rsqrt(ss * (1.0 / HEAD_DIM) + EPS) * gain


def _rope(z, cos, sin_signed):
    width = z.shape[-1]
    lane = lax.broadcasted_iota(jnp.int32, z.shape, 1)
    nxt = pltpu.roll(z, width - 1, axis=1)
    prv = pltpu.roll(z, 1, axis=1)
    partner = jnp.where((lane & 1) == 0, nxt, prv)
    return z * cos + partner * sin_signed


def _blocks_from_tokens(u_rows, ub_ref):
    n_slab, _, _, n_blk, _ = ub_ref.shape
    chunk = 2 * SUBLANES
    piece = lax.broadcasted_iota(jnp.int32, (chunk, LANES), 1) // SSM_GROUP
    for s in range(n_slab):
        sets, dests = [], []
        for c in range(n_blk // chunk):
            for half in range(2):
                tok = c * chunk * SSM_BLOCK + half * SLAB_GROUPS
                sets.append([u_rows[s, pl.ds(tok + rr, chunk, stride=SSM_BLOCK), :]
                             for rr in range(SLAB_GROUPS)])
                dests.append((half, c))
        for (half, c), vs in zip(dests, _transpose_pieces(sets, piece)):
            for gl, v in enumerate(vs):
                ub_ref[s, gl, half, c * chunk:(c + 1) * chunk, :] = v


def _inproj_kernel(*refs, rope):
    if rope:
        (x_ref, mod_ref, g1_ref, w_ref, qg_ref, kg_ref, ones_ref, cos_ref, sin_ref,
         q_ref, k_ref, v_ref, u_ref, ub_ref, u_rows) = refs
    else:
        (x_ref, mod_ref, g1_ref, w_ref, qg_ref, kg_ref, ones_ref,
         q_ref, k_ref, v_ref, u_ref, ub_ref, u_rows) = refs
    mod = mod_ref[0]
    h = (_rms(x_ref[...], g1_ref[...]) * (1.0 + mod[1:2]) + mod[0:1]).astype(BF16)
    qkv_cols = D_ATTN + 2 * D_KV
    u = jnp.dot(h, w_ref[:, qkv_cols:], preferred_element_type=F32)
    u_ref[...] = u
    for s in range(u_rows.shape[0]):
        u_rows[s] = u[:, s * LANES:(s + 1) * LANES]
    _blocks_from_tokens(u_rows, ub_ref)
    proj = jnp.dot(h, w_ref[:, :qkv_cols], preferred_element_type=F32)
    q = _head_rms(proj[:, :D_ATTN], qg_ref[...], ones_ref)
    k = _head_rms(proj[:, D_ATTN:D_ATTN + D_KV], kg_ref[...], ones_ref)
    if rope:
        q = _rope(q, cos_ref[...], sin_ref[...])
        k = _rope(k, cos_ref[:, 0:D_KV], sin_ref[:, 0:D_KV])
    q_ref[...] = (q * (HEAD_DIM ** -0.5)).astype(BF16)
    k_ref[...] = k
    v_ref[...] = proj[:, D_ATTN + D_KV:qkv_cols]


def _inproj_call(x2d, mods, g1, w_in_b, qg, kg, ones_bd, rope_tabs, layer, mod_row, tokens_per_batch,
                 tile):
    n_tok, d_model = x2d.shape
    d_in = w_in_b.shape[-1]
    d_ssm = d_in - D_ATTN - 2 * D_KV
    n_slab = d_ssm // LANES
    tiles_per_batch = None if tokens_per_batch is None else tokens_per_batch // tile
    in_specs = [
        pl.BlockSpec((tile, d_model), lambda i: (i, 0)),
        _mod_spec(d_model, layer, mod_row, tiles_per_batch),
        _layer_spec((1, d_model), layer),
        _layer_spec((d_model, d_in), layer),
        _layer_spec((1, D_ATTN), layer),
        _layer_spec((1, D_KV), layer),
        _const_spec((D_ATTN, D_ATTN)),
    ]
    args = [x2d, mods, g1, w_in_b, qg, kg, ones_bd]
    if rope_tabs is not None:
        in_specs += [pl.BlockSpec((tile, D_ATTN), lambda i: (i % tiles_per_batch, 0))] * 2
        args += list(rope_tabs)
    row = lambda width: pl.BlockSpec((tile, width), lambda i: (i, 0))
    return pl.pallas_call(
        functools.partial(_inproj_kernel, rope=rope_tabs is not None),
        grid=(n_tok // tile,),
        in_specs=in_specs,
        out_specs=(row(D_ATTN), row(D_KV), row(D_KV), row(d_ssm),
                   pl.BlockSpec((n_slab, SLAB_GROUPS, 2, tile // SSM_BLOCK, LANES),
                                lambda i: (0, 0, 0, i, 0))),
        out_shape=(jax.ShapeDtypeStruct((n_tok, D_ATTN), BF16),
                   jax.ShapeDtypeStruct((n_tok, D_KV), F32),
                   jax.ShapeDtypeStruct((n_tok, D_KV), F32),
                   jax.ShapeDtypeStruct((n_tok, d_ssm), F32),
                   jax.ShapeDtypeStruct((n_slab, SLAB_GROUPS, 2, n_tok // SSM_BLOCK, LANES), F32)),
        scratch_shapes=[pltpu.VMEM((n_slab, tile, LANES), F32)],
        compiler_params=pltpu.CompilerParams(
            dimension_semantics=("arbitrary",), vmem_limit_bytes=VMEM_LIMIT),
        name="inproj_rope" if rope_tabs is not None else "inproj",
    )(*args)


def _attn_kernel(*refs, has_cache, nb):
    if has_cache:
        q_ref, kn_ref, vn_ref, ck_ref, cv_ref, o_ref, kvar, vvar = refs
    else:
        q_ref, kn_ref, vn_ref, o_ref, kvar, vvar = refs

    @pl.when(pl.program_id(1) == 0)
    def _():
        for bi in range(nb):
            for new_ref, cache_ref, dst in ((kn_ref, ck_ref if has_cache else None, kvar),
                                            (vn_ref, cv_ref if has_cache else None, vvar)):
                src = new_ref[bi]
                if has_cache:
                    src = jnp.concatenate([cache_ref[bi], src], axis=0)
                low = lax.broadcasted_iota(jnp.int32, src.shape, 1) < HEAD_DIM
                head0 = jnp.where(low, src, 0.0)
                head1 = jnp.where(low, 0.0, src)
                dst[4 * bi + 0] = head0.astype(BF16)
                dst[4 * bi + 1] = pltpu.roll(head0, HEAD_DIM, axis=1).astype(BF16)
                dst[4 * bi + 2] = pltpu.roll(head1, HEAD_DIM, axis=1).astype(BF16)
                dst[4 * bi + 3] = head1.astype(BF16)

    slabs = D_ATTN // LANES
    for bi in range(nb):
        for slab in range(slabs):
            qs = q_ref[bi, :, slab * LANES:(slab + 1) * LANES]
            kv = slab // (slabs // N_KV_HEADS)
            acc = None
            for half in range(2):
                idx = 4 * bi + kv * 2 + half
                s = lax.dot_general(qs, kvar[idx], _NT, preferred_element_type=F32)
                m = jnp.max(s, axis=-1, keepdims=True)
                p = jnp.exp(s - m)
                denom = jnp.sum(p, axis=-1, keepdims=True)
                o = jnp.dot(p.astype(BF16), vvar[idx], preferred_element_type=F32) / denom
                acc = o if acc is None else acc + o
            o_ref[bi, :, slab * LANES:(slab + 1) * LANES] = acc.astype(BF16)


def _attn_call(q3, k3, v3, ck3, cv3, tq, nb):
    b, l, _ = q3.shape
    has_cache = ck3 is not None
    s_len = l + (ck3.shape[1] if has_cache else 0)
    in_specs = [
        pl.BlockSpec((nb, tq, D_ATTN), lambda bi, qi: (bi, qi, 0)),
        pl.BlockSpec((nb, l, D_KV), lambda bi, qi: (bi, 0, 0)),
        pl.BlockSpec((nb, l, D_KV), lambda bi, qi: (bi, 0, 0)),
    ]
    args = [q3, k3, v3]
    if has_cache:
        in_specs += [pl.BlockSpec((nb, ck3.shape[1], D_KV), lambda bi, qi: (bi, 0, 0))] * 2
        args += [ck3, cv3]
    return pl.pallas_call(
        functools.partial(_attn_kernel, has_cache=has_cache, nb=nb),
        grid=(b // nb, l // tq),
        in_specs=in_specs,
        out_specs=pl.BlockSpec((nb, tq, D_ATTN), lambda bi, qi: (bi, qi, 0)),
        out_shape=jax.ShapeDtypeStruct((b, l, D_ATTN), BF16),
        scratch_shapes=[pltpu.VMEM((4 * nb, s_len, D_KV), BF16), pltpu.VMEM((4 * nb, s_len, D_KV), BF16)],
        compiler_params=pltpu.CompilerParams(
            dimension_semantics=("arbitrary", "arbitrary"), vmem_limit_bytes=VMEM_LIMIT),
        name="attn_cache" if has_cache else "attn",
    )(*args)


def _dot3(a, b):
    a_hi = a.astype(BF16)
    b_hi = b.astype(BF16)
    a_lo = (a - a_hi.astype(F32)).astype(BF16)
    b_lo = (b - b_hi.astype(F32)).astype(BF16)
    dot = lambda x, y: lax.dot_general(x, y, _NT, preferred_element_type=F32)
    return dot(a_hi, b_hi) + (dot(a_hi, b_lo) + dot(a_lo, b_hi))


def _ssm_prep_kernel(are_ref, aim_ref, ldt_ref, bre_ref, bim_ref, cre_ref, cim_ref,
                     tab_ref, wx_ref, wxs_ref, vt_ref, tt_ref, clm, lbm):
    n_groups = wx_ref.shape[0]
    width = SSM_BLOCK * SSM_GROUP
    shape = (SSM_GROUP, LANES)
    low = lax.broadcasted_iota(jnp.int32, shape, 1) < SSM_STATE
    row_blk = lax.broadcasted_iota(jnp.int32, (width, width), 0) // SSM_GROUP
    col_blk = lax.broadcasted_iota(jnp.int32, (width, width), 1) // SSM_GROUP

    def cmul(xr, xi, yr, yi):
        return xr * yr - xi * yi, xr * yi + xi * yr

    def group(g, carry):
        rows = pl.ds(pl.multiple_of(g * SSM_GROUP, SSM_GROUP), SSM_GROUP)
        tile = None
        for d in range(2):
            lanes = slice(d * LANES, (d + 1) * LANES)
            dt = jnp.exp(ldt_ref[d, rows, :])
            ar = are_ref[d, rows, :]
            ai = aim_ref[d, rows, :]
            mag = jnp.exp(ar * dt)
            lr = mag * jnp.cos(ai * dt)
            li = mag * jnp.sin(ai * dt)
            den = ar * ar + ai * ai
            zr = ((lr - 1.0) * ar + li * ai) / den
            zi = (li * ar - (lr - 1.0) * ai) / den
            bbr, bbi = cmul(zr, zi, bre_ref[d, rows, :], bim_ref[d, rows, :])
            cr = cre_ref[d, rows, :]
            ci = cim_ref[d, rows, :]
            inv = 1.0 / (lr * lr + li * li)
            ir = lr * inv
            ii = -li * inv
            pr, pi = jnp.ones(shape, F32), jnp.zeros(shape, F32)
            qr, qi = pr, pi
            for j in range(SSM_BLOCK + 1):
                c_re, c_im = cmul(cr, ci, pr, pi)
                c_pos = jnp.where(low, c_re, -c_im)
                if j >= 1:
                    vt_ref[g, (j - 1) if d == 0 else (SSM_BLOCK - j), :, lanes] = c_pos.astype(BF16)
                if j < SSM_BLOCK:
                    g_re, g_im = cmul(pr, pi, bbr, bbi)
                    g_pos = jnp.where(low, g_re, g_im)
                    r = (SSM_BLOCK - 1 - j) if d == 0 else j
                    wx_ref[g, r, :, lanes] = g_pos.astype(BF16)
                    wxs_ref[g, r, :, lanes] = jnp.where(low, g_im, g_re).astype(BF16)
                    blk = pl.ds(j * SSM_GROUP, SSM_GROUP)
                    if d == 0:
                        n_re, n_im = cmul(qr, qi, bbr, bbi)
                        clm[blk, :] = c_pos
                        lbm[blk, :] = jnp.where(low, n_re, n_im)
                    else:
                        m_re, m_im = cmul(cr, ci, qr, qi)
                        clm[blk, :] = jnp.where(low, m_re, -m_im)
                        lbm[blk, :] = g_pos
                else:
                    tab_ref[g, d, 0] = pr[0:SUBLANES]
                    tab_ref[g, d, 1] = jnp.where(low, -pi, pi)[0:SUBLANES]
                pr, pi = cmul(pr, pi, lr, li)
                qr, qi = cmul(qr, qi, ir, ii)
            t = _dot3(clm[...], lbm[...])
            t = jnp.where((col_blk <= row_blk) if d == 0 else (col_blk >= row_blk), t, 0.0)
            tile = t if tile is None else tile + t
        tt_ref[g] = tile.astype(BF16)
        return carry

    lax.fori_loop(0, n_groups, group, 0, unroll=4)


def _ssm_prep_call(a_re, a_im, log_dt, b_re, b_im, c_re, c_im):
    depth, ndir, g, p, h = b_re.shape
    halves = 2
    g_half = g // halves
    rows = g_half * h
    width = SSM_BLOCK * h

    def rows_lanes(a):
        a = a.astype(F32).reshape(depth, ndir, g * h, p)
        return jnp.concatenate([a, a], axis=-1)

    per_state = lambda a: rows_lanes(jnp.broadcast_to(a[:, :, :, None, :], (depth, ndir, g, h, p)))
    ldt = rows_lanes(jnp.broadcast_to(log_dt[:, :, :, None, None], (depth, ndir, g, h, p)))
    to_hp = lambda a: rows_lanes(jnp.swapaxes(a, -1, -2))
    in_spec = pl.BlockSpec((None, ndir, rows, LANES), lambda l, s: (l, 0, s, 0))
    out_blk = lambda *shape: pl.BlockSpec((None, g_half) + shape, lambda l, s: (l, s) + (0,) * len(shape))
    op4 = jax.ShapeDtypeStruct((depth, g, SSM_BLOCK, h, 2 * LANES), BF16)
    tab, wx, wxs, vt, tt = pl.pallas_call(
        _ssm_prep_kernel,
        grid=(depth, halves),
        in_specs=[in_spec] * 7,
        out_specs=(out_blk(2, 2, SUBLANES, LANES), out_blk(SSM_BLOCK, h, 2 * LANES),
                   out_blk(SSM_BLOCK, h, 2 * LANES), out_blk(SSM_BLOCK, h, 2 * LANES),
                   out_blk(width, width)),
        out_shape=(jax.ShapeDtypeStruct((depth, g, 2, 2, SUBLANES, LANES), F32), op4, op4, op4,
                   jax.ShapeDtypeStruct((depth, g, width, width), BF16)),
        scratch_shapes=[pltpu.VMEM((width, LANES), F32), pltpu.VMEM((width, LANES), F32)],
        compiler_params=pltpu.CompilerParams(
            dimension_semantics=("arbitrary", "arbitrary"), vmem_limit_bytes=VMEM_LIMIT),
        name="ssm_prepare",
    )(per_state(a_re), per_state(a_im), ldt, to_hp(b_re), to_hp(b_im), rows_lanes(c_re), rows_lanes(c_im))
    merge = lambda a: a.reshape(depth, g, width, 2 * LANES)
    return tab, merge(wx), merge(wxs), tt, merge(vt)


def _transpose_pieces(sets, piece):
    sets = [list(vs) for vs in sets]
    n = len(sets[0])
    s = n // 2
    while s:
        keep_low = (piece & s) == 0
        pairs = [(vs, i) for vs in sets for i in range(n) if not i & s]
        moved = [(pltpu.roll(vs[i + s], s * SSM_GROUP, axis=1),
                  pltpu.roll(vs[i], LANES - s * SSM_GROUP, axis=1)) for vs, i in pairs]
        for (vs, i), (from_hi, from_lo) in zip(pairs, moved):
            vs[i], vs[i + s] = jnp.where(keep_low, vs[i], from_hi), jnp.where(keep_low, from_lo, vs[i + s])
        s //= 2
    return sets


def _ssm_kernel(u_ref, h0_ref, tab_ref, wx_ref, wxs_ref, tt_ref, vt_ref, y_ref, ht_ref, ug, xb, xsb, yg,
                *, batch, seq, pitch):
    m_blk = seq // SSM_BLOCK

    @pl.when(pl.program_id(0) == 0)
    def _():
        ug[...] = jnp.zeros(ug.shape, F32)

    for gl in range(SLAB_GROUPS):
        for half in range(2):
            for b in range(batch):
                ug[gl, half, b * pitch:b * pitch + m_blk, :] = u_ref[gl, half, b * m_blk:(b + 1) * m_blk, :]
        lhs = jnp.concatenate([ug[gl, 0], ug[gl, 1]], axis=1).astype(BF16)
        x = jnp.dot(lhs, wx_ref[gl], preferred_element_type=F32)
        xb[gl, 0] = x[:, :LANES]
        xb[gl, 1] = x[:, LANES:]
        x = jnp.dot(lhs, wxs_ref[gl], preferred_element_type=F32)
        xsb[gl, 0] = x[:, :LANES]
        xsb[gl, 1] = x[:, LANES:]
        y_in = lax.dot_general(lhs, tt_ref[gl], _NT, preferred_element_type=F32)
        yg[gl, 0] = y_in[:, :LANES]
        yg[gl, 1] = y_in[:, LANES:]

    n_bt = -(-batch // SUBLANES)
    rows_per = min(batch, SUBLANES)
    per_pass = max(1, 4 // n_bt)
    for g0 in range(0, SLAB_GROUPS, per_pass):
        keys = [(gl, d, bt) for gl in range(g0, g0 + per_pass) for d in range(2) for bt in range(n_bt)]
        lane0 = lambda gl, d: (gl * 2 + d) * LANES
        init = []
        for gl, d, bt in keys:
            h = h0_ref[bt * SUBLANES:bt * SUBLANES + rows_per, lane0(gl, d):lane0(gl, d) + LANES]
            init += [h, pltpu.roll(h, SSM_STATE, axis=1)]
        mult = {(gl, d): (tab_ref[gl, d, 0, 0:rows_per, :], tab_ref[gl, d, 1, 0:rows_per, :])
                for gl, d, _ in keys}

        def step(m, hs, keys=keys, mult=mult):
            out = []
            for k, (gl, d, bt) in enumerate(keys):
                h, h_sw = hs[2 * k], hs[2 * k + 1]
                row = m if d == 0 else m_blk - 1 - m
                idx = pl.ds(bt * SUBLANES * pitch + row, rows_per, stride=pitch)
                x = xb[gl, d, idx, :]
                x_sw = xsb[gl, d, idx, :]
                ug[gl, d, idx, :] = h
                a, b = mult[(gl, d)]
                out += [a * h + b * h_sw + x, a * h_sw - b * h + x_sw]
            return tuple(out)

        final = lax.fori_loop(0, m_blk, step, tuple(init), unroll=2)
        for k, (gl, d, bt) in enumerate(keys):
            ht_ref[bt * SUBLANES:bt * SUBLANES + rows_per, lane0(gl, d):lane0(gl, d) + LANES] = final[2 * k]

    for gl in range(SLAB_GROUPS):
        states = jnp.concatenate([ug[gl, 0], ug[gl, 1]], axis=1).astype(BF16)
        y_st = lax.dot_general(states, vt_ref[gl], _NT, preferred_element_type=F32)
        for half in range(2):
            total = yg[gl, half] + y_st[:, half * LANES:(half + 1) * LANES]
            for b in range(batch):
                y_ref[gl, half, b * m_blk:(b + 1) * m_blk, :] = total[b * pitch:b * pitch + m_blk]


def _ssm_call(u_blk, h0, tab, wx, wxs, tt, vt, layer, batch, seq):
    n_slab = u_blk.shape[0]
    m_blk = seq // SSM_BLOCK
    blk_spec = pl.BlockSpec((None, SLAB_GROUPS, 2, batch * m_blk, LANES), lambda s: (s, 0, 0, 0, 0))
    pitch = m_blk + SUBLANES
    rows_p = batch * pitch
    st_lanes = SLAB_GROUPS * 2 * LANES
    op_spec = pl.BlockSpec((None, SLAB_GROUPS, MXU_DIM, MXU_DIM), lambda s: (layer, s, 0, 0))
    scratch = pltpu.VMEM((SLAB_GROUPS, 2, rows_p, LANES), F32)
    return pl.pallas_call(
        functools.partial(_ssm_kernel, batch=batch, seq=seq, pitch=pitch),
        grid=(n_slab,),
        in_specs=[
            blk_spec,
            pl.BlockSpec((batch, st_lanes), lambda s: (0, s)),
            pl.BlockSpec((None, SLAB_GROUPS, 2, 2, SUBLANES, LANES), lambda s: (layer, s, 0, 0, 0, 0)),
            op_spec, op_spec, op_spec, op_spec,
        ],
        out_specs=(blk_spec,
                   pl.BlockSpec((batch, st_lanes), lambda s: (0, s))),
        out_shape=(jax.ShapeDtypeStruct((n_slab, SLAB_GROUPS, 2, batch * m_blk, LANES), F32),
                   jax.ShapeDtypeStruct((batch, n_slab * st_lanes), F32)),
        scratch_shapes=[scratch, scratch, scratch, scratch],
        compiler_params=pltpu.CompilerParams(
            dimension_semantics=("arbitrary",), vmem_limit_bytes=VMEM_LIMIT),
        name=f"ssm_scan_b{batch}",
    )(u_blk, h0, tab, wx, wxs, tt, vt)


def _tokens_from_blocks(yb_ref, rows_ref, slot):
    n_slab, _, _, n_blk, _ = yb_ref.shape
    chunk = 2 * SUBLANES
    piece = lax.broadcasted_iota(jnp.int32, (chunk, LANES), 1) // SSM_GROUP
    for s in range(n_slab):
        sets, dests = [], []
        for c in range(n_blk // chunk):
            for half in range(2):
                sets.append([yb_ref[s, gl, half, c * chunk:(c + 1) * chunk, :] for gl in range(SLAB_GROUPS)])
                dests.append(c * chunk * SSM_BLOCK + half * SLAB_GROUPS)
        for tok, vs in zip(dests, _transpose_pieces(sets, piece)):
            for rr, v in enumerate(vs):
                rows_ref[slot, s, pl.ds(tok + rr, chunk, stride=SSM_BLOCK), :] = v


def _mix_ffn_kernel(attn_ref, yb_first_ref, yb_next_ref, u_ref, x_ref, mod_ref, dskip_ref, wglu_ref,
                    bglu_ref, wout_ref, g2_ref, wffi_ref, wffo_ref, fn_ref, o_ref, y_rows,
                    *, final, ff_chunks):
    d_ff = wffo_ref.shape[0]
    i = pl.program_id(0)
    slot = i % 2

    @pl.when(i == 0)
    def _():
        _tokens_from_blocks(yb_first_ref, y_rows, 0)

    y_ssm = jnp.concatenate([y_rows[slot, s] for s in range(y_rows.shape[1])], axis=1)
    _tokens_from_blocks(yb_next_ref, y_rows, 1 - slot)
    y = y_ssm + dskip_ref[...] * u_ref[...]
    g = 0.5 * y * (1.0 + lax.erf(y * (2.0 ** -0.5)))
    z = jnp.dot(g.astype(BF16), wglu_ref[...], preferred_element_type=F32) + bglu_ref[...]
    ssm_out = g * jax.nn.sigmoid(z)
    mixed = (jnp.dot(attn_ref[...], wout_ref[0:D_ATTN, :], preferred_element_type=F32)
             + jnp.dot(ssm_out.astype(BF16), wout_ref[D_ATTN:, :], preferred_element_type=F32))
    mod = mod_ref[0]
    x1 = x_ref[...] + mod[2:3] * mixed
    h2 = (_rms(x1, g2_ref[...]) * (1.0 + mod[4:5]) + mod[3:4]).astype(BF16)
    tiles = d_ff // MXU_DIM
    edges = [MXU_DIM * ((tiles * ci) // ff_chunks) for ci in range(ff_chunks)] + [d_ff]
    acc = None
    for c0, c1 in zip(edges[:-1], edges[1:]):
        gate = jnp.dot(h2, wffi_ref[:, c0:c1], preferred_element_type=F32)
        up = jnp.dot(h2, wffi_ref[:, d_ff + c0:d_ff + c1], preferred_element_type=F32)
        act = (gate * jax.nn.sigmoid(gate) * up).astype(BF16)
        part = jnp.dot(act, wffo_ref[c0:c1, :], preferred_element_type=F32)
        acc = part if acc is None else acc + part
    x2 = x1 + mod[5:6] * acc
    if final:
        x2 = _rms(x2, fn_ref[...])
    o_ref[...] = x2


def _mix_ffn_call(attn, y, u, x2d, mods, dskip, wglu_b, bglu, wout_b, g2, wffi_b, wffo_b, fnorm,
                  layer, mod_row, tokens_per_batch, tile, final):
    n_tok, d_model = x2d.shape
    d_ssm = u.shape[1]
    d_ff = wffo_b.shape[-2]
    tiles_per_batch = None if tokens_per_batch is None else tokens_per_batch // tile
    row = lambda width: pl.BlockSpec((tile, width), lambda i: (i, 0))
    n_slab = y.shape[0]
    n_tiles = n_tok // tile
    y_block = (n_slab, SLAB_GROUPS, 2, tile // SSM_BLOCK, LANES)
    y_first = pl.BlockSpec(y_block, lambda i: (0, 0, 0, 0, 0))
    y_next = pl.BlockSpec(y_block, lambda i: (0, 0, 0, jnp.minimum(i + 1, n_tiles - 1), 0))
    return pl.pallas_call(
        functools.partial(_mix_ffn_kernel, final=final, ff_chunks=2),
        grid=(n_tiles,),
        in_specs=[
            row(D_ATTN), y_first, y_next, row(d_ssm), row(d_model),
            _mod_spec(d_model, layer, mod_row, tiles_per_batch),
            _layer_spec((1, d_ssm), layer),
            _layer_spec((d_ssm, d_ssm), layer),
            _layer_spec((1, d_ssm), layer),
            _layer_spec((D_ATTN + d_ssm, d_model), layer),
            _layer_spec((1, d_model), layer),
            _layer_spec((d_model, 2 * d_ff), layer),
            _layer_spec((d_ff, d_model), layer),
            _const_spec((1, d_model)),
        ],
        out_specs=row(d_model),
        out_shape=jax.ShapeDtypeStruct((n_tok, d_model), F32),
        scratch_shapes=[pltpu.VMEM((2, n_slab, tile, LANES), F32)],
        compiler_params=pltpu.CompilerParams(
            dimension_semantics=("arbitrary",), vmem_limit_bytes=VMEM_LIMIT),
        name="mix_ffn_final" if final else "mix_ffn",
    )(attn, y, y, u, x2d, mods, dskip, wglu_b, bglu, wout_b, g2, wffi_b, wffo_b, fnorm)


def _rope_tables(n_tokens):
    axis_dim = HEAD_DIM // 2
    rows = n_tokens // GRID_W
    row = jnp.repeat(jnp.arange(rows, dtype=F32), GRID_W)
    col = jnp.tile(jnp.arange(GRID_W, dtype=F32), rows)
    inv_freq = ROPE_THETA ** (-jnp.arange(0, axis_dim, 2, dtype=F32) / axis_dim)
    ang = jnp.concatenate([row[:, None] * inv_freq, col[:, None] * inv_freq], axis=-1)
    cos = jnp.repeat(jnp.cos(ang), 2, axis=-1)
    sin = jnp.repeat(jnp.sin(ang), 2, axis=-1) * jnp.tile(jnp.array([-1.0, 1.0], F32), HEAD_DIM // 2)
    return jnp.tile(cos, (1, N_Q_HEADS)), jnp.tile(sin, (1, N_Q_HEADS))


def _states_to_lanes(st):
    return st.transpose(0, 3, 1, 2, 4).reshape(st.shape[0], -1)


def _lanes_to_states(rows, n_groups):
    return rows.reshape(rows.shape[0], n_groups, 2, 2, SSM_STATE).transpose(0, 2, 3, 1, 4)


def kernel(x_prompt, x_sample, cache_k, cache_v, state_ssm, c, c_ctx, w_mod, b_mod, norm1, norm2, w_in, q_norm, k_norm, ssm_a_re, ssm_a_im, ssm_log_dt, ssm_b_re, ssm_b_im, ssm_c_re, ssm_c_im, ssm_d, w_glu, b_glu, w_out, w_ffn_in, w_ffn_out, final_norm):
    batch, seq, d_model = x_prompt.shape
    dec_batch, dec_seq, _ = x_sample.shape
    depth = w_in.shape[0]
    past = cache_k.shape[2]
    n_groups = ssm_a_re.shape[2]
    d_ssm = n_groups * SSM_GROUP

    cond = jnp.zeros((SUBLANES, d_model), F32).at[0].set(c_ctx).at[1:1 + dec_batch].set(c)
    mods = _mods_call(cond, w_mod, b_mod).reshape(depth, SUBLANES, N_MOD, d_model)
    tab, wx, wxs, tt, vt = _ssm_prep_call(ssm_a_re, ssm_a_im, ssm_log_dt, ssm_b_re, ssm_b_im,
                                          ssm_c_re, ssm_c_im)
    rope_tabs = _rope_tables(dec_seq)
    head_ids = jnp.arange(D_ATTN) // HEAD_DIM
    ones_bd = (head_ids[:, None] == head_ids[None, :]).astype(BF16)

    xp = x_prompt.reshape(batch * seq, d_model)
    xs = x_sample.reshape(dec_batch * dec_seq, d_model)
    zero_state = jnp.zeros((batch, n_groups * 2 * LANES), F32)
    w_in_b = w_in.astype(BF16)
    wglu_b = w_glu.astype(BF16)
    wout_b = w_out.astype(BF16)
    wffi_b = w_ffn_in.astype(BF16)
    wffo_b = w_ffn_out.astype(BF16)
    g1 = norm1.reshape(depth, 1, d_model)
    g2 = norm2.reshape(depth, 1, d_model)
    qg = jnp.tile(q_norm, (1, N_Q_HEADS)).reshape(depth, 1, D_ATTN)
    kg = jnp.tile(k_norm, (1, N_KV_HEADS)).reshape(depth, 1, D_KV)
    dskip = ssm_d.reshape(depth, 1, d_ssm)
    bglu = b_glu.reshape(depth, 1, d_ssm)
    fnorm = final_norm.reshape(1, d_model)
    new_k, new_v, new_s = [], [], []
    for l in range(depth):
        final = l == depth - 1
        for is_ctx in (True, False):
            if is_ctx:
                x2d, n_b, n_l, tq, attn_nb = xp, batch, seq, seq, 8
                mod_row, tokens_per_batch, tabs, ck, cv, h0 = 0, None, None, None, None, zero_state
            else:
                x2d, n_b, n_l, tq, attn_nb = xs, dec_batch, dec_seq, 512, 1
                mod_row, tokens_per_batch, tabs = 1, dec_seq, rope_tabs
                ck = cache_k[:, l].reshape(dec_batch, past, D_KV)
                cv = cache_v[:, l].reshape(dec_batch, past, D_KV)
                h0 = _states_to_lanes(state_ssm[:, l])
            q, k, v, u, u_blk = _inproj_call(x2d, mods, g1, w_in_b, qg, kg, ones_bd, tabs, l, mod_row,
                                             tokens_per_batch, 2 * TOKEN_TILE)
            attn = _attn_call(q.reshape(n_b, n_l, D_ATTN), k.reshape(n_b, n_l, D_KV),
                              v.reshape(n_b, n_l, D_KV), ck, cv, tq, attn_nb)
            y, ht = _ssm_call(u_blk, h0, tab, wx, wxs, tt, vt, l, n_b, n_l)
            x_new = _mix_ffn_call(attn.reshape(-1, D_ATTN), y, u, x2d, mods, dskip, wglu_b, bglu, wout_b,
                                  g2, wffi_b, wffo_b, fnorm, l, mod_row, tokens_per_batch, TOKEN_TILE,
                                  final)
            if is_ctx:
                xp = x_new
                new_k.append(k.reshape(batch, seq, N_KV_HEADS, HEAD_DIM))
                new_v.append(v.reshape(batch, seq, N_KV_HEADS, HEAD_DIM))
                new_s.append(_lanes_to_states(ht, n_groups))
            else:
                xs = x_new
    return (xp.reshape(batch, seq, d_model), xs.reshape(dec_batch, dec_seq, d_model),
            jnp.stack(new_k, axis=1), jnp.stack(new_v, axis=1), jnp.stack(new_s, axis=1))
```
